```python
import math
import jax, jax.numpy as jnp
from jax import lax
import numpy as np

D_MODEL = 4096
BATCH = 2
SEQ = 8192
DEPTH = 2

N_A_LAYERS = DEPTH // 2
N_B_LAYERS = DEPTH - N_A_LAYERS
HEAD_DIM = 128
DIFF_HEADS = D_MODEL // (2 * HEAD_DIM)
DIFF_MAPS = 2 * DIFF_HEADS
SWA_Q_HEADS = D_MODEL // HEAD_DIM
SWA_KV_HEADS = SWA_Q_HEADS // 4
SWA_GROUP = SWA_Q_HEADS // SWA_KV_HEADS
WINDOW = 128
BLOCK = 128
D_FF = -(-8 * D_MODEL // (3 * 256)) * 256
NUM_BUCKETS = 32
MAX_DISTANCE = 128
BIAS_HEADS = SWA_Q_HEADS
EPS = 1e-6
NEG = -1e30

kernel_name = 'hybrid_diffattn_swa_sinks_yoco'


def rms_norm(x, g):
    xf = x.astype(jnp.float32)
    y = xf * lax.rsqrt(jnp.mean(xf * xf, axis=-1, keepdims=True) + EPS)
    return (y * g.astype(jnp.float32)).astype(x.dtype)


def t5_bucket(n):
    max_exact = NUM_BUCKETS // 2
    nf = jnp.maximum(n, 1).astype(jnp.float32)
    large = max_exact + (jnp.log(nf / max_exact) / math.log(MAX_DISTANCE / max_exact)
                         * (NUM_BUCKETS - max_exact)).astype(jnp.int32)
    large = jnp.minimum(large, NUM_BUCKETS - 1)
    return jnp.where(n < max_exact, n, large)


def distance_bias_table(rel_bias):
    return rel_bias[t5_bucket(jnp.arange(MAX_DISTANCE))]


def bias_from_distance(lut, dist):
    return jnp.moveaxis(lut[jnp.clip(dist, 0, MAX_DISTANCE - 1)], -1, 0).astype(jnp.float32)


def swiglu(h, w_gate, w_up, w_down):
    return (jax.nn.silu(h @ w_gate) * (h @ w_up)) @ w_down


def diff_attention(h, w_qkv, w_o, g_q, g_k, lam_qk, g_sub, lut, lambda_init):
    B, S, _ = h.shape
    nb = S // BLOCK
    q, k, v = jnp.split(h @ w_qkv, 3, axis=-1)
    q = rms_norm(q.reshape(B, S, DIFF_MAPS, HEAD_DIM), g_q)
    k = rms_norm(k.reshape(B, S, DIFF_MAPS, HEAD_DIM), g_k)
    v = v.reshape(B, S, DIFF_HEADS, 2 * HEAD_DIM)
    lf = lam_qk.astype(jnp.float32)
    lam = jnp.exp(jnp.sum(lf[0] * lf[1])) - jnp.exp(jnp.sum(lf[2] * lf[3])) + lambda_init
    scale = HEAD_DIM ** -0.5
    q_blocks = q.reshape(B, nb, BLOCK, DIFF_MAPS, HEAD_DIM).transpose(1, 0, 2, 3, 4)
    key_pos = jnp.arange(S)

    def one_block(args):
        qb, bi = args
        logits = jnp.einsum('bqmd,bkmd->bmqk', qb, k,
                            preferred_element_type=jnp.float32) * scale
        dist = (bi * BLOCK + jnp.arange(BLOCK))[:, None] - key_pos[None, :]
        logits = logits + bias_from_distance(lut, dist)[None]
        logits = jnp.where(dist >= 0, logits, NEG)
        p = jax.nn.softmax(logits, axis=-1).reshape(B, DIFF_HEADS, 2, BLOCK, S)
        a = p[:, :, 0] - lam * p[:, :, 1]
        return jnp.einsum('bhqk,bkhe->bqhe', a.astype(v.dtype), v)

    o = lax.map(one_block, (q_blocks, jnp.arange(nb)))
    o = o.transpose(1, 0, 2, 3, 4).reshape(B, S, DIFF_HEADS, 2 * HEAD_DIM)
    o = rms_norm(o, g_sub) * (1.0 - lambda_init)
    return o.reshape(B, S, D_MODEL) @ w_o


def swa_sinks_attention(h, w_q, w_o, g_q, k, v, sinks, lut):
    B, S, _ = h.shape
    nb = S // BLOCK
    q = rms_norm((h @ w_q).reshape(B, S, SWA_Q_HEADS, HEAD_DIM), g_q)
    q = q.reshape(B, nb, BLOCK, SWA_KV_HEADS, SWA_GROUP, HEAD_DIM)

    def with_prev(t):
        t = t.reshape(B, nb, BLOCK, SWA_KV_HEADS, HEAD_DIM)
        prev = jnp.pad(t[:, :-1], ((0, 0), (1, 0), (0, 0), (0, 0), (0, 0)))
        return jnp.concatenate([prev, t], axis=2)

    kk, vv = with_prev(k), with_prev(v)
    logits = jnp.einsum('bnqgrd,bnkgd->bngrqk', q, kk,
                        preferred_element_type=jnp.float32) * (HEAD_DIM ** -0.5)
    qi = jnp.arange(BLOCK)[:, None]
    kj = jnp.arange(2 * BLOCK)[None, :]
    dist = BLOCK + qi - kj
    bias = bias_from_distance(lut, dist).reshape(SWA_KV_HEADS, SWA_GROUP, BLOCK, 2 * BLOCK)
    logits = logits + bias
    valid = ((dist >= 0) & (dist < WINDOW))[None] & (
        (jnp.arange(nb)[:, None, None] > 0) | (kj[None] >= BLOCK))
    logits = jnp.where(valid[None, :, None, None], logits, NEG)
    sink = jnp.broadcast_to(
        sinks.astype(jnp.float32).reshape(1, 1, SWA_KV_HEADS, SWA_GROUP, 1, 1),
        logits.shape[:-1] + (1,))
    p = jax.nn.softmax(jnp.concatenate([logits, sink], axis=-1), axis=-1)[..., :-1]
    o = jnp.einsum('bngrqk,bnkgd->bnqgrd', p.astype(vv.dtype), vv)
    return o.reshape(B, S, D_MODEL) @ w_o


def setup_inputs(seed: int = 0) -> dict:
    key = jax.random.key(seed)
    ks = jax.random.split(key, 24)
    f32 = jnp.float32

    def w(k, shape, fan_in):
        return jax.random.normal(k, shape, f32) * (fan_in ** -0.5)

    def gain(k, shape):
        return 1.0 + 0.02 * jax.random.normal(k, shape, f32)

    kv_width = SWA_KV_HEADS * HEAD_DIM
    return {
        'x': jax.random.normal(ks[0], (BATCH, SEQ, D_MODEL), f32),
        'rel_bias': 0.5 * jax.random.normal(ks[1], (NUM_BUCKETS, BIAS_HEADS), f32),
        'g_attn_norm': gain(ks[2], (DEPTH, D_MODEL)),
        'g_ffn_norm': gain(ks[3], (DEPTH, D_MODEL)),
        'w_qkv_a': w(ks[4], (N_A_LAYERS, D_MODEL, 3 * D_MODEL), D_MODEL),
        'w_o_a': w(ks[5], (N_A_LAYERS, D_MODEL, D_MODEL), D_MODEL),
        'g_q_a': gain(ks[6], (N_A_LAYERS, HEAD_DIM)),
        'g_k_a': gain(ks[7], (N_A_LAYERS, HEAD_DIM)),
        'lam_qk_a': 0.1 * jax.random.normal(ks[8], (N_A_LAYERS, 4, HEAD_DIM), f32),
        'g_sub_a': gain(ks[9], (N_A_LAYERS, 2 * HEAD_DIM)),
        'g_kv_norm': gain(ks[10], (D_MODEL,)),
        'w_kv': w(ks[11], (D_MODEL, 2 * kv_width), D_MODEL),
        'g_k_shared': gain(ks[12], (HEAD_DIM,)),
        'w_q_b': w(ks[13], (N_B_LAYERS, D_MODEL, D_MODEL), D_MODEL),
        'w_o_b': w(ks[14], (N_B_LAYERS, D_MODEL, D_MODEL), D_MODEL),
        'g_q_b': gain(ks[15], (N_B_LAYERS, HEAD_DIM)),
        'sinks_b': 0.5 * jax.random.normal(ks[16], (N_B_LAYERS, SWA_Q_HEADS), f32),
        'w_gate': w(ks[17], (DEPTH, D_MODEL, D_FF), D_MODEL),
        'w_up': w(ks[18], (DEPTH, D_MODEL, D_FF), D_MODEL),
        'w_down': w(ks[19], (DEPTH, D_FF, D_MODEL), D_FF),
    }


def reference(x, rel_bias, g_attn_norm, g_ffn_norm, w_qkv_a, w_o_a, g_q_a, g_k_a,
              lam_qk_a, g_sub_a, g_kv_norm, w_kv, g_k_shared, w_q_b, w_o_b, g_q_b,
              sinks_b, w_gate, w_up, w_down):
    B, S, _ = x.shape
    lut = distance_bias_table(rel_bias)
    k_sh = None
    v_sh = None
    for i in range(DEPTH):
        h = rms_norm(x, g_attn_norm[i])
        if i < N_A_LAYERS:
            a = i
            lambda_init = 0.8 - 0.6 * math.exp(-0.3 * i)
            x = x + diff_attention(h, w_qkv_a[a], w_o_a[a], g_q_a[a], g_k_a[a],
                                   lam_qk_a[a], g_sub_a[a], lut, lambda_init)
        else:
            b = i - N_A_LAYERS
            x = x + swa_sinks_attention(h, w_q_b[b], w_o_b[b], g_q_b[b],
                                        k_sh, v_sh, sinks_b[b], lut)
        x = x + swiglu(rms_norm(x, g_ffn_norm[i]), w_gate[i], w_up[i], w_down[i])
        if i == N_A_LAYERS - 1:
            k_sh, v_sh = jnp.split(rms_norm(x, g_kv_norm) @ w_kv, 2, axis=-1)
            k_sh = rms_norm(k_sh.reshape(B, S, SWA_KV_HEADS, HEAD_DIM), g_k_shared)
            v_sh = v_sh.reshape(B, S, SWA_KV_HEADS, HEAD_DIM)
    return x
```

```python
import functools
import math

import jax
import jax.numpy as jnp
from jax import lax
from jax.experimental import pallas as pl
from jax.experimental.pallas import tpu as pltpu

HEAD_DIM = 128
WINDOW = 128
SWA_GROUP = 4
NUM_BUCKETS = 32
MAX_DISTANCE = 128
EPS = 1e-6
NEG = -1e30
LANES = 128
VMEM_LIMIT_BYTES = 56 * 1024 * 1024

_F32 = jnp.float32
_BF16 = jnp.bfloat16
_NT = (((1,), (1,)), ((), ()))


def _params(*sem):
    return pltpu.CompilerParams(dimension_semantics=sem, vmem_limit_bytes=VMEM_LIMIT_BYTES)


def _tiles(M, S, D, D_FF):
    return dict(
        rms_tm=min(256, M),
        mm_tm=min(1024, M),
        qkv_tn=min(1024, D),
        res_tn=min(512, D),
        gu_tn=min(256, D_FF),
        down_tk=D_FF // 2,
        down_tn=min(256, D),
        attn_t=min(512, S // 4),
        swa_tq=min(512, S),
    )


def _rms_kernel(x_ref, g_ref, *o_refs):
    x = x_ref[...]
    y = x * lax.rsqrt(jnp.mean(x * x, axis=-1, keepdims=True) + EPS)
    for n, o_ref in enumerate(o_refs):
        o_ref[...] = (y * g_ref[n:n + 1, :]).astype(o_ref.dtype)


def _rmsnorm(x, gains, tm):
    M, D = x.shape
    n = gains.shape[0]
    return pl.pallas_call(
        _rms_kernel,
        grid=(M // tm,),
        in_specs=[pl.BlockSpec((tm, D), lambda i: (i, 0)),
                  pl.BlockSpec((n, D), lambda i: (0, 0))],
        out_specs=[pl.BlockSpec((tm, D), lambda i: (i, 0))] * n,
        out_shape=[jax.ShapeDtypeStruct((M, D), _BF16)] * n,
        compiler_params=_params("arbitrary"),
        name="rmsnorm",
    )(x, gains)


def _proj_headnorm_kernel(x_ref, w_ref, g_ref, o_ref, *, n_norm_tiles, n_tiles):
    y = jnp.dot(x_ref[...], w_ref[...], preferred_element_type=_F32)
    tn = y.shape[1]

    def normed():
        for c in range(tn // HEAD_DIM):
            sl = slice(c * HEAD_DIM, (c + 1) * HEAD_DIM)
            yc = y[:, sl]
            r = lax.rsqrt(jnp.mean(yc * yc, axis=-1, keepdims=True) + EPS)
            o_ref[:, sl] = (yc * r * g_ref[:, sl]).astype(o_ref.dtype)

    def plain():
        o_ref[...] = y.astype(o_ref.dtype)

    if n_norm_tiles == n_tiles:
        normed()
    else:
        j = pl.program_id(1)
        pl.when(j < n_norm_tiles)(normed)
        pl.when(j >= n_norm_tiles)(plain)


def _proj_headnorm(x, w, g_cols, n_norm_cols, tm, tn):
    M, K = x.shape
    N = w.shape[1]
    kern = functools.partial(_proj_headnorm_kernel, n_norm_tiles=n_norm_cols // tn, n_tiles=N // tn)
    return pl.pallas_call(
        kern,
        grid=(M // tm, N // tn),
        in_specs=[pl.BlockSpec((tm, K), lambda i, j: (i, 0)),
                  pl.BlockSpec((K, tn), lambda i, j: (0, j)),
                  pl.BlockSpec((1, tn), lambda i, j: (0, j))],
        out_specs=pl.BlockSpec((tm, tn), lambda i, j: (i, j)),
        out_shape=jax.ShapeDtypeStruct((M, N), _BF16),
        compiler_params=_params("arbitrary", "arbitrary"),
        name="proj_headnorm",
    )(x, w, g_cols)


def _proj_residual_kernel(x_ref, w_ref, r_ref, o_ref):
    o_ref[...] = r_ref[...] + jnp.dot(x_ref[...], w_ref[...], preferred_element_type=_F32)


def _proj_residual(x, w, res, tm, tn):
    M, K = x.shape
    N = w.shape[1]
    return pl.pallas_call(
        _proj_residual_kernel,
        grid=(M // tm, N // tn),
        in_specs=[pl.BlockSpec((tm, K), lambda i, j: (i, 0)),
                  pl.BlockSpec((K, tn), lambda i, j: (0, j)),
                  pl.BlockSpec((tm, tn), lambda i, j: (i, j))],
        out_specs=pl.BlockSpec((tm, tn), lambda i, j: (i, j)),
        out_shape=jax.ShapeDtypeStruct((M, N), _F32),
        compiler_params=_params("arbitrary", "arbitrary"),
        name="proj_residual",
    )(x, w, res)


def _gate_up_kernel(x_ref, wg_ref, wu_ref, o_ref):
    x = x_ref[...]
    g = jnp.dot(x, wg_ref[...], preferred_element_type=_F32)
    u = jnp.dot(x, wu_ref[...], preferred_element_type=_F32)
    o_ref[...] = (g * jax.nn.sigmoid(g) * u).astype(o_ref.dtype)


def _gate_up(x, wg, wu, tm, tn):
    M, K = x.shape
    N = wg.shape[1]
    return pl.pallas_call(
        _gate_up_kernel,
        grid=(M // tm, N // tn),
        in_specs=[pl.BlockSpec((tm, K), lambda i, j: (i, 0)),
                  pl.BlockSpec((K, tn), lambda i, j: (0, j)),
                  pl.BlockSpec((K, tn), lambda i, j: (0, j))],
        out_specs=pl.BlockSpec((tm, tn), lambda i, j: (i, j)),
        out_shape=jax.ShapeDtypeStruct((M, N), _BF16),
        compiler_params=_params("arbitrary", "arbitrary"),
        name="ffn_gate_up",
    )(x, wg, wu)


def _down_kernel(x_ref, w_ref, r_ref, o_ref, acc_ref):
    k = pl.program_id(1)
    j = pl.program_id(2)
    part = jnp.dot(x_ref[...], w_ref[...], preferred_element_type=_F32)

    @pl.when(k == 0)
    def _():
        acc_ref[j] = part

    @pl.when(k == pl.num_programs(1) - 1)
    def _():
        o_ref[...] = r_ref[...] + acc_ref[j] + part


def _down_residual(x, w, res, tm, tk, tn):
    M, K = x.shape
    N = w.shape[1]
    nk = K // tk
    assert nk == 2
    return pl.pallas_call(
        _down_kernel,
        grid=(M // tm, nk, N // tn),
        in_specs=[pl.BlockSpec((tm, tk), lambda i, k, j: (i, k)),
                  pl.BlockSpec((tk, tn), lambda i, k, j: (k, j)),
                  pl.BlockSpec((tm, tn), lambda i, k, j: (i, j * k))],
        out_specs=pl.BlockSpec((tm, tn), lambda i, k, j: (i, j * k)),
        out_shape=jax.ShapeDtypeStruct((M, N), _F32),
        scratch_shapes=[pltpu.VMEM((N // tn, tm, tn), _F32)],
        compiler_params=_params("arbitrary", "arbitrary", "arbitrary"),
        name="ffn_down_residual",
    )(x, w, res)


def _diff_attn_kernel(q_ref, k_ref, v_ref, td_ref, ts_ref, lam_ref, gsub_ref, o_ref,
                      bias_ref, m_ref, l_ref, acc_ref, *, t, lambda_init):
    i = pl.program_id(2)
    nt = t // LANES

    @pl.when(i == 0)
    def _build_bias():
        for c in range(2):
            for a in range(nt):
                for b in range(nt):
                    rows = slice(a * LANES, (a + 1) * LANES)
                    cols = slice(b * LANES, (b + 1) * LANES)
                    if a == b:
                        tile = td_ref[c]
                    elif a == b + 1:
                        tile = ts_ref[c]
                    elif a > b:
                        tile = jnp.zeros((LANES, LANES), _F32)
                    else:
                        tile = jnp.full((LANES, LANES), NEG, _F32)
                    bias_ref[c, 0, rows, cols] = tile
                    corner = (a == 0 and b == nt - 1)
                    bias_ref[c, 1, rows, cols] = ts_ref[c] if corner else jnp.zeros((LANES, LANES), _F32)

    m_ref[...] = jnp.full(m_ref.shape, NEG, _F32)
    l_ref[...] = jnp.zeros(l_ref.shape, _F32)
    acc_ref[...] = jnp.zeros(acc_ref.shape, _F32)

    def step(j, kind):
        start = pl.multiple_of(j * t, t)
        vj = v_ref[pl.ds(start, t), :]
        for c in range(2):
            cols = slice(c * HEAD_DIM, (c + 1) * HEAD_DIM)
            qc = q_ref[:, cols]
            kj = k_ref[pl.ds(start, t), cols]
            s = lax.dot_general(qc, kj, _NT, preferred_element_type=_F32)
            if kind is not None:
                s = s + bias_ref[c, kind]
            m_old = m_ref[c]
            m_new = jnp.maximum(m_old, jnp.max(s, axis=-1, keepdims=True))
            alpha = jnp.exp(m_old - m_new)
            p = jnp.exp(s - m_new)
            l_ref[c] = alpha * l_ref[c] + jnp.sum(p, axis=-1, keepdims=True)
            acc_ref[c] = alpha * acc_ref[c] + jnp.dot(p.astype(_BF16), vj, preferred_element_type=_F32)
            m_ref[c] = m_new

    def far_body(j, carry):
        step(j, None)
        return carry

    lax.fori_loop(0, jnp.maximum(i - 1, 0), far_body, 0)

    @pl.when(i >= 1)
    def _():
        step(i - 1, 1)

    step(i, 0)

    lf = lam_ref[...]
    lam = (jnp.exp(jnp.sum(lf[0:1] * lf[1:2], axis=-1, keepdims=True))
           - jnp.exp(jnp.sum(lf[2:3] * lf[3:4], axis=-1, keepdims=True)) + lambda_init)
    o = acc_ref[0] / l_ref[0] - lam * (acc_ref[1] / l_ref[1])
    o = o * lax.rsqrt(jnp.mean(o * o, axis=-1, keepdims=True) + EPS)
    o_ref[...] = (o * gsub_ref[...] * (1.0 - lambda_init)).astype(o_ref.dtype)


def _diff_attention(qkv, td, ts, lam_qk, g_sub, B, S, D, t, lambda_init):
    n_heads = D // (2 * HEAD_DIM)
    nq = S // t
    w = 2 * HEAD_DIM
    kern = functools.partial(_diff_attn_kernel, t=t, lambda_init=lambda_init)
    return pl.pallas_call(
        kern,
        grid=(B, n_heads, nq),
        in_specs=[pl.BlockSpec((t, w), lambda b, h, i: (b * nq + i, h)),
                  pl.BlockSpec((S, w), lambda b, h, i: (b, n_heads + h)),
                  pl.BlockSpec((S, w), lambda b, h, i: (b, 2 * n_heads + h)),
                  pl.BlockSpec((2, LANES, LANES), lambda b, h, i: (h, 0, 0)),
                  pl.BlockSpec((2, LANES, LANES), lambda b, h, i: (h, 0, 0)),
                  pl.BlockSpec((4, HEAD_DIM), lambda b, h, i: (0, 0)),
                  pl.BlockSpec((1, w), lambda b, h, i: (0, 0))],
        out_specs=pl.BlockSpec((t, w), lambda b, h, i: (b * nq + i, h)),
        out_shape=jax.ShapeDtypeStruct((B * S, D), _BF16),
        scratch_shapes=[pltpu.VMEM((2, 2, t, t), _F32),
                        pltpu.VMEM((2, t, 1), _F32),
                        pltpu.VMEM((2, t, 1), _F32),
                        pltpu.VMEM((2, t, w), _F32)],
        compiler_params=_params("arbitrary", "arbitrary", "arbitrary"),
        name="diff_attention",
    )(qkv, qkv, qkv, td, ts, lam_qk, g_sub)


def _swa_kernel(q_ref, kc_ref, kp_ref, vc_ref, vp_ref, bias_ref, sink_ref, o_ref, *, tq):
    first = pl.program_id(2) == 0
    rows = SWA_GROUP * WINDOW
    bias = bias_ref[...].reshape(rows, 2 * WINDOW)
    sink = jnp.concatenate(
        [jnp.broadcast_to(sink_ref[r:r + 1, 0:1], (WINDOW, 1)) for r in range(SWA_GROUP)], axis=0)
    col = lax.broadcasted_iota(jnp.int32, (rows, 2 * WINDOW), 1)
    for c in range(tq // WINDOW):
        cur = slice(c * WINDOW, (c + 1) * WINDOW)
        if c == 0:
            k_prev, v_prev = kp_ref[...], vp_ref[...]
        else:
            prev = slice((c - 1) * WINDOW, c * WINDOW)
            k_prev, v_prev = kc_ref[prev, :], vc_ref[prev, :]
        kk = jnp.concatenate([k_prev, kc_ref[cur, :]], axis=0)
        vv = jnp.concatenate([v_prev, vc_ref[cur, :]], axis=0)
        q4 = jnp.concatenate(
            [q_ref[cur, r * HEAD_DIM:(r + 1) * HEAD_DIM] for r in range(SWA_GROUP)], axis=0)
        s = lax.dot_general(q4, kk, _NT, preferred_element_type=_F32) + bias
        if c == 0:
            s = jnp.where(first & (col < WINDOW), NEG, s)
        m = jnp.maximum(jnp.max(s, axis=-1, keepdims=True), sink)
        p = jnp.exp(s - m)
        denom = jnp.sum(p, axis=-1, keepdims=True) + jnp.exp(sink - m)
        o = jnp.dot(p.astype(_BF16), vv, preferred_element_type=_F32) / denom
        for r in range(SWA_GROUP):
            o_ref[cur, r * HEAD_DIM:(r + 1) * HEAD_DIM] = (
                o[r * WINDOW:(r + 1) * WINDOW, :].astype(o_ref.dtype))


def _swa_attention(q, kv, bias, sinks, B, S, D, tq):
    n_kv = D // (SWA_GROUP * HEAD_DIM)
    nt = S // tq
    bpt = tq // WINDOW
    gw = SWA_GROUP * HEAD_DIM

    def prev_map(off):
        return lambda b, g, t: (jnp.maximum((b * nt + t) * bpt - 1, 0), off + g)

    kern = functools.partial(_swa_kernel, tq=tq)
    return pl.pallas_call(
        kern,
        grid=(B, n_kv, nt),
        in_specs=[pl.BlockSpec((tq, gw), lambda b, g, t: (b * nt + t, g)),
                  pl.BlockSpec((tq, HEAD_DIM), lambda b, g, t: (b * nt + t, g)),
                  pl.BlockSpec((WINDOW, HEAD_DIM), prev_map(0)),
                  pl.BlockSpec((tq, HEAD_DIM), lambda b, g, t: (b * nt + t, n_kv + g)),
                  pl.BlockSpec((WINDOW, HEAD_DIM), prev_map(n_kv)),
                  pl.BlockSpec((SWA_GROUP, WINDOW, 2 * WINDOW), lambda b, g, t: (g, 0, 0)),
                  pl.BlockSpec((None, SWA_GROUP, LANES), lambda b, g, t: (g, 0, 0))],
        out_specs=pl.BlockSpec((tq, gw), lambda b, g, t: (b * nt + t, g)),
        out_shape=jax.ShapeDtypeStruct((B * S, D), _BF16),
        compiler_params=_params("arbitrary", "arbitrary", "arbitrary"),
        name="swa_attention",
    )(q, kv, kv, kv, kv, bias, sinks)


def _t5_bucket(n):
    max_exact = NUM_BUCKETS // 2
    nf = jnp.maximum(n, 1).astype(_F32)
    large = max_exact + (jnp.log(nf / max_exact) / math.log(MAX_DISTANCE / max_exact)
                         * (NUM_BUCKETS - max_exact)).astype(jnp.int32)
    large = jnp.minimum(large, NUM_BUCKETS - 1)
    return jnp.where(n < max_exact, n, large)


def _bias_tiles(rel_bias):
    lut = rel_bias[_t5_bucket(jnp.arange(MAX_DISTANCE))].astype(_F32)
    qi = jnp.arange(LANES)[:, None]
    kj = jnp.arange(LANES)[None, :]
    shifted = lut - lut[MAX_DISTANCE - 1]

    def gather(table, dist):
        return jnp.moveaxis(table[jnp.clip(dist, 0, MAX_DISTANCE - 1)], -1, 0)

    d0 = qi - kj
    td = jnp.where((d0 >= 0)[None], gather(shifted, d0), NEG)
    ts = gather(shifted, LANES + qi - kj)
    k2 = jnp.arange(2 * WINDOW)[None, :]
    dw = WINDOW + qi - k2
    swa = jnp.where(((dw >= 0) & (dw < WINDOW))[None], gather(lut, dw), NEG)
    return td, ts, swa


def _forward(x, rel_bias, g_attn_norm, g_ffn_norm, w_qkv_a, w_o_a, g_q_a, g_k_a, lam_qk_a, g_sub_a,
             g_kv_norm, w_kv, g_k_shared, w_q_b, w_o_b, g_q_b, sinks_b, w_gate, w_up, w_down, tiles):
    B, S, D = x.shape
    M = B * S
    T = tiles
    scale = HEAD_DIM ** -0.5
    n_maps = D // HEAD_DIM
    kvw = w_kv.shape[1] // 2
    td, ts, swa_bias = _bias_tiles(rel_bias)
    bf = lambda a: a.astype(_BF16)
    x = x.reshape(M, D)

    def ffn(x, layer):
        h = _rmsnorm(x, g_ffn_norm[layer][None], T["rms_tm"])[0]
        act = _gate_up(h, bf(w_gate[layer]), bf(w_up[layer]), T["mm_tm"], T["gu_tn"])
        return _down_residual(act, bf(w_down[layer]), x, T["mm_tm"], T["down_tk"], T["down_tn"])

    lambda_init = 0.8 - 0.6 * math.exp(-0.3 * 0)
    h = _rmsnorm(x, g_attn_norm[0][None], T["rms_tm"])[0]
    g_cols = jnp.concatenate([jnp.tile(g_q_a[0] * scale, n_maps), jnp.tile(g_k_a[0], n_maps),
                              jnp.ones((D,), _F32)])[None]
    qkv = _proj_headnorm(h, bf(w_qkv_a[0]), g_cols, 2 * D, T["mm_tm"], T["qkv_tn"])
    o = _diff_attention(qkv, td, ts, lam_qk_a[0], g_sub_a[0][None], B, S, D, T["attn_t"], lambda_init)
    x = _proj_residual(o, bf(w_o_a[0]), x, T["mm_tm"], T["res_tn"])
    x = ffn(x, 0)

    h_kv, h_q = _rmsnorm(x, jnp.stack([g_kv_norm, g_attn_norm[1]]), T["rms_tm"])
    g_kv_cols = jnp.concatenate([jnp.tile(g_k_shared, kvw // HEAD_DIM), jnp.ones((kvw,), _F32)])[None]
    kv = _proj_headnorm(h_kv, bf(w_kv), g_kv_cols, kvw, T["mm_tm"], min(T["qkv_tn"], kvw))
    q = _proj_headnorm(h_q, bf(w_q_b[0]), jnp.tile(g_q_b[0] * scale, n_maps)[None], D,
                       T["mm_tm"], T["qkv_tn"])
    sinks = jnp.broadcast_to(sinks_b[0].astype(_F32).reshape(-1, SWA_GROUP, 1),
                             (n_maps // SWA_GROUP, SWA_GROUP, LANES))
    o = _swa_attention(q, kv, swa_bias, sinks, B, S, D, T["swa_tq"])
    x = _proj_residual(o, bf(w_o_b[0]), x, T["mm_tm"], T["res_tn"])
    x = ffn(x, 1)
    return x.reshape(B, S, D)


def kernel(x, rel_bias, g_attn_norm, g_ffn_norm, w_qkv_a, w_o_a, g_q_a, g_k_a, lam_qk_a, g_sub_a,
           g_kv_norm, w_kv, g_k_shared, w_q_b, w_o_b, g_q_b, sinks_b, w_gate, w_up, w_down):
    B, S, D = x.shape
    tiles = _tiles(B * S, S, D, w_gate.shape[-1])
    return _forward(x, rel_bias, g_attn_norm, g_ffn_norm, w_qkv_a, w_o_a, g_q_a, g_k_a, lam_qk_a,
                    g_sub_a, g_kv_norm, w_kv, g_k_shared, w_q_b, w_o_b, g_q_b, sinks_b,
                    w_gate, w_up, w_down, tiles)
```

```python
import functools
import math

import jax
import jax.numpy as jnp
from jax import lax
from jax.experimental import pallas as pl
from jax.experimental.pallas import tpu as pltpu

HEAD_DIM = 128
WINDOW = 128
SWA_GROUP = 4
NUM_BUCKETS = 32
MAX_DISTANCE = 128
EPS = 1e-6
NEG = -1e30
LOG2E = math.log2(math.e)
LANES = 128
VMEM_LIMIT_BYTES = 56 * 1024 * 1024

_F32 = jnp.float32
_BF16 = jnp.bfloat16
_NT = (((1,), (1,)), ((), ()))


def _params(*sem):
    return pltpu.CompilerParams(dimension_semantics=sem, vmem_limit_bytes=VMEM_LIMIT_BYTES)


def _tiles(M, S, D, D_FF):
    return dict(
        rms_tm=min(256, M),
        mm_tm=min(1024, M),
        qkv_tn=min(1024, D),
        res_tn=min(512, D),
        gu_tn=min(256, D_FF),
        down_tk=D_FF // 2,
        down_tn=min(256, D),
        attn_t=min(512, S // 4),
        swa_tq=min(512, S),
    )


def _rms_kernel(x_ref, g_ref, *o_refs):
    x = x_ref[...]
    y = x * lax.rsqrt(jnp.mean(x * x, axis=-1, keepdims=True) + EPS)
    for n, o_ref in enumerate(o_refs):
        o_ref[...] = (y * g_ref[n:n + 1, :]).astype(o_ref.dtype)


def _rmsnorm(x, gains, tm):
    M, D = x.shape
    n = gains.shape[0]
    return pl.pallas_call(
        _rms_kernel,
        grid=(M // tm,),
        in_specs=[pl.BlockSpec((tm, D), lambda i: (i, 0)),
                  pl.BlockSpec((n, D), lambda i: (0, 0))],
        out_specs=[pl.BlockSpec((tm, D), lambda i: (i, 0))] * n,
        out_shape=[jax.ShapeDtypeStruct((M, D), _BF16)] * n,
        compiler_params=_params("arbitrary"),
        name="rmsnorm",
    )(x, gains)


def _proj_headnorm_kernel(x_ref, w_ref, g_ref, o_ref, *, n_norm_tiles, n_tiles):
    y = jnp.dot(x_ref[...], w_ref[...], preferred_element_type=_F32)
    tn = y.shape[1]

    def normed():
        for c in range(tn // HEAD_DIM):
            sl = slice(c * HEAD_DIM, (c + 1) * HEAD_DIM)
            yc = y[:, sl]
            r = lax.rsqrt(jnp.mean(yc * yc, axis=-1, keepdims=True) + EPS)
            o_ref[:, sl] = (yc * r * g_ref[:, sl]).astype(o_ref.dtype)

    def plain():
        o_ref[...] = y.astype(o_ref.dtype)

    if n_norm_tiles == n_tiles:
        normed()
    else:
        j = pl.program_id(1)
        pl.when(j < n_norm_tiles)(normed)
        pl.when(j >= n_norm_tiles)(plain)


def _proj_headnorm(x, w, g_cols, n_norm_cols, tm, tn):
    M, K = x.shape
    N = w.shape[1]
    kern = functools.partial(_proj_headnorm_kernel, n_norm_tiles=n_norm_cols // tn, n_tiles=N // tn)
    return pl.pallas_call(
        kern,
        grid=(M // tm, N // tn),
        in_specs=[pl.BlockSpec((tm, K), lambda i, j: (i, 0)),
                  pl.BlockSpec((K, tn), lambda i, j: (0, j)),
                  pl.BlockSpec((1, tn), lambda i, j: (0, j))],
        out_specs=pl.BlockSpec((tm, tn), lambda i, j: (i, j)),
        out_shape=jax.ShapeDtypeStruct((M, N), _BF16),
        compiler_params=_params("arbitrary", "arbitrary"),
        name="proj_headnorm",
    )(x, w, g_cols)


def _proj_residual_kernel(x_ref, w_ref, r_ref, o_ref):
    o_ref[...] = r_ref[...] + jnp.dot(x_ref[...], w_ref[...], preferred_element_type=_F32)


def _proj_residual(x, w, res, tm, tn):
    M, K = x.shape
    N = w.shape[1]
    return pl.pallas_call(
        _proj_residual_kernel,
        grid=(M // tm, N // tn),
        in_specs=[pl.BlockSpec((tm, K), lambda i, j: (i, 0)),
                  pl.BlockSpec((K, tn), lambda i, j: (0, j)),
                  pl.BlockSpec((tm, tn), lambda i, j: (i, j))],
        out_specs=pl.BlockSpec((tm, tn), lambda i, j: (i, j)),
        out_shape=jax.ShapeDtypeStruct((M, N), _F32),
        compiler_params=_params("arbitrary", "arbitrary"),
        name="proj_residual",
    )(x, w, res)


def _gate_up_kernel(x_ref, wg_ref, wu_ref, o_ref):
    x = x_ref[...]
    g = jnp.dot(x, wg_ref[...], preferred_element_type=_F32)
    u = jnp.dot(x, wu_ref[...], preferred_element_type=_F32)
    o_ref[...] = (g * jax.nn.sigmoid(g) * u).astype(o_ref.dtype)


def _gate_up(x, wg, wu, tm, tn):
    M, K = x.shape
    N = wg.shape[1]
    return pl.pallas_call(
        _gate_up_kernel,
        grid=(M // tm, N // tn),
        in_specs=[pl.BlockSpec((tm, K), lambda i, j: (i, 0)),
                  pl.BlockSpec((K, tn), lambda i, j: (0, j)),
                  pl.BlockSpec((K, tn), lambda i, j: (0, j))],
        out_specs=pl.BlockSpec((tm, tn), lambda i, j: (i, j)),
        out_shape=jax.ShapeDtypeStruct((M, N), _BF16),
        compiler_params=_params("arbitrary", "arbitrary"),
        name="ffn_gate_up",
    )(x, wg, wu)


def _down_kernel(x_ref, w_ref, r_ref, o_ref, acc_ref):
    k = pl.program_id(1)
    j = pl.program_id(2)
    part = jnp.dot(x_ref[...], w_ref[...], preferred_element_type=_F32)

    @pl.when(k == 0)
    def _():
        acc_ref[j] = part

    @pl.when(k == pl.num_programs(1) - 1)
    def _():
        o_ref[...] = r_ref[...] + acc_ref[j] + part


def _down_residual(x, w, res, tm, tk, tn):
    M, K = x.shape
    N = w.shape[1]
    nk = K // tk
    assert nk == 2
    return pl.pallas_call(
        _down_kernel,
        grid=(M // tm, nk, N // tn),
        in_specs=[pl.BlockSpec((tm, tk), lambda i, k, j: (i, k)),
                  pl.BlockSpec((tk, tn), lambda i, k, j: (k, j)),
                  pl.BlockSpec((tm, tn), lambda i, k, j: (i, j * k))],
        out_specs=pl.BlockSpec((tm, tn), lambda i, k, j: (i, j * k)),
        out_shape=jax.ShapeDtypeStruct((M, N), _F32),
        scratch_shapes=[pltpu.VMEM((N // tn, tm, tn), _F32)],
        compiler_params=_params("arbitrary", "arbitrary", "arbitrary"),
        name="ffn_down_residual",
    )(x, w, res)


def _lane_tiles(a):
    return [a[:, u * LANES:(u + 1) * LANES] for u in range(a.shape[1] // LANES)]


def _diff_attn_kernel(q_ref, k_ref, v_ref, td_ref, ts_ref, lam_ref, gsub_ref, o_ref,
                      bias_ref, sx_ref, sy_ref, mx_ref, my_ref, m_ref, l_ref, acc_ref, *, t, lambda_init):
    i = pl.program_id(2)
    nt = t // LANES

    @pl.when(i == 0)
    def _build_bias():
        zeros = jnp.zeros((LANES, LANES), _F32)
        neg = jnp.full((LANES, LANES), NEG, _F32)
        for c in range(2):
            for a in range(nt):
                for b in range(nt):
                    rows = slice(a * LANES, (a + 1) * LANES)
                    cols = slice(b * LANES, (b + 1) * LANES)
                    if a == b:
                        tile = td_ref[c]
                    elif a == b + 1:
                        tile = ts_ref[c]
                    else:
                        tile = zeros if a > b else neg
                    bias_ref[c, 0, rows, cols] = tile
                    bias_ref[c, 1, rows, cols] = ts_ref[c] if (a == 0 and b == nt - 1) else zeros
                    bias_ref[c, 2, rows, cols] = neg

    m_ref[...] = jnp.full(m_ref.shape, NEG, _F32)
    l_ref[...] = jnp.zeros(l_ref.shape, _F32)
    acc_ref[...] = jnp.zeros(acc_ref.shape, _F32)

    def stage_a(j, kind, s_out, m_out):
        start = pl.multiple_of(j * t, t)
        for c in range(2):
            cols = slice(c * HEAD_DIM, (c + 1) * HEAD_DIM)
            s = lax.dot_general(q_ref[:, cols], k_ref[pl.ds(start, t), cols], _NT,
                                preferred_element_type=_F32)
            if kind is not None:
                s = s + bias_ref[c, kind]
            s_out[c] = s
            part = functools.reduce(jnp.maximum, _lane_tiles(s))
            m_out[c] = jnp.broadcast_to(jnp.max(part, axis=-1, keepdims=True), (t, LANES))

    def stage_b(j, s_in, m_in):
        start = pl.multiple_of(j * t, t)
        vj = v_ref[pl.ds(start, t), :]
        for c in range(2):
            m_old = m_ref[c]
            m_new = jnp.maximum(m_old, m_in[c])
            alpha = jnp.exp2(m_old - m_new)
            p = jnp.exp2(s_in[c] - jnp.concatenate([m_new] * nt, axis=1))
            l_ref[c] = alpha * l_ref[c] + functools.reduce(jnp.add, _lane_tiles(p))
            pv = jnp.dot(p.astype(_BF16), vj, preferred_element_type=_F32)
            acc_ref[c] = jnp.concatenate([alpha] * (acc_ref.shape[2] // LANES), axis=1) * acc_ref[c] + pv
            m_ref[c] = m_new

    j_sub = jnp.maximum(i - 1, 0)
    n_far = jnp.maximum(i - 1, 0)
    n_pairs = n_far // 2
    stage_a(i, 0, sx_ref, mx_ref)
    stage_a(j_sub, jnp.where(i == 0, 2, 1), sy_ref, my_ref)
    stage_b(i, sx_ref, mx_ref)

    def pair_body(u, carry):
        stage_a(2 * u, None, sx_ref, mx_ref)
        stage_b(jnp.where(u == 0, j_sub, 2 * u - 1), sy_ref, my_ref)
        stage_a(2 * u + 1, None, sy_ref, my_ref)
        stage_b(2 * u, sx_ref, mx_ref)
        return carry

    lax.fori_loop(0, n_pairs, pair_body, 0)
    j_pending = jnp.where(n_pairs == 0, j_sub, 2 * n_pairs - 1)

    @pl.when(n_far % 2 == 1)
    def _():
        stage_a(n_far - 1, None, sx_ref, mx_ref)
        stage_b(j_pending, sy_ref, my_ref)
        stage_b(n_far - 1, sx_ref, mx_ref)

    @pl.when(n_far % 2 == 0)
    def _():
        stage_b(j_pending, sy_ref, my_ref)

    lf = lam_ref[...]
    lam = (jnp.exp(jnp.sum(lf[0:1] * lf[1:2], axis=-1, keepdims=True))
           - jnp.exp(jnp.sum(lf[2:3] * lf[3:4], axis=-1, keepdims=True)) + lambda_init)
    l0 = jnp.sum(l_ref[0], axis=-1, keepdims=True)
    l1 = jnp.sum(l_ref[1], axis=-1, keepdims=True)
    o = acc_ref[0] / l0 - lam * (acc_ref[1] / l1)
    o = o * lax.rsqrt(jnp.mean(o * o, axis=-1, keepdims=True) + EPS)
    o_ref[...] = (o * gsub_ref[...] * (1.0 - lambda_init)).astype(o_ref.dtype)


def _diff_attention(qkv, td, ts, lam_qk, g_sub, B, S, D, t, lambda_init):
    n_heads = D // (2 * HEAD_DIM)
    nq = S // t
    w = 2 * HEAD_DIM
    kern = functools.partial(_diff_attn_kernel, t=t, lambda_init=lambda_init)
    return pl.pallas_call(
        kern,
        grid=(B, n_heads, nq),
        in_specs=[pl.BlockSpec((t, w), lambda b, h, i: (b * nq + i, h)),
                  pl.BlockSpec((S, w), lambda b, h, i: (b, n_heads + h)),
                  pl.BlockSpec((S, w), lambda b, h, i: (b, 2 * n_heads + h)),
                  pl.BlockSpec((2, LANES, LANES), lambda b, h, i: (h, 0, 0)),
                  pl.BlockSpec((2, LANES, LANES), lambda b, h, i: (h, 0, 0)),
                  pl.BlockSpec((4, HEAD_DIM), lambda b, h, i: (0, 0)),
                  pl.BlockSpec((1, w), lambda b, h, i: (0, 0))],
        out_specs=pl.BlockSpec((t, w), lambda b, h, i: (b * nq + i, h)),
        out_shape=jax.ShapeDtypeStruct((B * S, D), _BF16),
        scratch_shapes=[pltpu.VMEM((2, 3, t, t), _F32),
                        pltpu.VMEM((2, t, t), _F32),
                        pltpu.VMEM((2, t, t), _F32),
                        pltpu.VMEM((2, t, LANES), _F32),
                        pltpu.VMEM((2, t, LANES), _F32),
                        pltpu.VMEM((2, t, LANES), _F32),
                        pltpu.VMEM((2, t, LANES), _F32),
                        pltpu.VMEM((2, t, w), _F32)],
        compiler_params=_params("arbitrary", "arbitrary", "arbitrary"),
        name="diff_attention",
    )(qkv, qkv, qkv, td, ts, lam_qk, g_sub)


def _swa_kernel(q_ref, kc_ref, kp_ref, vc_ref, vp_ref, bias_ref, sink_ref, o_ref, *, tq):
    first = pl.program_id(2) == 0
    rows = SWA_GROUP * WINDOW
    bias = bias_ref[...].reshape(rows, 2 * WINDOW)
    sink = jnp.concatenate(
        [jnp.broadcast_to(sink_ref[r:r + 1, 0:1], (WINDOW, 1)) for r in range(SWA_GROUP)], axis=0)
    col = lax.broadcasted_iota(jnp.int32, (rows, 2 * WINDOW), 1)
    for c in range(tq // WINDOW):
        cur = slice(c * WINDOW, (c + 1) * WINDOW)
        if c == 0:
            k_prev, v_prev = kp_ref[...], vp_ref[...]
        else:
            prev = slice((c - 1) * WINDOW, c * WINDOW)
            k_prev, v_prev = kc_ref[prev, :], vc_ref[prev, :]
        kk = jnp.concatenate([k_prev, kc_ref[cur, :]], axis=0)
        vv = jnp.concatenate([v_prev, vc_ref[cur, :]], axis=0)
        q4 = jnp.concatenate(
            [q_ref[cur, r * HEAD_DIM:(r + 1) * HEAD_DIM] for r in range(SWA_GROUP)], axis=0)
        s = lax.dot_general(q4, kk, _NT, preferred_element_type=_F32) + bias
        if c == 0:
            s = jnp.where(first & (col < WINDOW), NEG, s)
        m = jnp.maximum(jnp.max(s, axis=-1, keepdims=True), sink)
        p = jnp.exp(s - m)
        denom = jnp.sum(p, axis=-1, keepdims=True) + jnp.exp(sink - m)
        o = jnp.dot(p.astype(_BF16), vv, preferred_element_type=_F32) / denom
        for r in range(SWA_GROUP):
            o_ref[cur, r * HEAD_DIM:(r + 1) * HEAD_DIM] = (
                o[r * WINDOW:(r + 1) * WINDOW, :].astype(o_ref.dtype))


def _swa_attention(q, kv, bias, sinks, B, S, D, tq):
    n_kv = D // (SWA_GROUP * HEAD_DIM)
    nt = S // tq
    bpt = tq // WINDOW
    gw = SWA_GROUP * HEAD_DIM

    def prev_map(off):
        return lambda b, g, t: (jnp.maximum((b * nt + t) * bpt - 1, 0), off + g)

    kern = functools.partial(_swa_kernel, tq=tq)
    return pl.pallas_call(
        kern,
        grid=(B, n_kv, nt),
        in_specs=[pl.BlockSpec((tq, gw), lambda b, g, t: (b * nt + t, g)),
                  pl.BlockSpec((tq, HEAD_DIM), lambda b, g, t: (b * nt + t, g)),
                  pl.BlockSpec((WINDOW, HEAD_DIM), prev_map(0)),
                  pl.BlockSpec((tq, HEAD_DIM), lambda b, g, t: (b * nt + t, n_kv + g)),
                  pl.BlockSpec((WINDOW, HEAD_DIM), prev_map(n_kv)),
                  pl.BlockSpec((SWA_GROUP, WINDOW, 2 * WINDOW), lambda b, g, t: (g, 0, 0)),
                  pl.BlockSpec((None, SWA_GROUP, LANES), lambda b, g, t: (g, 0, 0))],
        out_specs=pl.BlockSpec((tq, gw), lambda b, g, t: (b * nt + t, g)),
        out_shape=jax.ShapeDtypeStruct((B * S, D), _BF16),
        compiler_params=_params("arbitrary", "arbitrary", "arbitrary"),
        name="swa_attention",
    )(q, kv, kv, kv, kv, bias, sinks)


def _t5_bucket(n):
    max_exact = NUM_BUCKETS // 2
    nf = jnp.maximum(n, 1).astype(_F32)
    large = max_exact + (jnp.log(nf / max_exact) / math.log(MAX_DISTANCE / max_exact)
                         * (NUM_BUCKETS - max_exact)).astype(jnp.int32)
    large = jnp.minimum(large, NUM_BUCKETS - 1)
    return jnp.where(n < max_exact, n, large)


def _bias_tiles(rel_bias):
    lut = rel_bias[_t5_bucket(jnp.arange(MAX_DISTANCE))].astype(_F32)
    qi = jnp.arange(LANES)[:, None]
    kj = jnp.arange(LANES)[None, :]
    shifted = lut - lut[MAX_DISTANCE - 1]

    def gather(table, dist):
        return jnp.moveaxis(table[jnp.clip(dist, 0, MAX_DISTANCE - 1)], -1, 0)

    d0 = qi - kj
    td = jnp.where((d0 >= 0)[None], gather(shifted, d0) * LOG2E, NEG)
    ts = gather(shifted, LANES + qi - kj) * LOG2E
    k2 = jnp.arange(2 * WINDOW)[None, :]
    dw = WINDOW + qi - k2
    swa = jnp.where(((dw >= 0) & (dw < WINDOW))[None], gather(lut, dw), NEG)
    return td, ts, swa


def _forward(x, rel_bias, g_attn_norm, g_ffn_norm, w_qkv_a, w_o_a, g_q_a, g_k_a, lam_qk_a, g_sub_a,
             g_kv_norm, w_kv, g_k_shared, w_q_b, w_o_b, g_q_b, sinks_b, w_gate, w_up, w_down, tiles):
    B, S, D = x.shape
    M = B * S
    T = tiles
    scale = HEAD_DIM ** -0.5
    n_maps = D // HEAD_DIM
    kvw = w_kv.shape[1] // 2
    td, ts, swa_bias = _bias_tiles(rel_bias)
    bf = lambda a: a.astype(_BF16)
    x = x.reshape(M, D)

    def ffn(x, layer):
        h = _rmsnorm(x, g_ffn_norm[layer][None], T["rms_tm"])[0]
        act = _gate_up(h, bf(w_gate[layer]), bf(w_up[layer]), T["mm_tm"], T["gu_tn"])
        return _down_residual(act, bf(w_down[layer]), x, T["mm_tm"], T["down_tk"], T["down_tn"])

    lambda_init = 0.8 - 0.6 * math.exp(-0.3 * 0)
    h = _rmsnorm(x, g_attn_norm[0][None], T["rms_tm"])[0]
    g_cols = jnp.concatenate([jnp.tile(g_q_a[0] * (scale * LOG2E), n_maps), jnp.tile(g_k_a[0], n_maps),
                              jnp.ones((D,), _F32)])[None]
    qkv = _proj_headnorm(h, bf(w_qkv_a[0]), g_cols, 2 * D, T["mm_tm"], T["qkv_tn"])
    o = _diff_attention(qkv, td, ts, lam_qk_a[0], g_sub_a[0][None], B, S, D, T["attn_t"], lambda_init)
    x = _proj_residual(o, bf(w_o_a[0]), x, T["mm_tm"], T["res_tn"])
    x = ffn(x, 0)

    h_kv, h_q = _rmsnorm(x, jnp.stack([g_kv_norm, g_attn_norm[1]]), T["rms_tm"])
    g_kv_cols = jnp.concatenate([jnp.tile(g_k_shared, kvw // HEAD_DIM), jnp.ones((kvw,), _F32)])[None]
    kv = _proj_headnorm(h_kv, bf(w_kv), g_kv_cols, kvw, T["mm_tm"], min(T["qkv_tn"], kvw))
    q = _proj_headnorm(h_q, bf(w_q_b[0]), jnp.tile(g_q_b[0] * scale, n_maps)[None], D,
                       T["mm_tm"], T["qkv_tn"])
    sinks = jnp.broadcast_to(sinks_b[0].astype(_F32).reshape(-1, SWA_GROUP, 1),
                             (n_maps // SWA_GROUP, SWA_GROUP, LANES))
    o = _swa_attention(q, kv, swa_bias, sinks, B, S, D, T["swa_tq"])
    x = _proj_residual(o, bf(w_o_b[0]), x, T["mm_tm"], T["res_tn"])
    x = ffn(x, 1)
    return x.reshape(B, S, D)


def kernel(x, rel_bias, g_attn_norm, g_ffn_norm, w_qkv_a, w_o_a, g_q_a, g_k_a, lam_qk_a, g_sub_a,
           g_kv_norm, w_kv, g_k_shared, w_q_b, w_o_b, g_q_b, sinks_b, w_gate, w_up, w_down):
    B, S, D = x.shape
    tiles = _tiles(B * S, S, D, w_gate.shape[-1])
    return _forward(x, rel_bias, g_attn_norm, g_ffn_norm, w_qkv_a, w_o_a, g_q_a, g_k_a, lam_qk_a,
                    g_sub_a, g_kv_norm, w_kv, g_k_shared, w_q_b, w_o_b, g_q_b, sinks_b,
                    w_gate, w_up, w_down, tiles)
```

```python
import functools
import math

import jax
import jax.numpy as jnp
from jax import lax
from jax.experimental import pallas as pl
from jax.experimental.pallas import tpu as pltpu

HEAD_DIM = 128
WINDOW = 128
SWA_GROUP = 4
NUM_BUCKETS = 32
MAX_DISTANCE = 128
EPS = 1e-6
NEG = -1e30
LOG2E = math.log2(math.e)
LANES = 128
VMEM_LIMIT_BYTES = 56 * 1024 * 1024
CAST_BLOCK_BYTES = 8 * 1024 * 1024
MXU_COLS = 256

_F32 = jnp.float32
_BF16 = jnp.bfloat16
_NT = (((1,), (1,)), ((), ()))


def _params(*sem):
    return pltpu.CompilerParams(dimension_semantics=sem, vmem_limit_bytes=VMEM_LIMIT_BYTES)


def _tiles(M, S, D, D_FF):
    return dict(
        rms_tm=min(256, M),
        mm_tm=min(1024, M),
        qkv_tn=min(1024, D),
        res_tn=min(512, D),
        gu_tn=min(256, D_FF),
        down_tm=min(512, M),
        down_tk=D_FF // 2,
        down_tn=min(1024, D),
        attn_t=min(512, S // 4),
        swa_tq=min(1024, S),
    )


def _cast_kernel(w_ref, o_ref):
    o_ref[...] = w_ref[...].astype(o_ref.dtype)


def _cast_bf16(w, layer):
    _, R, C = w.shape
    tr = min(R, max(8, CAST_BLOCK_BYTES // (4 * C) // 8 * 8))
    while R % tr:
        tr -= 8
    return pl.pallas_call(
        _cast_kernel,
        grid=(R // tr,),
        in_specs=[pl.BlockSpec((None, tr, C), lambda i: (layer, i, 0))],
        out_specs=pl.BlockSpec((tr, C), lambda i: (i, 0)),
        out_shape=jax.ShapeDtypeStruct((R, C), _BF16),
        compiler_params=_params("arbitrary"),
        name="cast_bf16",
    )(w)


def _rms_kernel(x_ref, g_ref, *o_refs):
    x = x_ref[...]
    y = x * lax.rsqrt(jnp.mean(x * x, axis=-1, keepdims=True) + EPS)
    for n, o_ref in enumerate(o_refs):
        o_ref[...] = (y * g_ref[n:n + 1, :]).astype(o_ref.dtype)


def _rmsnorm(x, gains, tm):
    M, D = x.shape
    n = gains.shape[0]
    return pl.pallas_call(
        _rms_kernel,
        grid=(M // tm,),
        in_specs=[pl.BlockSpec((tm, D), lambda i: (i, 0)),
                  pl.BlockSpec((n, D), lambda i: (0, 0))],
        out_specs=[pl.BlockSpec((tm, D), lambda i: (i, 0))] * n,
        out_shape=[jax.ShapeDtypeStruct((M, D), _BF16)] * n,
        compiler_params=_params("arbitrary"),
        name="rmsnorm",
    )(x, gains)


def _proj_headnorm_kernel(x_ref, w_ref, g_ref, o_ref, *, n_norm_tiles, n_tiles):
    tn = o_ref.shape[1]
    sub = min(MXU_COLS, tn)

    def sub_dots():
        for u in range(tn // sub):
            cols = slice(u * sub, (u + 1) * sub)
            yield u * sub, jnp.dot(x_ref[...], w_ref[:, cols], preferred_element_type=_F32)

    def normed():
        for base, y in sub_dots():
            for c in range(sub // HEAD_DIM):
                yc = y[:, c * HEAD_DIM:(c + 1) * HEAD_DIM]
                sl = slice(base + c * HEAD_DIM, base + (c + 1) * HEAD_DIM)
                r = lax.rsqrt(jnp.mean(yc * yc, axis=-1, keepdims=True) + EPS)
                o_ref[:, sl] = (yc * r * g_ref[:, sl]).astype(o_ref.dtype)

    def plain():
        for base, y in sub_dots():
            o_ref[:, base:base + sub] = y.astype(o_ref.dtype)

    if n_norm_tiles == n_tiles:
        normed()
    else:
        j = pl.program_id(1)
        pl.when(j < n_norm_tiles)(normed)
        pl.when(j >= n_norm_tiles)(plain)


def _proj_headnorm(x, w, g_cols, n_norm_cols, tm, tn):
    M, K = x.shape
    N = w.shape[1]
    kern = functools.partial(_proj_headnorm_kernel, n_norm_tiles=n_norm_cols // tn, n_tiles=N // tn)
    return pl.pallas_call(
        kern,
        grid=(M // tm, N // tn),
        in_specs=[pl.BlockSpec((tm, K), lambda i, j: (i, 0)),
                  pl.BlockSpec((K, tn), lambda i, j: (0, j)),
                  pl.BlockSpec((1, tn), lambda i, j: (0, j))],
        out_specs=pl.BlockSpec((tm, tn), lambda i, j: (i, j)),
        out_shape=jax.ShapeDtypeStruct((M, N), _BF16),
        compiler_params=_params("arbitrary", "arbitrary"),
        name="proj_headnorm",
    )(x, w, g_cols)


def _proj_residual_kernel(x_ref, w_ref, r_ref, o_ref):
    o_ref[...] = r_ref[...] + jnp.dot(x_ref[...], w_ref[...], preferred_element_type=_F32)


def _proj_residual(x, w, res, tm, tn):
    M, K = x.shape
    N = w.shape[1]
    return pl.pallas_call(
        _proj_residual_kernel,
        grid=(M // tm, N // tn),
        in_specs=[pl.BlockSpec((tm, K), lambda i, j: (i, 0)),
                  pl.BlockSpec((K, tn), lambda i, j: (0, j)),
                  pl.BlockSpec((tm, tn), lambda i, j: (i, j))],
        out_specs=pl.BlockSpec((tm, tn), lambda i, j: (i, j)),
        out_shape=jax.ShapeDtypeStruct((M, N), _F32),
        compiler_params=_params("arbitrary", "arbitrary"),
        name="proj_residual",
    )(x, w, res)


def _gate_up_kernel(x_ref, wg_ref, wu_ref, o_ref):
    x = x_ref[...]
    g = jnp.dot(x, wg_ref[...], preferred_element_type=_F32)
    u = jnp.dot(x, wu_ref[...], preferred_element_type=_F32)
    o_ref[...] = (g * jax.nn.sigmoid(g) * u).astype(o_ref.dtype)


def _gate_up(x, wg, wu, tm, tn):
    M, K = x.shape
    N = wg.shape[1]
    return pl.pallas_call(
        _gate_up_kernel,
        grid=(M // tm, N // tn),
        in_specs=[pl.BlockSpec((tm, K), lambda i, j: (i, 0)),
                  pl.BlockSpec((K, tn), lambda i, j: (0, j)),
                  pl.BlockSpec((K, tn), lambda i, j: (0, j))],
        out_specs=pl.BlockSpec((tm, tn), lambda i, j: (i, j)),
        out_shape=jax.ShapeDtypeStruct((M, N), _BF16),
        compiler_params=_params("arbitrary", "arbitrary"),
        name="ffn_gate_up",
    )(x, wg, wu)


def _down_kernel(x_ref, w_ref, r_ref, o_ref, acc_ref):
    k = pl.program_id(1)
    j = pl.program_id(2)
    part = jnp.dot(x_ref[...], w_ref[...], preferred_element_type=_F32)

    @pl.when(k == 0)
    def _():
        acc_ref[j] = part

    @pl.when(k == pl.num_programs(1) - 1)
    def _():
        o_ref[...] = r_ref[...] + acc_ref[j] + part


def _down_residual(x, w, res, tm, tk, tn):
    M, K = x.shape
    N = w.shape[1]
    nk = K // tk
    assert nk == 2
    return pl.pallas_call(
        _down_kernel,
        grid=(M // tm, nk, N // tn),
        in_specs=[pl.BlockSpec((tm, tk), lambda i, k, j: (i, k)),
                  pl.BlockSpec((tk, tn), lambda i, k, j: (k, j)),
                  pl.BlockSpec((tm, tn), lambda i, k, j: (i, j * k))],
        out_specs=pl.BlockSpec((tm, tn), lambda i, k, j: (i, j * k)),
        out_shape=jax.ShapeDtypeStruct((M, N), _F32),
        scratch_shapes=[pltpu.VMEM((N // tn, tm, tn), _F32)],
        compiler_params=_params("arbitrary", "arbitrary", "arbitrary"),
        name="ffn_down_residual",
    )(x, w, res)


def _lane_tiles(a):
    return [a[:, u * LANES:(u + 1) * LANES] for u in range(a.shape[1] // LANES)]


def _diff_attn_kernel(q_ref, k_ref, v_ref, td_ref, ts_ref, lam_ref, gsub_ref, o_ref,
                      bias_ref, sx_ref, sy_ref, mx_ref, my_ref, m_ref, l_ref, acc_ref, *, t, lambda_init):
    i = pl.program_id(2)
    nt = t // LANES

    @pl.when(i == 0)
    def _build_bias():
        zeros = jnp.zeros((LANES, LANES), _F32)
        neg = jnp.full((LANES, LANES), NEG, _F32)
        for c in range(2):
            for a in range(nt):
                for b in range(nt):
                    rows = slice(a * LANES, (a + 1) * LANES)
                    cols = slice(b * LANES, (b + 1) * LANES)
                    if a == b:
                        tile = td_ref[c]
                    elif a == b + 1:
                        tile = ts_ref[c]
                    else:
                        tile = zeros if a > b else neg
                    bias_ref[c, 0, rows, cols] = tile
                    bias_ref[c, 1, rows, cols] = ts_ref[c] if (a == 0 and b == nt - 1) else zeros
                    bias_ref[c, 2, rows, cols] = neg

    m_ref[...] = jnp.full(m_ref.shape, NEG, _F32)
    l_ref[...] = jnp.zeros(l_ref.shape, _F32)
    acc_ref[...] = jnp.zeros(acc_ref.shape, _F32)

    def stage_a(j, kind, s_out, m_out):
        start = pl.multiple_of(j * t, t)
        for c in range(2):
            cols = slice(c * HEAD_DIM, (c + 1) * HEAD_DIM)
            s = lax.dot_general(q_ref[:, cols], k_ref[pl.ds(start, t), cols], _NT,
                                preferred_element_type=_F32)
            if kind is not None:
                s = s + bias_ref[c, kind]
            s_out[c] = s
            part = functools.reduce(jnp.maximum, _lane_tiles(s))
            m_out[c] = jnp.broadcast_to(jnp.max(part, axis=-1, keepdims=True), (t, LANES))

    def stage_b(j, s_in, m_in):
        start = pl.multiple_of(j * t, t)
        vj = v_ref[pl.ds(start, t), :]
        for c in range(2):
            m_old = m_ref[c]
            m_new = jnp.maximum(m_old, m_in[c])
            alpha = jnp.exp2(m_old - m_new)
            p = jnp.exp2(s_in[c] - jnp.concatenate([m_new] * nt, axis=1))
            l_ref[c] = alpha * l_ref[c] + functools.reduce(jnp.add, _lane_tiles(p))
            pv = jnp.dot(p.astype(_BF16), vj, preferred_element_type=_F32)
            acc_ref[c] = jnp.concatenate([alpha] * (acc_ref.shape[2] // LANES), axis=1) * acc_ref[c] + pv
            m_ref[c] = m_new

    j_sub = jnp.maximum(i - 1, 0)
    n_far = jnp.maximum(i - 1, 0)
    n_pairs = n_far // 2
    stage_a(i, 0, sx_ref, mx_ref)
    stage_a(j_sub, jnp.where(i == 0, 2, 1), sy_ref, my_ref)
    stage_b(i, sx_ref, mx_ref)

    def pair_body(u, carry):
        stage_a(2 * u, None, sx_ref, mx_ref)
        stage_b(jnp.where(u == 0, j_sub, 2 * u - 1), sy_ref, my_ref)
        stage_a(2 * u + 1, None, sy_ref, my_ref)
        stage_b(2 * u, sx_ref, mx_ref)
        return carry

    lax.fori_loop(0, n_pairs, pair_body, 0)
    j_pending = jnp.where(n_pairs == 0, j_sub, 2 * n_pairs - 1)

    @pl.when(n_far % 2 == 1)
    def _():
        stage_a(n_far - 1, None, sx_ref, mx_ref)
        stage_b(j_pending, sy_ref, my_ref)
        stage_b(n_far - 1, sx_ref, mx_ref)

    @pl.when(n_far % 2 == 0)
    def _():
        stage_b(j_pending, sy_ref, my_ref)

    lf = lam_ref[...]
    lam = (jnp.exp(jnp.sum(lf[0:1] * lf[1:2], axis=-1, keepdims=True))
           - jnp.exp(jnp.sum(lf[2:3] * lf[3:4], axis=-1, keepdims=True)) + lambda_init)
    l0 = jnp.sum(l_ref[0], axis=-1, keepdims=True)
    l1 = jnp.sum(l_ref[1], axis=-1, keepdims=True)
    o = acc_ref[0] / l0 - lam * (acc_ref[1] / l1)
    o = o * lax.rsqrt(jnp.mean(o * o, axis=-1, keepdims=True) + EPS)
    o_ref[...] = (o * gsub_ref[...] * (1.0 - lambda_init)).astype(o_ref.dtype)


def _diff_attention(qkv, td, ts, lam_qk, g_sub, B, S, D, t, lambda_init):
    n_heads = D // (2 * HEAD_DIM)
    nq = S // t
    w = 2 * HEAD_DIM
    kern = functools.partial(_diff_attn_kernel, t=t, lambda_init=lambda_init)
    return pl.pallas_call(
        kern,
        grid=(B, n_heads, nq),
        in_specs=[pl.BlockSpec((t, w), lambda b, h, i: (b * nq + i, h)),
                  pl.BlockSpec((S, w), lambda b, h, i: (b, n_heads + h)),
                  pl.BlockSpec((S, w), lambda b, h, i: (b, 2 * n_heads + h)),
                  pl.BlockSpec((2, LANES, LANES), lambda b, h, i: (h, 0, 0)),
                  pl.BlockSpec((2, LANES, LANES), lambda b, h, i: (h, 0, 0)),
                  pl.BlockSpec((4, HEAD_DIM), lambda b, h, i: (0, 0)),
                  pl.BlockSpec((1, w), lambda b, h, i: (0, 0))],
        out_specs=pl.BlockSpec((t, w), lambda b, h, i: (b * nq + i, h)),
        out_shape=jax.ShapeDtypeStruct((B * S, D), _BF16),
        scratch_shapes=[pltpu.VMEM((2, 3, t, t), _F32),
                        pltpu.VMEM((2, t, t), _F32),
                        pltpu.VMEM((2, t, t), _F32),
                        pltpu.VMEM((2, t, LANES), _F32),
                        pltpu.VMEM((2, t, LANES), _F32),
                        pltpu.VMEM((2, t, LANES), _F32),
                        pltpu.VMEM((2, t, LANES), _F32),
                        pltpu.VMEM((2, t, w), _F32)],
        compiler_params=_params("arbitrary", "arbitrary", "arbitrary"),
        name="diff_attention",
    )(qkv, qkv, qkv, td, ts, lam_qk, g_sub)


def _swa_kernel(q_ref, kc_ref, kp_ref, vc_ref, vp_ref, bias_ref, sink_ref, o_ref, *, tq):
    first = pl.program_id(2) == 0
    rows = SWA_GROUP * WINDOW
    bias = bias_ref[...].reshape(rows, 2 * WINDOW)
    sink = jnp.concatenate(
        [jnp.broadcast_to(sink_ref[r:r + 1, :], (WINDOW, LANES)) for r in range(SWA_GROUP)], axis=0)
    col = lax.broadcasted_iota(jnp.int32, (rows, 2 * WINDOW), 1)
    ones = jnp.ones((2 * WINDOW, LANES), _BF16)
    for c in range(tq // WINDOW):
        cur = slice(c * WINDOW, (c + 1) * WINDOW)
        if c == 0:
            k_prev, v_prev = kp_ref[...], vp_ref[...]
        else:
            prev = slice((c - 1) * WINDOW, c * WINDOW)
            k_prev, v_prev = kc_ref[prev, :], vc_ref[prev, :]
        kk = jnp.concatenate([k_prev, kc_ref[cur, :]], axis=0)
        vv = jnp.concatenate([jnp.concatenate([v_prev, vc_ref[cur, :]], axis=0), ones], axis=1)
        q4 = jnp.concatenate(
            [q_ref[cur, r * HEAD_DIM:(r + 1) * HEAD_DIM] for r in range(SWA_GROUP)], axis=0)
        s = lax.dot_general(q4, kk, _NT, preferred_element_type=_F32) + bias
        if c == 0:
            s = jnp.where(first & (col < WINDOW), NEG, s)
        part = functools.reduce(jnp.maximum, _lane_tiles(s))
        m = jnp.maximum(jnp.broadcast_to(jnp.max(part, axis=-1, keepdims=True), (rows, LANES)), sink)
        p = jnp.exp2(s - jnp.concatenate([m, m], axis=1))
        pv = jnp.dot(p.astype(_BF16), vv, preferred_element_type=_F32)
        o = pv[:, :HEAD_DIM] / (pv[:, HEAD_DIM:] + jnp.exp2(sink - m))
        for r in range(SWA_GROUP):
            o_ref[cur, r * HEAD_DIM:(r + 1) * HEAD_DIM] = (
                o[r * WINDOW:(r + 1) * WINDOW, :].astype(o_ref.dtype))


def _swa_attention(q, kv, bias, sinks, B, S, D, tq):
    n_kv = D // (SWA_GROUP * HEAD_DIM)
    nt = S // tq
    bpt = tq // WINDOW
    gw = SWA_GROUP * HEAD_DIM

    def prev_map(off):
        return lambda b, g, t: (jnp.maximum((b * nt + t) * bpt - 1, 0), off + g)

    kern = functools.partial(_swa_kernel, tq=tq)
    return pl.pallas_call(
        kern,
        grid=(B, n_kv, nt),
        in_specs=[pl.BlockSpec((tq, gw), lambda b, g, t: (b * nt + t, g)),
                  pl.BlockSpec((tq, HEAD_DIM), lambda b, g, t: (b * nt + t, g)),
                  pl.BlockSpec((WINDOW, HEAD_DIM), prev_map(0)),
                  pl.BlockSpec((tq, HEAD_DIM), lambda b, g, t: (b * nt + t, n_kv + g)),
                  pl.BlockSpec((WINDOW, HEAD_DIM), prev_map(n_kv)),
                  pl.BlockSpec((SWA_GROUP, WINDOW, 2 * WINDOW), lambda b, g, t: (g, 0, 0)),
                  pl.BlockSpec((None, SWA_GROUP, LANES), lambda b, g, t: (g, 0, 0))],
        out_specs=pl.BlockSpec((tq, gw), lambda b, g, t: (b * nt + t, g)),
        out_shape=jax.ShapeDtypeStruct((B * S, D), _BF16),
        compiler_params=_params("arbitrary", "arbitrary", "arbitrary"),
        name="swa_attention",
    )(q, kv, kv, kv, kv, bias, sinks)


def _t5_bucket(n):
    max_exact = NUM_BUCKETS // 2
    nf = jnp.maximum(n, 1).astype(_F32)
    large = max_exact + (jnp.log(nf / max_exact) / math.log(MAX_DISTANCE / max_exact)
                         * (NUM_BUCKETS - max_exact)).astype(jnp.int32)
    large = jnp.minimum(large, NUM_BUCKETS - 1)
    return jnp.where(n < max_exact, n, large)


def _bias_tiles(rel_bias):
    def lookup(table, idx):
        onehot = (idx[..., None] == jnp.arange(table.shape[0])).astype(_F32)
        return jnp.einsum("...d,dh->h...", onehot, table, precision=lax.Precision.HIGHEST)

    lut = lookup(rel_bias.astype(_F32), _t5_bucket(jnp.arange(MAX_DISTANCE))).T * LOG2E
    shifted = lut - lut[MAX_DISTANCE - 1]
    qi = jnp.arange(LANES)[:, None]
    kj = jnp.arange(LANES)[None, :]
    clip = lambda dist: jnp.clip(dist, 0, MAX_DISTANCE - 1)
    d0 = qi - kj
    td = jnp.where((d0 >= 0)[None], lookup(shifted, clip(d0)), NEG)
    ts = lookup(shifted, clip(LANES + qi - kj))
    dw = WINDOW + qi - jnp.arange(2 * WINDOW)[None, :]
    swa = jnp.where(((dw >= 0) & (dw < WINDOW))[None], lookup(lut, clip(dw)), NEG)
    return td, ts, swa


def _forward(x, rel_bias, g_attn_norm, g_ffn_norm, w_qkv_a, w_o_a, g_q_a, g_k_a, lam_qk_a, g_sub_a,
             g_kv_norm, w_kv, g_k_shared, w_q_b, w_o_b, g_q_b, sinks_b, w_gate, w_up, w_down, tiles):
    B, S, D = x.shape
    M = B * S
    T = tiles
    q_scale = HEAD_DIM ** -0.5 * LOG2E
    n_maps = D // HEAD_DIM
    kvw = w_kv.shape[1] // 2
    td, ts, swa_bias = _bias_tiles(rel_bias)
    bf = _cast_bf16
    x = x.reshape(M, D)

    def ffn(x, layer):
        h = _rmsnorm(x, g_ffn_norm[layer][None], T["rms_tm"])[0]
        act = _gate_up(h, bf(w_gate, layer), bf(w_up, layer), T["mm_tm"], T["gu_tn"])
        return _down_residual(act, bf(w_down, layer), x, T["down_tm"], T["down_tk"], T["down_tn"])

    lambda_init = 0.8 - 0.6 * math.exp(-0.3 * 0)
    h = _rmsnorm(x, g_attn_norm[0][None], T["rms_tm"])[0]
    g_cols = jnp.concatenate([jnp.tile(g_q_a[0] * q_scale, n_maps), jnp.tile(g_k_a[0], n_maps),
                              jnp.ones((D,), _F32)])[None]
    qkv = _proj_headnorm(h, bf(w_qkv_a, 0), g_cols, 2 * D, T["mm_tm"], T["qkv_tn"])
    o = _diff_attention(qkv, td, ts, lam_qk_a[0], g_sub_a[0][None], B, S, D, T["attn_t"], lambda_init)
    x = _proj_residual(o, bf(w_o_a, 0), x, T["mm_tm"], T["res_tn"])
    x = ffn(x, 0)

    h_kv, h_q = _rmsnorm(x, jnp.stack([g_kv_norm, g_attn_norm[1]]), T["rms_tm"])
    g_kv_cols = jnp.concatenate([jnp.tile(g_k_shared, kvw // HEAD_DIM), jnp.ones((kvw,), _F32)])[None]
    kv = _proj_headnorm(h_kv, bf(w_kv[None], 0), g_kv_cols, kvw, T["mm_tm"], min(T["qkv_tn"], kvw))
    q = _proj_headnorm(h_q, bf(w_q_b, 0), jnp.tile(g_q_b[0] * q_scale, n_maps)[None], D,
                       T["mm_tm"], T["qkv_tn"])
    sinks = jnp.broadcast_to((sinks_b[0].astype(_F32) * LOG2E).reshape(-1, SWA_GROUP, 1),
                             (n_maps // SWA_GROUP, SWA_GROUP, LANES))
    o = _swa_attention(q, kv, swa_bias, sinks, B, S, D, T["swa_tq"])
    x = _proj_residual(o, bf(w_o_b, 0), x, T["mm_tm"], T["res_tn"])
    x = ffn(x, 1)
    return x.reshape(B, S, D)


def kernel(x, rel_bias, g_attn_norm, g_ffn_norm, w_qkv_a, w_o_a, g_q_a, g_k_a, lam_qk_a, g_sub_a,
           g_kv_norm, w_kv, g_k_shared, w_q_b, w_o_b, g_q_b, sinks_b, w_gate, w_up, w_down):
    B, S, D = x.shape
    tiles = _tiles(B * S, S, D, w_gate.shape[-1])
    return _forward(x, rel_bias, g_attn_norm, g_ffn_norm, w_qkv_a, w_o_a, g_q_a, g_k_a, lam_qk_a,
                    g_sub_a, g_kv_norm, w_kv, g_k_shared, w_q_b, w_o_b, g_q_b, sinks_b,
                    w_gate, w_up, w_down, tiles)
```

```python
import functools
import math

import jax
import jax.numpy as jnp
from jax import lax
from jax.experimental import pallas as pl
from jax.experimental.pallas import tpu as pltpu

HEAD_DIM = 128
WINDOW = 128
SWA_GROUP = 4
NUM_BUCKETS = 32
MAX_DISTANCE = 128
EPS = 1e-6
NEG = -1e30
LOG2E = math.log2(math.e)
LANES = 128
VMEM_LIMIT_BYTES = 56 * 1024 * 1024
CAST_BLOCK_BYTES = 8 * 1024 * 1024
MXU_COLS = 256

_F32 = jnp.float32
_BF16 = jnp.bfloat16
_NT = (((1,), (1,)), ((), ()))


def _params(*sem):
    return pltpu.CompilerParams(dimension_semantics=sem, vmem_limit_bytes=VMEM_LIMIT_BYTES)


def _tiles(M, S, D, D_FF):
    return dict(
        rms_tm=min(256, M),
        mm_tm=min(1024, M),
        qkv_tn=min(512, D),
        res_tn=min(512, D),
        gu_tn=min(256, D_FF),
        down_tm=min(512, M),
        down_tk=D_FF // 2,
        down_tn=min(1024, D),
        attn_t=min(512, S // 4),
        swa_tq=min(1024, S),
    )


def _cast_kernel(w_ref, o_ref):
    o_ref[...] = w_ref[...].astype(o_ref.dtype)


def _cast_bf16(w, layer):
    _, R, C = w.shape
    tr = min(R, max(8, CAST_BLOCK_BYTES // (4 * C) // 8 * 8))
    while R % tr:
        tr -= 8
    return pl.pallas_call(
        _cast_kernel,
        grid=(R // tr,),
        in_specs=[pl.BlockSpec((None, tr, C), lambda i: (layer, i, 0))],
        out_specs=pl.BlockSpec((tr, C), lambda i: (i, 0)),
        out_shape=jax.ShapeDtypeStruct((R, C), _BF16),
        compiler_params=_params("arbitrary"),
        name="cast_bf16",
    )(w)


def _rms_kernel(x_ref, g_ref, *o_refs):
    x = x_ref[...]
    y = x * lax.rsqrt(jnp.mean(x * x, axis=-1, keepdims=True) + EPS)
    for n, o_ref in enumerate(o_refs):
        o_ref[...] = (y * g_ref[n:n + 1, :]).astype(o_ref.dtype)


def _rmsnorm(x, gains, tm):
    M, D = x.shape
    n = gains.shape[0]
    return pl.pallas_call(
        _rms_kernel,
        grid=(M // tm,),
        in_specs=[pl.BlockSpec((tm, D), lambda i: (i, 0)),
                  pl.BlockSpec((n, D), lambda i: (0, 0))],
        out_specs=[pl.BlockSpec((tm, D), lambda i: (i, 0))] * n,
        out_shape=[jax.ShapeDtypeStruct((M, D), _BF16)] * n,
        compiler_params=_params("arbitrary"),
        name="rmsnorm",
    )(x, gains)


def _proj_headnorm_kernel(x_ref, w_ref, g_ref, o_ref, *, n_norm_tiles, n_tiles):
    tn = o_ref.shape[1]
    sub = min(MXU_COLS, tn)

    def sub_dots():
        for u in range(tn // sub):
            cols = slice(u * sub, (u + 1) * sub)
            yield u * sub, jnp.dot(x_ref[...], w_ref[:, cols].astype(_BF16), preferred_element_type=_F32)

    def normed():
        for base, y in sub_dots():
            for c in range(sub // HEAD_DIM):
                yc = y[:, c * HEAD_DIM:(c + 1) * HEAD_DIM]
                sl = slice(base + c * HEAD_DIM, base + (c + 1) * HEAD_DIM)
                r = lax.rsqrt(jnp.mean(yc * yc, axis=-1, keepdims=True) + EPS)
                o_ref[:, sl] = (yc * r * g_ref[:, sl]).astype(o_ref.dtype)

    def plain():
        for base, y in sub_dots():
            o_ref[:, base:base + sub] = y.astype(o_ref.dtype)

    if n_norm_tiles == n_tiles:
        normed()
    else:
        j = pl.program_id(1)
        pl.when(j < n_norm_tiles)(normed)
        pl.when(j >= n_norm_tiles)(plain)


def _proj_headnorm(x, w, layer, g_cols, n_norm_cols, tm, tn):
    M, K = x.shape
    N = w.shape[2]
    kern = functools.partial(_proj_headnorm_kernel, n_norm_tiles=n_norm_cols // tn, n_tiles=N // tn)
    return pl.pallas_call(
        kern,
        grid=(M // tm, N // tn),
        in_specs=[pl.BlockSpec((tm, K), lambda i, j: (i, 0)),
                  pl.BlockSpec((None, K, tn), lambda i, j: (layer, 0, j)),
                  pl.BlockSpec((1, tn), lambda i, j: (0, j))],
        out_specs=pl.BlockSpec((tm, tn), lambda i, j: (i, j)),
        out_shape=jax.ShapeDtypeStruct((M, N), _BF16),
        compiler_params=_params("arbitrary", "arbitrary"),
        name="proj_headnorm",
    )(x, w, g_cols)


def _proj_residual_kernel(x_ref, w_ref, r_ref, o_ref):
    o_ref[...] = r_ref[...] + jnp.dot(x_ref[...], w_ref[...].astype(_BF16), preferred_element_type=_F32)


def _proj_residual(x, w, layer, res, tm, tn):
    M, K = x.shape
    N = w.shape[2]
    return pl.pallas_call(
        _proj_residual_kernel,
        grid=(M // tm, N // tn),
        in_specs=[pl.BlockSpec((tm, K), lambda i, j: (i, 0)),
                  pl.BlockSpec((None, K, tn), lambda i, j: (layer, 0, j)),
                  pl.BlockSpec((tm, tn), lambda i, j: (i, j))],
        out_specs=pl.BlockSpec((tm, tn), lambda i, j: (i, j)),
        out_shape=jax.ShapeDtypeStruct((M, N), _F32),
        compiler_params=_params("arbitrary", "arbitrary"),
        name="proj_residual",
    )(x, w, res)


def _gate_up_kernel(x_ref, wg_ref, wu_ref, o_ref):
    x = x_ref[...]
    g = jnp.dot(x, wg_ref[...].astype(_BF16), preferred_element_type=_F32)
    u = jnp.dot(x, wu_ref[...].astype(_BF16), preferred_element_type=_F32)
    o_ref[...] = (g * jax.nn.sigmoid(g) * u).astype(o_ref.dtype)


def _gate_up(x, wg, wu, layer, tm, tn):
    M, K = x.shape
    N = wg.shape[2]
    return pl.pallas_call(
        _gate_up_kernel,
        grid=(M // tm, N // tn),
        in_specs=[pl.BlockSpec((tm, K), lambda i, j: (i, 0)),
                  pl.BlockSpec((None, K, tn), lambda i, j: (layer, 0, j)),
                  pl.BlockSpec((None, K, tn), lambda i, j: (layer, 0, j))],
        out_specs=pl.BlockSpec((tm, tn), lambda i, j: (i, j)),
        out_shape=jax.ShapeDtypeStruct((M, N), _BF16),
        compiler_params=_params("arbitrary", "arbitrary"),
        name="ffn_gate_up",
    )(x, wg, wu)


def _down_kernel(x_ref, w_ref, r_ref, o_ref, acc_ref):
    k = pl.program_id(1)
    j = pl.program_id(2)
    part = jnp.dot(x_ref[...], w_ref[...], preferred_element_type=_F32)

    @pl.when(k == 0)
    def _():
        acc_ref[j] = part

    @pl.when(k == pl.num_programs(1) - 1)
    def _():
        o_ref[...] = r_ref[...] + acc_ref[j] + part


def _down_residual(x, w, res, tm, tk, tn):
    M, K = x.shape
    N = w.shape[1]
    nk = K // tk
    assert nk == 2
    return pl.pallas_call(
        _down_kernel,
        grid=(M // tm, nk, N // tn),
        in_specs=[pl.BlockSpec((tm, tk), lambda i, k, j: (i, k)),
                  pl.BlockSpec((tk, tn), lambda i, k, j: (k, j)),
                  pl.BlockSpec((tm, tn), lambda i, k, j: (i, j * k))],
        out_specs=pl.BlockSpec((tm, tn), lambda i, k, j: (i, j * k)),
        out_shape=jax.ShapeDtypeStruct((M, N), _F32),
        scratch_shapes=[pltpu.VMEM((N // tn, tm, tn), _F32)],
        compiler_params=_params("arbitrary", "arbitrary", "arbitrary"),
        name="ffn_down_residual",
    )(x, w, res)


def _lane_tiles(a):
    return [a[:, u * LANES:(u + 1) * LANES] for u in range(a.shape[1] // LANES)]


def _diff_attn_kernel(q_ref, k_ref, v_ref, td_ref, ts_ref, lam_ref, gsub_ref, o_ref,
                      bias_ref, sx_ref, sy_ref, mx_ref, my_ref, m_ref, l_ref, acc_ref, *, t, lambda_init):
    i = pl.program_id(2)
    nt = t // LANES

    @pl.when(i == 0)
    def _build_bias():
        zeros = jnp.zeros((LANES, LANES), _F32)
        neg = jnp.full((LANES, LANES), NEG, _F32)
        for c in range(2):
            for a in range(nt):
                for b in range(nt):
                    rows = slice(a * LANES, (a + 1) * LANES)
                    cols = slice(b * LANES, (b + 1) * LANES)
                    if a == b:
                        tile = td_ref[c]
                    elif a == b + 1:
                        tile = ts_ref[c]
                    else:
                        tile = zeros if a > b else neg
                    bias_ref[c, 0, rows, cols] = tile
                    bias_ref[c, 1, rows, cols] = ts_ref[c] if (a == 0 and b == nt - 1) else zeros
                    bias_ref[c, 2, rows, cols] = neg

    m_ref[...] = jnp.full(m_ref.shape, NEG, _F32)
    l_ref[...] = jnp.zeros(l_ref.shape, _F32)
    acc_ref[...] = jnp.zeros(acc_ref.shape, _F32)

    def stage_a(j, kind, s_out, m_out):
        start = pl.multiple_of(j * t, t)
        for c in range(2):
            cols = slice(c * HEAD_DIM, (c + 1) * HEAD_DIM)
            s = lax.dot_general(q_ref[:, cols], k_ref[pl.ds(start, t), cols], _NT,
                                preferred_element_type=_F32)
            if kind is not None:
                s = s + bias_ref[c, kind]
            s_out[c] = s
            part = functools.reduce(jnp.maximum, _lane_tiles(s))
            m_out[c] = jnp.broadcast_to(jnp.max(part, axis=-1, keepdims=True), (t, LANES))

    def stage_b(j, s_in, m_in):
        start = pl.multiple_of(j * t, t)
        vj = v_ref[pl.ds(start, t), :]
        for c in range(2):
            m_old = m_ref[c]
            m_new = jnp.maximum(m_old, m_in[c])
            alpha = jnp.exp2(m_old - m_new)
            m_rep = jnp.concatenate([m_new] * (MXU_COLS // LANES), axis=1)
            psum = alpha * l_ref[c]
            pv = jnp.concatenate([alpha] * (acc_ref.shape[2] // LANES), axis=1) * acc_ref[c]
            for u in range(t // MXU_COLS):
                cols = slice(u * MXU_COLS, (u + 1) * MXU_COLS)
                p = jnp.exp2(s_in[c, :, cols] - m_rep)
                psum = psum + functools.reduce(jnp.add, _lane_tiles(p))
                pv = pv + jnp.dot(p.astype(_BF16), vj[cols, :], preferred_element_type=_F32)
            l_ref[c] = psum
            acc_ref[c] = pv
            m_ref[c] = m_new

    j_sub = jnp.maximum(i - 1, 0)
    n_far = jnp.maximum(i - 1, 0)
    n_pairs = n_far // 2
    stage_a(i, 0, sx_ref, mx_ref)
    stage_a(j_sub, jnp.where(i == 0, 2, 1), sy_ref, my_ref)
    stage_b(i, sx_ref, mx_ref)

    def pair_body(u, carry):
        stage_a(2 * u, None, sx_ref, mx_ref)
        stage_b(jnp.where(u == 0, j_sub, 2 * u - 1), sy_ref, my_ref)
        stage_a(2 * u + 1, None, sy_ref, my_ref)
        stage_b(2 * u, sx_ref, mx_ref)
        return carry

    lax.fori_loop(0, n_pairs, pair_body, 0)
    j_pending = jnp.where(n_pairs == 0, j_sub, 2 * n_pairs - 1)

    @pl.when(n_far % 2 == 1)
    def _():
        stage_a(n_far - 1, None, sx_ref, mx_ref)
        stage_b(j_pending, sy_ref, my_ref)
        stage_b(n_far - 1, sx_ref, mx_ref)

    @pl.when(n_far % 2 == 0)
    def _():
        stage_b(j_pending, sy_ref, my_ref)

    lf = lam_ref[...]
    lam = (jnp.exp(jnp.sum(lf[0:1] * lf[1:2], axis=-1, keepdims=True))
           - jnp.exp(jnp.sum(lf[2:3] * lf[3:4], axis=-1, keepdims=True)) + lambda_init)
    l0 = jnp.sum(l_ref[0], axis=-1, keepdims=True)
    l1 = jnp.sum(l_ref[1], axis=-1, keepdims=True)
    o = acc_ref[0] / l0 - lam * (acc_ref[1] / l1)
    o = o * lax.rsqrt(jnp.mean(o * o, axis=-1, keepdims=True) + EPS)
    o_ref[...] = (o * gsub_ref[...] * (1.0 - lambda_init)).astype(o_ref.dtype)


def _diff_attention(qkv, td, ts, lam_qk, g_sub, B, S, D, t, lambda_init):
    n_heads = D // (2 * HEAD_DIM)
    nq = S // t
    w = 2 * HEAD_DIM
    kern = functools.partial(_diff_attn_kernel, t=t, lambda_init=lambda_init)
    return pl.pallas_call(
        kern,
        grid=(B, n_heads, nq),
        in_specs=[pl.BlockSpec((t, w), lambda b, h, i: (b * nq + i, h)),
                  pl.BlockSpec((S, w), lambda b, h, i: (b, n_heads + h)),
                  pl.BlockSpec((S, w), lambda b, h, i: (b, 2 * n_heads + h)),
                  pl.BlockSpec((2, LANES, LANES), lambda b, h, i: (h, 0, 0)),
                  pl.BlockSpec((2, LANES, LANES), lambda b, h, i: (h, 0, 0)),
                  pl.BlockSpec((4, HEAD_DIM), lambda b, h, i: (0, 0)),
                  pl.BlockSpec((1, w), lambda b, h, i: (0, 0))],
        out_specs=pl.BlockSpec((t, w), lambda b, h, i: (b * nq + i, h)),
        out_shape=jax.ShapeDtypeStruct((B * S, D), _BF16),
        scratch_shapes=[pltpu.VMEM((2, 3, t, t), _F32),
                        pltpu.VMEM((2, t, t), _F32),
                        pltpu.VMEM((2, t, t), _F32),
                        pltpu.VMEM((2, t, LANES), _F32),
                        pltpu.VMEM((2, t, LANES), _F32),
                        pltpu.VMEM((2, t, LANES), _F32),
                        pltpu.VMEM((2, t, LANES), _F32),
                        pltpu.VMEM((2, t, w), _F32)],
        compiler_params=_params("arbitrary", "arbitrary", "arbitrary"),
        name="diff_attention",
    )(qkv, qkv, qkv, td, ts, lam_qk, g_sub)


def _swa_kernel(q_ref, kc_ref, kp_ref, vc_ref, vp_ref, bias_ref, sink_ref, o_ref, *, tq):
    first = pl.program_id(2) == 0
    rows = SWA_GROUP * WINDOW
    bias = bias_ref[...].reshape(rows, 2 * WINDOW)
    sink = jnp.concatenate(
        [jnp.broadcast_to(sink_ref[r:r + 1, :], (WINDOW, LANES)) for r in range(SWA_GROUP)], axis=0)
    col = lax.broadcasted_iota(jnp.int32, (rows, 2 * WINDOW), 1)
    ones = jnp.ones((2 * WINDOW, LANES), _BF16)
    for c in range(tq // WINDOW):
        cur = slice(c * WINDOW, (c + 1) * WINDOW)
        if c == 0:
            k_prev, v_prev = kp_ref[...], vp_ref[...]
        else:
            prev = slice((c - 1) * WINDOW, c * WINDOW)
            k_prev, v_prev = kc_ref[prev, :], vc_ref[prev, :]
        kk = jnp.concatenate([k_prev, kc_ref[cur, :]], axis=0)
        vv = jnp.concatenate([jnp.concatenate([v_prev, vc_ref[cur, :]], axis=0), ones], axis=1)
        q4 = jnp.concatenate(
            [q_ref[cur, r * HEAD_DIM:(r + 1) * HEAD_DIM] for r in range(SWA_GROUP)], axis=0)
        s = lax.dot_general(q4, kk, _NT, preferred_element_type=_F32) + bias
        if c == 0:
            s = jnp.where(first & (col < WINDOW), NEG, s)
        part = functools.reduce(jnp.maximum, _lane_tiles(s))
        m = jnp.maximum(jnp.broadcast_to(jnp.max(part, axis=-1, keepdims=True), (rows, LANES)), sink)
        p = jnp.exp2(s - jnp.concatenate([m, m], axis=1))
        pv = jnp.dot(p.astype(_BF16), vv, preferred_element_type=_F32)
        o = pv[:, :HEAD_DIM] / (pv[:, HEAD_DIM:] + jnp.exp2(sink - m))
        for r in range(SWA_GROUP):
            o_ref[cur, r * HEAD_DIM:(r + 1) * HEAD_DIM] = (
                o[r * WINDOW:(r + 1) * WINDOW, :].astype(o_ref.dtype))


def _swa_attention(q, kv, bias, sinks, B, S, D, tq):
    n_kv = D // (SWA_GROUP * HEAD_DIM)
    nt = S // tq
    bpt = tq // WINDOW
    gw = SWA_GROUP * HEAD_DIM

    def prev_map(off):
        return lambda b, g, t: (jnp.maximum((b * nt + t) * bpt - 1, 0), off + g)

    kern = functools.partial(_swa_kernel, tq=tq)
    return pl.pallas_call(
        kern,
        grid=(B, n_kv, nt),
        in_specs=[pl.BlockSpec((tq, gw), lambda b, g, t: (b * nt + t, g)),
                  pl.BlockSpec((tq, HEAD_DIM), lambda b, g, t: (b * nt + t, g)),
                  pl.BlockSpec((WINDOW, HEAD_DIM), prev_map(0)),
                  pl.BlockSpec((tq, HEAD_DIM), lambda b, g, t: (b * nt + t, n_kv + g)),
                  pl.BlockSpec((WINDOW, HEAD_DIM), prev_map(n_kv)),
                  pl.BlockSpec((SWA_GROUP, WINDOW, 2 * WINDOW), lambda b, g, t: (g, 0, 0)),
                  pl.BlockSpec((None, SWA_GROUP, LANES), lambda b, g, t: (g, 0, 0))],
        out_specs=pl.BlockSpec((tq, gw), lambda b, g, t: (b * nt + t, g)),
        out_shape=jax.ShapeDtypeStruct((B * S, D), _BF16),
        compiler_params=_params("arbitrary", "arbitrary", "arbitrary"),
        name="swa_attention",
    )(q, kv, kv, kv, kv, bias, sinks)


def _t5_bucket(n):
    max_exact = NUM_BUCKETS // 2
    nf = jnp.maximum(n, 1).astype(_F32)
    large = max_exact + (jnp.log(nf / max_exact) / math.log(MAX_DISTANCE / max_exact)
                         * (NUM_BUCKETS - max_exact)).astype(jnp.int32)
    large = jnp.minimum(large, NUM_BUCKETS - 1)
    return jnp.where(n < max_exact, n, large)


def _bias_tiles(rel_bias):
    def lookup(table, idx):
        onehot = (idx[..., None] == jnp.arange(table.shape[0])).astype(_F32)
        return jnp.einsum("...d,dh->h...", onehot, table, precision=lax.Precision.HIGHEST)

    lut = lookup(rel_bias.astype(_F32), _t5_bucket(jnp.arange(MAX_DISTANCE))).T * LOG2E
    shifted = lut - lut[MAX_DISTANCE - 1]
    qi = jnp.arange(LANES)[:, None]
    kj = jnp.arange(LANES)[None, :]
    clip = lambda dist: jnp.clip(dist, 0, MAX_DISTANCE - 1)
    d0 = qi - kj
    td = jnp.where((d0 >= 0)[None], lookup(shifted, clip(d0)), NEG)
    ts = lookup(shifted, clip(LANES + qi - kj))
    dw = WINDOW + qi - jnp.arange(2 * WINDOW)[None, :]
    swa = jnp.where(((dw >= 0) & (dw < WINDOW))[None], lookup(lut, clip(dw)), NEG)
    return td, ts, swa


def _forward(x, rel_bias, g_attn_norm, g_ffn_norm, w_qkv_a, w_o_a, g_q_a, g_k_a, lam_qk_a, g_sub_a,
             g_kv_norm, w_kv, g_k_shared, w_q_b, w_o_b, g_q_b, sinks_b, w_gate, w_up, w_down, tiles):
    B, S, D = x.shape
    M = B * S
    T = tiles
    q_scale = HEAD_DIM ** -0.5 * LOG2E
    n_maps = D // HEAD_DIM
    kvw = w_kv.shape[1] // 2
    td, ts, swa_bias = _bias_tiles(rel_bias)
    bf = _cast_bf16
    x = x.reshape(M, D)

    def ffn(x, layer):
        h = _rmsnorm(x, g_ffn_norm[layer][None], T["rms_tm"])[0]
        act = _gate_up(h, w_gate, w_up, layer, T["mm_tm"], T["gu_tn"])
        return _down_residual(act, bf(w_down, layer), x, T["down_tm"], T["down_tk"], T["down_tn"])

    lambda_init = 0.8 - 0.6 * math.exp(-0.3 * 0)
    h = _rmsnorm(x, g_attn_norm[0][None], T["rms_tm"])[0]
    g_cols = jnp.concatenate([jnp.tile(g_q_a[0] * q_scale, n_maps), jnp.tile(g_k_a[0], n_maps),
                              jnp.ones((D,), _F32)])[None]
    qkv = _proj_headnorm(h, w_qkv_a, 0, g_cols, 2 * D, T["mm_tm"], T["qkv_tn"])
    o = _diff_attention(qkv, td, ts, lam_qk_a[0], g_sub_a[0][None], B, S, D, T["attn_t"], lambda_init)
    x = _proj_residual(o, w_o_a, 0, x, T["mm_tm"], T["res_tn"])
    x = ffn(x, 0)

    h_kv, h_q = _rmsnorm(x, jnp.stack([g_kv_norm, g_attn_norm[1]]), T["rms_tm"])
    g_kv_cols = jnp.concatenate([jnp.tile(g_k_shared, kvw // HEAD_DIM), jnp.ones((kvw,), _F32)])[None]
    kv = _proj_headnorm(h_kv, w_kv[None], 0, g_kv_cols, kvw, T["mm_tm"], min(T["qkv_tn"], kvw))
    q = _proj_headnorm(h_q, w_q_b, 0, jnp.tile(g_q_b[0] * q_scale, n_maps)[None], D,
                       T["mm_tm"], T["qkv_tn"])
    sinks = jnp.broadcast_to((sinks_b[0].astype(_F32) * LOG2E).reshape(-1, SWA_GROUP, 1),
                             (n_maps // SWA_GROUP, SWA_GROUP, LANES))
    o = _swa_attention(q, kv, swa_bias, sinks, B, S, D, T["swa_tq"])
    x = _proj_residual(o, w_o_b, 0, x, T["mm_tm"], T["res_tn"])
    x = ffn(x, 1)
    return x.reshape(B, S, D)


def kernel(x, rel_bias, g_attn_norm, g_ffn_norm, w_qkv_a, w_o_a, g_q_a, g_k_a, lam_qk_a, g_sub_a,
           g_kv_norm, w_kv, g_k_shared, w_q_b, w_o_b, g_q_b, sinks_b, w_gate, w_up, w_down):
    B, S, D = x.shape
    tiles = _tiles(B * S, S, D, w_gate.shape[-1])
    return _forward(x, rel_bias, g_attn_norm, g_ffn_norm, w_qkv_a, w_o_a, g_q_a, g_k_a, lam_qk_a,
                    g_sub_a, g_kv_norm, w_kv, g_k_shared, w_q_b, w_o_b, g_q_b, sinks_b,
                    w_gate, w_up, w_down, tiles)
```

```python
import functools
import math

import jax
import jax.numpy as jnp
from jax import lax
from jax.experimental import pallas as pl
from jax.experimental.pallas import tpu as pltpu

HEAD_DIM = 128
WINDOW = 128
SWA_GROUP = 4
NUM_BUCKETS = 32
MAX_DISTANCE = 128
EPS = 1e-6
NEG = -1e30
LOG2E = math.log2(math.e)
LANES = 128
VMEM_LIMIT_BYTES = 56 * 1024 * 1024
CAST_BLOCK_BYTES = 8 * 1024 * 1024
MXU_COLS = 256

_F32 = jnp.float32
_BF16 = jnp.bfloat16
_NT = (((1,), (1,)), ((), ()))


def _params(*sem):
    return pltpu.CompilerParams(dimension_semantics=sem, vmem_limit_bytes=VMEM_LIMIT_BYTES)


def _tiles(M, S, D, D_FF):
    return dict(
        rms_tm=min(256, M),
        mm_tm=min(1024, M),
        qkv_tn=min(1024, D),
        res_tn=min(512, D),
        gu_tn=min(256, D_FF),
        down_tm=min(512, M),
        down_tk=D_FF // 2,
        down_tn=min(1024, D),
        attn_t=min(512, S // 4),
        swa_tq=min(1024, S),
    )


def _lane_tiles(a):
    return [a[:, u * LANES:(u + 1) * LANES] for u in range(a.shape[1] // LANES)]


def _rep(a, width):
    return a if width == LANES else jnp.concatenate([a] * (width // LANES), axis=1)


def _lane_rep(v):
    return jnp.broadcast_to(v.astype(_F32)[:, None], (v.shape[0], LANES))


def _cast_kernel(w_ref, *rest):
    o_ref = rest[-1]
    w = w_ref[...]
    if len(rest) == 2:
        w = w * _rep(rest[0][...], w.shape[1])
    o_ref[...] = w.astype(o_ref.dtype)


def _cast_bf16(w, layer, row_gain=None):
    _, R, C = w.shape
    tr = min(R, max(8, CAST_BLOCK_BYTES // (4 * C) // 8 * 8))
    while R % tr:
        tr -= 8
    args, specs = [w], [pl.BlockSpec((None, tr, C), lambda i: (layer, i, 0))]
    if row_gain is not None:
        args.append(_lane_rep(row_gain))
        specs.append(pl.BlockSpec((tr, LANES), lambda i: (i, 0)))
    return pl.pallas_call(
        _cast_kernel,
        grid=(R // tr,),
        in_specs=specs,
        out_specs=pl.BlockSpec((tr, C), lambda i: (i, 0)),
        out_shape=jax.ShapeDtypeStruct((R, C), _BF16),
        compiler_params=_params("arbitrary"),
        name="cast_bf16",
    )(*args)


def _prep_kernel(x_ref, xb_ref, r_ref):
    x = x_ref[...]
    xb_ref[...] = x.astype(xb_ref.dtype)
    r_ref[...] = jnp.broadcast_to(lax.rsqrt(jnp.mean(x * x, axis=-1, keepdims=True) + EPS), r_ref.shape)


def _prep(x, tm):
    M, D = x.shape
    return pl.pallas_call(
        _prep_kernel,
        grid=(M // tm,),
        in_specs=[pl.BlockSpec((tm, D), lambda i: (i, 0))],
        out_specs=[pl.BlockSpec((tm, D), lambda i: (i, 0)), pl.BlockSpec((tm, LANES), lambda i: (i, 0))],
        out_shape=[jax.ShapeDtypeStruct((M, D), _BF16), jax.ShapeDtypeStruct((M, LANES), _F32)],
        compiler_params=_params("arbitrary"),
        name="rms_stats",
    )(x)


def _accumulate_inv_rms(r_ref, ssq, first, last, width):
    @pl.when(first)
    def _():
        r_ref[...] = ssq

    @pl.when(jnp.logical_not(first))
    def _():
        r_ref[...] += ssq

    @pl.when(last)
    def _():
        total = jnp.sum(r_ref[...], axis=-1, keepdims=True)
        r_ref[...] = jnp.broadcast_to(lax.rsqrt(total / width + EPS), r_ref.shape)


def _col_groups(width):
    sub = min(MXU_COLS, width)
    return [slice(u * sub, (u + 1) * sub) for u in range(width // sub)]


def _proj_headnorm_kernel(x_ref, r_ref, w_ref, g_ref, o_ref, *, n_norm_tiles, n_tiles):
    def sub_dots():
        for cols in _col_groups(o_ref.shape[1]):
            y = jnp.dot(x_ref[...], w_ref[:, cols], preferred_element_type=_F32)
            yield cols, _rep(r_ref[...], y.shape[1]) * y

    def normed():
        for cols, y in sub_dots():
            for c in range(y.shape[1] // HEAD_DIM):
                yc = y[:, c * HEAD_DIM:(c + 1) * HEAD_DIM]
                sl = slice(cols.start + c * HEAD_DIM, cols.start + (c + 1) * HEAD_DIM)
                r = lax.rsqrt(jnp.mean(yc * yc, axis=-1, keepdims=True) + EPS)
                o_ref[:, sl] = (yc * r * g_ref[:, sl]).astype(o_ref.dtype)

    def plain():
        for cols, y in sub_dots():
            o_ref[:, cols] = y.astype(o_ref.dtype)

    if n_norm_tiles == n_tiles:
        normed()
    else:
        j = pl.program_id(1)
        pl.when(j < n_norm_tiles)(normed)
        pl.when(j >= n_norm_tiles)(plain)


def _proj_headnorm(xb, inv_rms, w, g_cols, n_norm_cols, tm, tn):
    M, K = xb.shape
    N = w.shape[1]
    kern = functools.partial(_proj_headnorm_kernel, n_norm_tiles=n_norm_cols // tn, n_tiles=N // tn)
    return pl.pallas_call(
        kern,
        grid=(M // tm, N // tn),
        in_specs=[pl.BlockSpec((tm, K), lambda i, j: (i, 0)),
                  pl.BlockSpec((tm, LANES), lambda i, j: (i, 0)),
                  pl.BlockSpec((K, tn), lambda i, j: (0, j)),
                  pl.BlockSpec((1, tn), lambda i, j: (0, j))],
        out_specs=pl.BlockSpec((tm, tn), lambda i, j: (i, j)),
        out_shape=jax.ShapeDtypeStruct((M, N), _BF16),
        compiler_params=_params("arbitrary", "arbitrary"),
        name="proj_headnorm",
    )(xb, inv_rms, w, g_cols)


def _store_residual_tile(cols, y, o_ref, stat_refs, ssq):
    o_ref[:, cols] = y
    if not stat_refs:
        return None
    stat_refs[0][:, cols] = y.astype(_BF16)
    part = functools.reduce(jnp.add, _lane_tiles(y * y))
    return part if ssq is None else ssq + part


def _proj_residual_kernel(x_ref, w_ref, res_ref, o_ref, ob_ref, r_ref, *, width):
    j = pl.program_id(1)
    ssq = None
    for cols in _col_groups(o_ref.shape[1]):
        y = res_ref[:, cols] + jnp.dot(x_ref[...], w_ref[:, cols], preferred_element_type=_F32)
        ssq = _store_residual_tile(cols, y, o_ref, (ob_ref,), ssq)
    _accumulate_inv_rms(r_ref, ssq, j == 0, j == pl.num_programs(1) - 1, width)


def _proj_residual(x, w, res, tm, tn):
    M, K = x.shape
    N = w.shape[1]
    tile = pl.BlockSpec((tm, tn), lambda i, j: (i, j))
    return pl.pallas_call(
        functools.partial(_proj_residual_kernel, width=N),
        grid=(M // tm, N // tn),
        in_specs=[pl.BlockSpec((tm, K), lambda i, j: (i, 0)),
                  pl.BlockSpec((K, tn), lambda i, j: (0, j)),
                  tile],
        out_specs=[tile, tile, pl.BlockSpec((tm, LANES), lambda i, j: (i, 0))],
        out_shape=[jax.ShapeDtypeStruct((M, N), _F32), jax.ShapeDtypeStruct((M, N), _BF16),
                   jax.ShapeDtypeStruct((M, LANES), _F32)],
        compiler_params=_params("arbitrary", "arbitrary"),
        name="proj_residual",
    )(x, w, res)


def _gate_up_kernel(x_ref, r_ref, wg_ref, wu_ref, gain_ref, o_ref):
    x = x_ref[...]
    tn = o_ref.shape[1]
    gain = _rep(gain_ref[...], tn)
    r = _rep(r_ref[...], tn)
    g = r * jnp.dot(x, (wg_ref[...] * gain).astype(_BF16), preferred_element_type=_F32)
    u = r * jnp.dot(x, (wu_ref[...] * gain).astype(_BF16), preferred_element_type=_F32)
    o_ref[...] = (g * jax.nn.sigmoid(g) * u).astype(o_ref.dtype)


def _gate_up(xb, inv_rms, wg, wu, layer, gain, tm, tn):
    M, K = xb.shape
    N = wg.shape[2]
    w_spec = pl.BlockSpec((None, K, tn), lambda i, j: (layer, 0, j))
    return pl.pallas_call(
        _gate_up_kernel,
        grid=(M // tm, N // tn),
        in_specs=[pl.BlockSpec((tm, K), lambda i, j: (i, 0)),
                  pl.BlockSpec((tm, LANES), lambda i, j: (i, 0)),
                  w_spec, w_spec,
                  pl.BlockSpec((K, LANES), lambda i, j: (0, 0))],
        out_specs=pl.BlockSpec((tm, tn), lambda i, j: (i, j)),
        out_shape=jax.ShapeDtypeStruct((M, N), _BF16),
        compiler_params=_params("arbitrary", "arbitrary"),
        name="ffn_gate_up",
    )(xb, inv_rms, wg, wu, _lane_rep(gain))


def _down_kernel(x_ref, w_ref, res_ref, o_ref, *rest, width):
    *stat_refs, acc_ref = rest
    k = pl.program_id(1)
    j = pl.program_id(2)

    def parts():
        for cols in _col_groups(o_ref.shape[1]):
            yield cols, jnp.dot(x_ref[...], w_ref[:, cols], preferred_element_type=_F32)

    @pl.when(k == 0)
    def _():
        for cols, part in parts():
            acc_ref[j, :, cols] = part

    @pl.when(k == pl.num_programs(1) - 1)
    def _():
        ssq = None
        for cols, part in parts():
            y = res_ref[:, cols] + acc_ref[j, :, cols] + part
            ssq = _store_residual_tile(cols, y, o_ref, stat_refs, ssq)
        if stat_refs:
            _accumulate_inv_rms(stat_refs[1], ssq, j == 0, j == pl.num_programs(2) - 1, width)


def _down_residual(x, w, res, with_stats, tm, tk, tn):
    M, K = x.shape
    N = w.shape[1]
    nk = K // tk
    assert nk == 2
    tile = pl.BlockSpec((tm, tn), lambda i, k, j: (i, j * k))
    out_specs, out_shape = [tile], [jax.ShapeDtypeStruct((M, N), _F32)]
    if with_stats:
        out_specs += [tile, pl.BlockSpec((tm, LANES), lambda i, k, j: (i, 0))]
        out_shape += [jax.ShapeDtypeStruct((M, N), _BF16), jax.ShapeDtypeStruct((M, LANES), _F32)]
    return pl.pallas_call(
        functools.partial(_down_kernel, width=N),
        grid=(M // tm, nk, N // tn),
        in_specs=[pl.BlockSpec((tm, tk), lambda i, k, j: (i, k)),
                  pl.BlockSpec((tk, tn), lambda i, k, j: (k, j)),
                  tile],
        out_specs=out_specs,
        out_shape=out_shape,
        scratch_shapes=[pltpu.VMEM((N // tn, tm, tn), _F32)],
        compiler_params=_params("arbitrary", "arbitrary", "arbitrary"),
        name="ffn_down_residual",
    )(x, w, res)


def _diff_attn_kernel(q_ref, k_ref, v_ref, td_ref, ts_ref, lam_ref, gsub_ref, o_ref,
                      bias_ref, sx_ref, sy_ref, mx_ref, my_ref, m_ref, l_ref, acc_ref, *, t, lambda_init):
    i = pl.program_id(2)
    nt = t // LANES

    @pl.when(i == 0)
    def _build_bias():
        zeros = jnp.zeros((LANES, LANES), _F32)
        neg = jnp.full((LANES, LANES), NEG, _F32)
        for c in range(2):
            for a in range(nt):
                for b in range(nt):
                    rows = slice(a * LANES, (a + 1) * LANES)
                    cols = slice(b * LANES, (b + 1) * LANES)
                    if a == b:
                        tile = td_ref[c]
                    elif a == b + 1:
                        tile = ts_ref[c]
                    else:
                        tile = zeros if a > b else neg
                    bias_ref[c, 0, rows, cols] = tile
                    bias_ref[c, 1, rows, cols] = ts_ref[c] if (a == 0 and b == nt - 1) else zeros
                    bias_ref[c, 2, rows, cols] = neg

    m_ref[...] = jnp.full(m_ref.shape, NEG, _F32)
    l_ref[...] = jnp.zeros(l_ref.shape, _F32)
    acc_ref[...] = jnp.zeros(acc_ref.shape, _F32)

    def stage_a(j, kind, s_out, m_out):
        start = pl.multiple_of(j * t, t)
        for c in range(2):
            cols = slice(c * HEAD_DIM, (c + 1) * HEAD_DIM)
            s = lax.dot_general(q_ref[:, cols], k_ref[pl.ds(start, t), cols], _NT,
                                preferred_element_type=_F32)
            if kind is not None:
                s = s + bias_ref[c, kind]
            s_out[c] = s
            part = functools.reduce(jnp.maximum, _lane_tiles(s))
            m_out[c] = jnp.broadcast_to(jnp.max(part, axis=-1, keepdims=True), (t, LANES))

    def stage_b(j, s_in, m_in):
        start = pl.multiple_of(j * t, t)
        vj = v_ref[pl.ds(start, t), :]
        for c in range(2):
            m_old = m_ref[c]
            m_new = jnp.maximum(m_old, m_in[c])
            alpha = jnp.exp2(m_old - m_new)
            m_rep = jnp.concatenate([m_new] * (MXU_COLS // LANES), axis=1)
            psum = alpha * l_ref[c]
            pv = jnp.concatenate([alpha] * (acc_ref.shape[2] // LANES), axis=1) * acc_ref[c]
            for u in range(t // MXU_COLS):
                cols = slice(u * MXU_COLS, (u + 1) * MXU_COLS)
                p = jnp.exp2(s_in[c, :, cols] - m_rep)
                psum = psum + functools.reduce(jnp.add, _lane_tiles(p))
                pv = pv + jnp.dot(p.astype(_BF16), vj[cols, :], preferred_element_type=_F32)
            l_ref[c] = psum
            acc_ref[c] = pv
            m_ref[c] = m_new

    j_sub = jnp.maximum(i - 1, 0)
    n_far = jnp.maximum(i - 1, 0)
    n_pairs = n_far // 2
    stage_a(i, 0, sx_ref, mx_ref)
    stage_a(j_sub, jnp.where(i == 0, 2, 1), sy_ref, my_ref)
    stage_b(i, sx_ref, mx_ref)

    def pair_body(u, carry):
        stage_a(2 * u, None, sx_ref, mx_ref)
        stage_b(jnp.where(u == 0, j_sub, 2 * u - 1), sy_ref, my_ref)
        stage_a(2 * u + 1, None, sy_ref, my_ref)
        stage_b(2 * u, sx_ref, mx_ref)
        return carry

    lax.fori_loop(0, n_pairs, pair_body, 0)
    j_pending = jnp.where(n_pairs == 0, j_sub, 2 * n_pairs - 1)

    @pl.when(n_far % 2 == 1)
    def _():
        stage_a(n_far - 1, None, sx_ref, mx_ref)
        stage_b(j_pending, sy_ref, my_ref)
        stage_b(n_far - 1, sx_ref, mx_ref)

    @pl.when(n_far % 2 == 0)
    def _():
        stage_b(j_pending, sy_ref, my_ref)

    lf = lam_ref[...]
    lam = (jnp.exp(jnp.sum(lf[0:1] * lf[1:2], axis=-1, keepdims=True))
           - jnp.exp(jnp.sum(lf[2:3] * lf[3:4], axis=-1, keepdims=True)) + lambda_init)
    l0 = jnp.sum(l_ref[0], axis=-1, keepdims=True)
    l1 = jnp.sum(l_ref[1], axis=-1, keepdims=True)
    o = acc_ref[0] / l0 - lam * (acc_ref[1] / l1)
    o = o * lax.rsqrt(jnp.mean(o * o, axis=-1, keepdims=True) + EPS)
    o_ref[...] = (o * gsub_ref[...] * (1.0 - lambda_init)).astype(o_ref.dtype)


def _diff_attention(qkv, td, ts, lam_qk, g_sub, B, S, D, t, lambda_init):
    n_heads = D // (2 * HEAD_DIM)
    nq = S // t
    w = 2 * HEAD_DIM
    kern = functools.partial(_diff_attn_kernel, t=t, lambda_init=lambda_init)
    return pl.pallas_call(
        kern,
        grid=(B, n_heads, nq),
        in_specs=[pl.BlockSpec((t, w), lambda b, h, i: (b * nq + i, h)),
                  pl.BlockSpec((S, w), lambda b, h, i: (b, n_heads + h)),
                  pl.BlockSpec((S, w), lambda b, h, i: (b, 2 * n_heads + h)),
                  pl.BlockSpec((2, LANES, LANES), lambda b, h, i: (h, 0, 0)),
                  pl.BlockSpec((2, LANES, LANES), lambda b, h, i: (h, 0, 0)),
                  pl.BlockSpec((4, HEAD_DIM), lambda b, h, i: (0, 0)),
                  pl.BlockSpec((1, w), lambda b, h, i: (0, 0))],
        out_specs=pl.BlockSpec((t, w), lambda b, h, i: (b * nq + i, h)),
        out_shape=jax.ShapeDtypeStruct((B * S, D), _BF16),
        scratch_shapes=[pltpu.VMEM((2, 3, t, t), _F32),
                        pltpu.VMEM((2, t, t), _F32),
                        pltpu.VMEM((2, t, t), _F32),
                        pltpu.VMEM((2, t, LANES), _F32),
                        pltpu.VMEM((2, t, LANES), _F32),
                        pltpu.VMEM((2, t, LANES), _F32),
                        pltpu.VMEM((2, t, LANES), _F32),
                        pltpu.VMEM((2, t, w), _F32)],
        compiler_params=_params("arbitrary", "arbitrary", "arbitrary"),
        name="diff_attention",
    )(qkv, qkv, qkv, td, ts, lam_qk, g_sub)


def _swa_kernel(q_ref, kc_ref, kp_ref, vc_ref, vp_ref, bias_ref, sink_ref, o_ref, *, tq):
    first = pl.program_id(2) == 0
    rows = SWA_GROUP * WINDOW
    bias = bias_ref[...].reshape(rows, 2 * WINDOW)
    sink = jnp.concatenate(
        [jnp.broadcast_to(sink_ref[r:r + 1, :], (WINDOW, LANES)) for r in range(SWA_GROUP)], axis=0)
    col = lax.broadcasted_iota(jnp.int32, (rows, 2 * WINDOW), 1)
    ones = jnp.ones((2 * WINDOW, LANES), _BF16)
    for c in range(tq // WINDOW):
        cur = slice(c * WINDOW, (c + 1) * WINDOW)
        if c == 0:
            k_prev, v_prev = kp_ref[...], vp_ref[...]
        else:
            prev = slice((c - 1) * WINDOW, c * WINDOW)
            k_prev, v_prev = kc_ref[prev, :], vc_ref[prev, :]
        kk = jnp.concatenate([k_prev, kc_ref[cur, :]], axis=0)
        vv = jnp.concatenate([jnp.concatenate([v_prev, vc_ref[cur, :]], axis=0), ones], axis=1)
        q4 = jnp.concatenate(
            [q_ref[cur, r * HEAD_DIM:(r + 1) * HEAD_DIM] for r in range(SWA_GROUP)], axis=0)
        s = lax.dot_general(q4, kk, _NT, preferred_element_type=_F32) + bias
        if c == 0:
            s = jnp.where(first & (col < WINDOW), NEG, s)
        part = functools.reduce(jnp.maximum, _lane_tiles(s))
        m = jnp.maximum(jnp.broadcast_to(jnp.max(part, axis=-1, keepdims=True), (rows, LANES)), sink)
        p = jnp.exp2(s - jnp.concatenate([m, m], axis=1))
        pv = jnp.dot(p.astype(_BF16), vv, preferred_element_type=_F32)
        o = pv[:, :HEAD_DIM] / (pv[:, HEAD_DIM:] + jnp.exp2(sink - m))
        for r in range(SWA_GROUP):
            o_ref[cur, r * HEAD_DIM:(r + 1) * HEAD_DIM] = (
                o[r * WINDOW:(r + 1) * WINDOW, :].astype(o_ref.dtype))


def _swa_attention(q, kv, bias, sinks, B, S, D, tq):
    n_kv = D // (SWA_GROUP * HEAD_DIM)
    nt = S // tq
    bpt = tq // WINDOW
    gw = SWA_GROUP * HEAD_DIM

    def prev_map(off):
        return lambda b, g, t: (jnp.maximum((b * nt + t) * bpt - 1, 0), off + g)

    kern = functools.partial(_swa_kernel, tq=tq)
    return pl.pallas_call(
        kern,
        grid=(B, n_kv, nt),
        in_specs=[pl.BlockSpec((tq, gw), lambda b, g, t: (b * nt + t, g)),
                  pl.BlockSpec((tq, HEAD_DIM), lambda b, g, t: (b * nt + t, g)),
                  pl.BlockSpec((WINDOW, HEAD_DIM), prev_map(0)),
                  pl.BlockSpec((tq, HEAD_DIM), lambda b, g, t: (b * nt + t, n_kv + g)),
                  pl.BlockSpec((WINDOW, HEAD_DIM), prev_map(n_kv)),
                  pl.BlockSpec((SWA_GROUP, WINDOW, 2 * WINDOW), lambda b, g, t: (g, 0, 0)),
                  pl.BlockSpec((None, SWA_GROUP, LANES), lambda b, g, t: (g, 0, 0))],
        out_specs=pl.BlockSpec((tq, gw), lambda b, g, t: (b * nt + t, g)),
        out_shape=jax.ShapeDtypeStruct((B * S, D), _BF16),
        compiler_params=_params("arbitrary", "arbitrary", "arbitrary"),
        name="swa_attention",
    )(q, kv, kv, kv, kv, bias, sinks)


def _t5_bucket(n):
    max_exact = NUM_BUCKETS // 2
    nf = jnp.maximum(n, 1).astype(_F32)
    large = max_exact + (jnp.log(nf / max_exact) / math.log(MAX_DISTANCE / max_exact)
                         * (NUM_BUCKETS - max_exact)).astype(jnp.int32)
    large = jnp.minimum(large, NUM_BUCKETS - 1)
    return jnp.where(n < max_exact, n, large)


def _bias_tiles(rel_bias):
    def lookup(table, idx):
        onehot = (idx[..., None] == jnp.arange(table.shape[0])).astype(_F32)
        return jnp.einsum("...d,dh->h...", onehot, table, precision=lax.Precision.HIGHEST)

    lut = lookup(rel_bias.astype(_F32), _t5_bucket(jnp.arange(MAX_DISTANCE))).T * LOG2E
    shifted = lut - lut[MAX_DISTANCE - 1]
    qi = jnp.arange(LANES)[:, None]
    kj = jnp.arange(LANES)[None, :]
    clip = lambda dist: jnp.clip(dist, 0, MAX_DISTANCE - 1)
    d0 = qi - kj
    td = jnp.where((d0 >= 0)[None], lookup(shifted, clip(d0)), NEG)
    ts = lookup(shifted, clip(LANES + qi - kj))
    dw = WINDOW + qi - jnp.arange(2 * WINDOW)[None, :]
    swa = jnp.where(((dw >= 0) & (dw < WINDOW))[None], lookup(lut, clip(dw)), NEG)
    return td, ts, swa


def _forward(x, rel_bias, g_attn_norm, g_ffn_norm, w_qkv_a, w_o_a, g_q_a, g_k_a, lam_qk_a, g_sub_a,
             g_kv_norm, w_kv, g_k_shared, w_q_b, w_o_b, g_q_b, sinks_b, w_gate, w_up, w_down, tiles):
    B, S, D = x.shape
    M = B * S
    T = tiles
    q_scale = HEAD_DIM ** -0.5 * LOG2E
    n_maps = D // HEAD_DIM
    kvw = w_kv.shape[1] // 2
    td, ts, swa_bias = _bias_tiles(rel_bias)
    bf = _cast_bf16
    x = x.reshape(M, D)

    def ffn(x, xb, inv_rms, layer, with_stats):
        act = _gate_up(xb, inv_rms, w_gate, w_up, layer, g_ffn_norm[layer], T["mm_tm"], T["gu_tn"])
        return _down_residual(act, bf(w_down, layer), x, with_stats, T["down_tm"], T["down_tk"], T["down_tn"])

    lambda_init = 0.8 - 0.6 * math.exp(-0.3 * 0)
    xb, inv_rms = _prep(x, T["rms_tm"])
    g_cols = jnp.concatenate([jnp.tile(g_q_a[0] * q_scale, n_maps), jnp.tile(g_k_a[0], n_maps),
                              jnp.ones((D,), _F32)])[None]
    qkv = _proj_headnorm(xb, inv_rms, bf(w_qkv_a, 0, g_attn_norm[0]), g_cols, 2 * D, T["mm_tm"], T["qkv_tn"])
    o = _diff_attention(qkv, td, ts, lam_qk_a[0], g_sub_a[0][None], B, S, D, T["attn_t"], lambda_init)
    x, xb, inv_rms = _proj_residual(o, bf(w_o_a, 0), x, T["mm_tm"], T["res_tn"])
    x, xb, inv_rms = ffn(x, xb, inv_rms, 0, True)

    g_kv_cols = jnp.concatenate([jnp.tile(g_k_shared, kvw // HEAD_DIM), jnp.ones((kvw,), _F32)])[None]
    kv = _proj_headnorm(xb, inv_rms, bf(w_kv[None], 0, g_kv_norm), g_kv_cols, kvw,
                        T["mm_tm"], min(T["qkv_tn"], kvw))
    q = _proj_headnorm(xb, inv_rms, bf(w_q_b, 0, g_attn_norm[1]), jnp.tile(g_q_b[0] * q_scale, n_maps)[None],
                       D, T["mm_tm"], T["qkv_tn"])
    sinks = jnp.broadcast_to((sinks_b[0].astype(_F32) * LOG2E).reshape(-1, SWA_GROUP, 1),
                             (n_maps // SWA_GROUP, SWA_GROUP, LANES))
    o = _swa_attention(q, kv, swa_bias, sinks, B, S, D, T["swa_tq"])
    x, xb, inv_rms = _proj_residual(o, bf(w_o_b, 0), x, T["mm_tm"], T["res_tn"])
    (x,) = ffn(x, xb, inv_rms, 1, False)
    return x.reshape(B, S, D)


def kernel(x, rel_bias, g_attn_norm, g_ffn_norm, w_qkv_a, w_o_a, g_q_a, g_k_a, lam_qk_a, g_sub_a,
           g_kv_norm, w_kv, g_k_shared, w_q_b, w_o_b, g_q_b, sinks_b, w_gate, w_up, w_down):
    B, S, D = x.shape
    tiles = _tiles(B * S, S, D, w_gate.shape[-1])
    return _forward(x, rel_bias, g_attn_norm, g_ffn_norm, w_qkv_a, w_o_a, g_q_a, g_k_a, lam_qk_a,
                    g_sub_a, g_kv_norm, w_kv, g_k_shared, w_q_b, w_o_b, g_q_b, sinks_b,
                    w_gate, w_up, w_down, tiles)
```

```python
import functools
import math

import jax
import jax.numpy as jnp
from jax import lax
from jax.experimental import pallas as pl
from jax.experimental.pallas import tpu as pltpu

HEAD_DIM = 128
WINDOW = 128
SWA_GROUP = 4
NUM_BUCKETS = 32
MAX_DISTANCE = 128
EPS = 1e-6
NEG = -1e30
LOG2E = math.log2(math.e)
LANES = 128
VMEM_LIMIT_BYTES = 56 * 1024 * 1024
CAST_BLOCK_BYTES = 8 * 1024 * 1024
MXU_COLS = 256

_F32 = jnp.float32
_BF16 = jnp.bfloat16
_NT = (((1,), (1,)), ((), ()))


def _params(*sem):
    return pltpu.CompilerParams(dimension_semantics=sem, vmem_limit_bytes=VMEM_LIMIT_BYTES)


def _tiles(M, S, D, D_FF):
    return dict(
        rms_tm=min(256, M),
        mm_tm=min(1024, M),
        qkv_tn=min(1024, D),
        res_tn=min(512, D),
        gu_tn=min(256, D_FF),
        down_tm=min(512, M),
        down_tk=D_FF // 2,
        down_tn=min(1024, D),
        attn_t=min(512, S // 4),
        swa_tq=min(1024, S),
    )


def _lane_tiles(a):
    return [a[:, u * LANES:(u + 1) * LANES] for u in range(a.shape[1] // LANES)]


def _rep(a, width):
    return a if width == LANES else jnp.concatenate([a] * (width // LANES), axis=1)


def _lane_rep(v):
    return jnp.broadcast_to(v.astype(_F32)[:, None], (v.shape[0], LANES))


def _cast_kernel(w_ref, *rest):
    o_ref = rest[-1]
    w = w_ref[...]
    if len(rest) == 2:
        w = w * _rep(rest[0][...], w.shape[1])
    o_ref[...] = w.astype(o_ref.dtype)


def _cast_bf16(w, layer, row_gain=None):
    _, R, C = w.shape
    tr = min(R, max(8, CAST_BLOCK_BYTES // (4 * C) // 8 * 8))
    while R % tr:
        tr -= 8
    args, specs = [w], [pl.BlockSpec((None, tr, C), lambda i: (layer, i, 0))]
    if row_gain is not None:
        args.append(_lane_rep(row_gain))
        specs.append(pl.BlockSpec((tr, LANES), lambda i: (i, 0)))
    return pl.pallas_call(
        _cast_kernel,
        grid=(R // tr,),
        in_specs=specs,
        out_specs=pl.BlockSpec((tr, C), lambda i: (i, 0)),
        out_shape=jax.ShapeDtypeStruct((R, C), _BF16),
        compiler_params=_params("arbitrary"),
        name="cast_bf16",
    )(*args)


def _prep_kernel(x_ref, xb_ref, r_ref):
    x = x_ref[...]
    xb_ref[...] = x.astype(xb_ref.dtype)
    r_ref[...] = jnp.broadcast_to(lax.rsqrt(jnp.mean(x * x, axis=-1, keepdims=True) + EPS), r_ref.shape)


def _prep(x, tm):
    M, D = x.shape
    return pl.pallas_call(
        _prep_kernel,
        grid=(M // tm,),
        in_specs=[pl.BlockSpec((tm, D), lambda i: (i, 0))],
        out_specs=[pl.BlockSpec((tm, D), lambda i: (i, 0)), pl.BlockSpec((tm, LANES), lambda i: (i, 0))],
        out_shape=[jax.ShapeDtypeStruct((M, D), _BF16), jax.ShapeDtypeStruct((M, LANES), _F32)],
        compiler_params=_params("arbitrary"),
        name="rms_stats",
    )(x)


def _accumulate_inv_rms(r_ref, ssq, first, last, width):
    @pl.when(first)
    def _():
        r_ref[...] = ssq

    @pl.when(jnp.logical_not(first))
    def _():
        r_ref[...] += ssq

    @pl.when(last)
    def _():
        total = jnp.sum(r_ref[...], axis=-1, keepdims=True)
        r_ref[...] = jnp.broadcast_to(lax.rsqrt(total / width + EPS), r_ref.shape)


def _col_groups(width):
    sub = min(MXU_COLS, width)
    return [slice(u * sub, (u + 1) * sub) for u in range(width // sub)]


def _proj_headnorm_kernel(x_ref, r_ref, w_ref, g_ref, o_ref, *, n_norm_tiles, n_tiles):
    def sub_dots():
        for cols in _col_groups(o_ref.shape[1]):
            y = jnp.dot(x_ref[...], w_ref[:, cols], preferred_element_type=_F32)
            yield cols, _rep(r_ref[...], y.shape[1]) * y

    def normed():
        for cols, y in sub_dots():
            for c in range(y.shape[1] // HEAD_DIM):
                yc = y[:, c * HEAD_DIM:(c + 1) * HEAD_DIM]
                sl = slice(cols.start + c * HEAD_DIM, cols.start + (c + 1) * HEAD_DIM)
                r = lax.rsqrt(jnp.mean(yc * yc, axis=-1, keepdims=True) + EPS)
                o_ref[:, sl] = (yc * r * g_ref[:, sl]).astype(o_ref.dtype)

    def plain():
        for cols, y in sub_dots():
            o_ref[:, cols] = y.astype(o_ref.dtype)

    if n_norm_tiles == n_tiles:
        normed()
    else:
        j = pl.program_id(1)
        pl.when(j < n_norm_tiles)(normed)
        pl.when(j >= n_norm_tiles)(plain)


def _proj_headnorm(xb, inv_rms, w, g_cols, n_norm_cols, tm, tn):
    M, K = xb.shape
    N = w.shape[1]
    kern = functools.partial(_proj_headnorm_kernel, n_norm_tiles=n_norm_cols // tn, n_tiles=N // tn)
    return pl.pallas_call(
        kern,
        grid=(M // tm, N // tn),
        in_specs=[pl.BlockSpec((tm, K), lambda i, j: (i, 0)),
                  pl.BlockSpec((tm, LANES), lambda i, j: (i, 0)),
                  pl.BlockSpec((K, tn), lambda i, j: (0, j)),
                  pl.BlockSpec((1, tn), lambda i, j: (0, j))],
        out_specs=pl.BlockSpec((tm, tn), lambda i, j: (i, j)),
        out_shape=jax.ShapeDtypeStruct((M, N), _BF16),
        compiler_params=_params("arbitrary", "arbitrary"),
        name="proj_headnorm",
    )(xb, inv_rms, w, g_cols)


def _store_residual_tile(cols, y, o_ref, stat_refs, ssq):
    o_ref[:, cols] = y
    if not stat_refs:
        return None
    stat_refs[0][:, cols] = y.astype(_BF16)
    part = functools.reduce(jnp.add, _lane_tiles(y * y))
    return part if ssq is None else ssq + part


def _proj_residual_kernel(x_ref, w_ref, res_ref, o_ref, ob_ref, r_ref, *, width):
    j = pl.program_id(1)
    ssq = None
    for cols in _col_groups(o_ref.shape[1]):
        y = res_ref[:, cols] + jnp.dot(x_ref[...], w_ref[:, cols], preferred_element_type=_F32)
        ssq = _store_residual_tile(cols, y, o_ref, (ob_ref,), ssq)
    _accumulate_inv_rms(r_ref, ssq, j == 0, j == pl.num_programs(1) - 1, width)


def _proj_residual(x, w, res, tm, tn):
    M, K = x.shape
    N = w.shape[1]
    tile = pl.BlockSpec((tm, tn), lambda i, j: (i, j))
    return pl.pallas_call(
        functools.partial(_proj_residual_kernel, width=N),
        grid=(M // tm, N // tn),
        in_specs=[pl.BlockSpec((tm, K), lambda i, j: (i, 0)),
                  pl.BlockSpec((K, tn), lambda i, j: (0, j)),
                  tile],
        out_specs=[tile, tile, pl.BlockSpec((tm, LANES), lambda i, j: (i, 0))],
        out_shape=[jax.ShapeDtypeStruct((M, N), _F32), jax.ShapeDtypeStruct((M, N), _BF16),
                   jax.ShapeDtypeStruct((M, LANES), _F32)],
        compiler_params=_params("arbitrary", "arbitrary"),
        name="proj_residual",
    )(x, w, res)


def _gate_up_kernel(x_ref, r_ref, wg_ref, wu_ref, gain_ref, o_ref):
    x = x_ref[...]
    tn = o_ref.shape[1]
    gain = _rep(gain_ref[...], tn)
    r = _rep(r_ref[...], tn)
    g = r * jnp.dot(x, (wg_ref[...] * gain).astype(_BF16), preferred_element_type=_F32)
    u = r * jnp.dot(x, (wu_ref[...] * gain).astype(_BF16), preferred_element_type=_F32)
    o_ref[...] = (g * jax.nn.sigmoid(g) * u).astype(o_ref.dtype)


def _gate_up(xb, inv_rms, wg, wu, layer, gain, tm, tn):
    M, K = xb.shape
    N = wg.shape[2]
    w_spec = pl.BlockSpec((None, K, tn), lambda i, j: (layer, 0, j))
    return pl.pallas_call(
        _gate_up_kernel,
        grid=(M // tm, N // tn),
        in_specs=[pl.BlockSpec((tm, K), lambda i, j: (i, 0)),
                  pl.BlockSpec((tm, LANES), lambda i, j: (i, 0)),
                  w_spec, w_spec,
                  pl.BlockSpec((K, LANES), lambda i, j: (0, 0))],
        out_specs=pl.BlockSpec((tm, tn), lambda i, j: (i, j)),
        out_shape=jax.ShapeDtypeStruct((M, N), _BF16),
        compiler_params=_params("arbitrary", "arbitrary"),
        name="ffn_gate_up",
    )(xb, inv_rms, wg, wu, _lane_rep(gain))


def _down_kernel(x_ref, w_ref, res_ref, o_ref, *rest, width):
    *stat_refs, acc_ref = rest
    k = pl.program_id(1)
    j = pl.program_id(2)

    def parts():
        for cols in _col_groups(o_ref.shape[1]):
            yield cols, jnp.dot(x_ref[...], w_ref[:, cols], preferred_element_type=_F32)

    @pl.when(k == 0)
    def _():
        for cols, part in parts():
            acc_ref[j, :, cols] = part

    @pl.when(k == pl.num_programs(1) - 1)
    def _():
        ssq = None
        for cols, part in parts():
            y = res_ref[:, cols] + acc_ref[j, :, cols] + part
            ssq = _store_residual_tile(cols, y, o_ref, stat_refs, ssq)
        if stat_refs:
            _accumulate_inv_rms(stat_refs[1], ssq, j == 0, j == pl.num_programs(2) - 1, width)


def _down_residual(x, w, res, with_stats, tm, tk, tn):
    M, K = x.shape
    N = w.shape[1]
    nk = K // tk
    assert nk == 2
    tile = pl.BlockSpec((tm, tn), lambda i, k, j: (i, j * k))
    out_specs, out_shape = [tile], [jax.ShapeDtypeStruct((M, N), _F32)]
    if with_stats:
        out_specs += [tile, pl.BlockSpec((tm, LANES), lambda i, k, j: (i, 0))]
        out_shape += [jax.ShapeDtypeStruct((M, N), _BF16), jax.ShapeDtypeStruct((M, LANES), _F32)]
    return pl.pallas_call(
        functools.partial(_down_kernel, width=N),
        grid=(M // tm, nk, N // tn),
        in_specs=[pl.BlockSpec((tm, tk), lambda i, k, j: (i, k)),
                  pl.BlockSpec((tk, tn), lambda i, k, j: (k, j)),
                  tile],
        out_specs=out_specs,
        out_shape=out_shape,
        scratch_shapes=[pltpu.VMEM((N // tn, tm, tn), _F32)],
        compiler_params=_params("arbitrary", "arbitrary", "arbitrary"),
        name="ffn_down_residual",
    )(x, w, res)


def _diff_attn_kernel(q_ref, k_ref, v_ref, td_ref, ts_ref, lam_ref, gsub_ref, o_ref,
                      bias_ref, sx_ref, sy_ref, sz_ref, mx_ref, my_ref, mz_ref, m_ref, l_ref, acc_ref,
                      *, t, lambda_init):
    i = pl.program_id(2)
    nt = t // LANES

    @pl.when(i == 0)
    def _build_bias():
        zeros = jnp.zeros((LANES, LANES), _F32)
        neg = jnp.full((LANES, LANES), NEG, _F32)
        for c in range(2):
            for a in range(nt):
                for b in range(nt):
                    rows = slice(a * LANES, (a + 1) * LANES)
                    cols = slice(b * LANES, (b + 1) * LANES)
                    if a == b:
                        tile = td_ref[c]
                    elif a == b + 1:
                        tile = ts_ref[c]
                    else:
                        tile = zeros if a > b else neg
                    bias_ref[c, 0, rows, cols] = tile
                    bias_ref[c, 1, rows, cols] = ts_ref[c] if (a == 0 and b == nt - 1) else zeros
                    bias_ref[c, 2, rows, cols] = neg

    def stage_a(iq, j, kind, s_out, m_out):
        q_start = pl.multiple_of(iq * t, t)
        start = pl.multiple_of(j * t, t)
        for c in range(2):
            cols = slice(c * HEAD_DIM, (c + 1) * HEAD_DIM)
            s = lax.dot_general(q_ref[pl.ds(q_start, t), cols], k_ref[pl.ds(start, t), cols], _NT,
                                preferred_element_type=_F32)
            if kind is not None:
                s = s + bias_ref[c, kind]
            s_out[c] = s
            part = functools.reduce(jnp.maximum, _lane_tiles(s))
            m_out[c] = jnp.broadcast_to(jnp.max(part, axis=-1, keepdims=True), (t, LANES))

    def stage_b(j, s_in, m_in, first=False):
        start = pl.multiple_of(j * t, t)
        vj = v_ref[pl.ds(start, t), :]
        for c in range(2):
            if first:
                m_new = m_in[c]
                psum = jnp.zeros((t, LANES), _F32)
                pv = jnp.zeros(acc_ref.shape[1:], _F32)
            else:
                m_old = m_ref[c]
                m_new = jnp.maximum(m_old, m_in[c])
                alpha = jnp.exp2(m_old - m_new)
                psum = alpha * l_ref[c]
                pv = _rep(alpha, acc_ref.shape[2]) * acc_ref[c]
            m_rep = _rep(m_new, MXU_COLS)
            for u in range(t // MXU_COLS):
                cols = slice(u * MXU_COLS, (u + 1) * MXU_COLS)
                p = jnp.exp2(s_in[c, :, cols] - m_rep)
                psum = psum + functools.reduce(jnp.add, _lane_tiles(p))
                pv = pv + jnp.dot(p.astype(_BF16), vj[cols, :], preferred_element_type=_F32)
            l_ref[c] = psum
            acc_ref[c] = pv
            m_ref[c] = m_new

    j_sub = jnp.maximum(i - 1, 0)
    n_far = jnp.maximum(i - 1, 0)
    n_pairs = n_far // 2
    i_next = jnp.minimum(i + 1, pl.num_programs(2) - 1)

    @pl.when(i == 0)
    def _():
        stage_a(0, 0, 0, sz_ref, mz_ref)

    stage_a(i, j_sub, jnp.where(i == 0, 2, 1), sy_ref, my_ref)
    stage_b(i, sz_ref, mz_ref, first=True)

    def pair_body(u, carry):
        stage_a(i, 2 * u, None, sx_ref, mx_ref)
        stage_b(jnp.where(u == 0, j_sub, 2 * u - 1), sy_ref, my_ref)
        stage_a(i, 2 * u + 1, None, sy_ref, my_ref)
        stage_b(2 * u, sx_ref, mx_ref)
        return carry

    lax.fori_loop(0, n_pairs, pair_body, 0)
    j_pending = jnp.where(n_pairs == 0, j_sub, 2 * n_pairs - 1)

    @pl.when(n_far % 2 == 1)
    def _():
        stage_a(i, n_far - 1, None, sx_ref, mx_ref)
        stage_b(j_pending, sy_ref, my_ref)
        stage_a(i_next, i_next, 0, sz_ref, mz_ref)
        stage_b(n_far - 1, sx_ref, mx_ref)

    @pl.when(n_far % 2 == 0)
    def _():
        stage_a(i_next, i_next, 0, sz_ref, mz_ref)
        stage_b(j_pending, sy_ref, my_ref)

    lf = lam_ref[...]
    lam = (jnp.exp(jnp.sum(lf[0:1] * lf[1:2], axis=-1, keepdims=True))
           - jnp.exp(jnp.sum(lf[2:3] * lf[3:4], axis=-1, keepdims=True)) + lambda_init)
    l0 = jnp.sum(l_ref[0], axis=-1, keepdims=True)
    l1 = jnp.sum(l_ref[1], axis=-1, keepdims=True)
    o = acc_ref[0] / l0 - lam * (acc_ref[1] / l1)
    o = o * lax.rsqrt(jnp.mean(o * o, axis=-1, keepdims=True) + EPS)
    o_ref[...] = (o * gsub_ref[...] * (1.0 - lambda_init)).astype(o_ref.dtype)


def _diff_attention(qkv, td, ts, lam_qk, g_sub, B, S, D, t, lambda_init):
    n_heads = D // (2 * HEAD_DIM)
    nq = S // t
    w = 2 * HEAD_DIM
    kern = functools.partial(_diff_attn_kernel, t=t, lambda_init=lambda_init)
    return pl.pallas_call(
        kern,
        grid=(B, n_heads, nq),
        in_specs=[pl.BlockSpec((S, w), lambda b, h, i: (b, h)),
                  pl.BlockSpec((S, w), lambda b, h, i: (b, n_heads + h)),
                  pl.BlockSpec((S, w), lambda b, h, i: (b, 2 * n_heads + h)),
                  pl.BlockSpec((2, LANES, LANES), lambda b, h, i: (h, 0, 0)),
                  pl.BlockSpec((2, LANES, LANES), lambda b, h, i: (h, 0, 0)),
                  pl.BlockSpec((4, HEAD_DIM), lambda b, h, i: (0, 0)),
                  pl.BlockSpec((1, w), lambda b, h, i: (0, 0))],
        out_specs=pl.BlockSpec((t, w), lambda b, h, i: (b * nq + i, h)),
        out_shape=jax.ShapeDtypeStruct((B * S, D), _BF16),
        scratch_shapes=[pltpu.VMEM((2, 3, t, t), _F32),
                        pltpu.VMEM((2, t, t), _F32),
                        pltpu.VMEM((2, t, t), _F32),
                        pltpu.VMEM((2, t, t), _F32),
                        pltpu.VMEM((2, t, LANES), _F32),
                        pltpu.VMEM((2, t, LANES), _F32),
                        pltpu.VMEM((2, t, LANES), _F32),
                        pltpu.VMEM((2, t, LANES), _F32),
                        pltpu.VMEM((2, t, LANES), _F32),
                        pltpu.VMEM((2, t, w), _F32)],
        compiler_params=_params("arbitrary", "arbitrary", "arbitrary"),
        name="diff_attention",
    )(qkv, qkv, qkv, td, ts, lam_qk, g_sub)


def _swa_kernel(q_ref, kc_ref, kp_ref, vc_ref, vp_ref, bias_ref, sink_ref, o_ref, *, tq):
    first = pl.program_id(2) == 0
    rows = SWA_GROUP * WINDOW
    bias = bias_ref[...].reshape(rows, 2 * WINDOW)
    sink = jnp.concatenate(
        [jnp.broadcast_to(sink_ref[r:r + 1, :], (WINDOW, LANES)) for r in range(SWA_GROUP)], axis=0)
    col = lax.broadcasted_iota(jnp.int32, (rows, 2 * WINDOW), 1)
    ones = jnp.ones((2 * WINDOW, LANES), _BF16)
    for c in range(tq // WINDOW):
        cur = slice(c * WINDOW, (c + 1) * WINDOW)
        if c == 0:
            k_prev, v_prev = kp_ref[...], vp_ref[...]
        else:
            prev = slice((c - 1) * WINDOW, c * WINDOW)
            k_prev, v_prev = kc_ref[prev, :], vc_ref[prev, :]
        kk = jnp.concatenate([k_prev, kc_ref[cur, :]], axis=0)
        vv = jnp.concatenate([jnp.concatenate([v_prev, vc_ref[cur, :]], axis=0), ones], axis=1)
        q4 = jnp.concatenate(
            [q_ref[cur, r * HEAD_DIM:(r + 1) * HEAD_DIM] for r in range(SWA_GROUP)], axis=0)
        s = lax.dot_general(q4, kk, _NT, preferred_element_type=_F32) + bias
        if c == 0:
            s = jnp.where(first & (col < WINDOW), NEG, s)
        part = functools.reduce(jnp.maximum, _lane_tiles(s))
        m = jnp.maximum(jnp.broadcast_to(jnp.max(part, axis=-1, keepdims=True), (rows, LANES)), sink)
        p = jnp.exp2(s - jnp.concatenate([m, m], axis=1))
        pv = jnp.dot(p.astype(_BF16), vv, preferred_element_type=_F32)
        o = pv[:, :HEAD_DIM] / (pv[:, HEAD_DIM:] + jnp.exp2(sink - m))
        for r in range(SWA_GROUP):
            o_ref[cur, r * HEAD_DIM:(r + 1) * HEAD_DIM] = (
                o[r * WINDOW:(r + 1) * WINDOW, :].astype(o_ref.dtype))


def _swa_attention(q, kv, bias, sinks, B, S, D, tq):
    n_kv = D // (SWA_GROUP * HEAD_DIM)
    nt = S // tq
    bpt = tq // WINDOW
    gw = SWA_GROUP * HEAD_DIM

    def prev_map(off):
        return lambda b, g, t: (jnp.maximum((b * nt + t) * bpt - 1, 0), off + g)

    kern = functools.partial(_swa_kernel, tq=tq)
    return pl.pallas_call(
        kern,
        grid=(B, n_kv, nt),
        in_specs=[pl.BlockSpec((tq, gw), lambda b, g, t: (b * nt + t, g)),
                  pl.BlockSpec((tq, HEAD_DIM), lambda b, g, t: (b * nt + t, g)),
                  pl.BlockSpec((WINDOW, HEAD_DIM), prev_map(0)),
                  pl.BlockSpec((tq, HEAD_DIM), lambda b, g, t: (b * nt + t, n_kv + g)),
                  pl.BlockSpec((WINDOW, HEAD_DIM), prev_map(n_kv)),
                  pl.BlockSpec((SWA_GROUP, WINDOW, 2 * WINDOW), lambda b, g, t: (g, 0, 0)),
                  pl.BlockSpec((None, SWA_GROUP, LANES), lambda b, g, t: (g, 0, 0))],
        out_specs=pl.BlockSpec((tq, gw), lambda b, g, t: (b * nt + t, g)),
        out_shape=jax.ShapeDtypeStruct((B * S, D), _BF16),
        compiler_params=_params("arbitrary", "arbitrary", "arbitrary"),
        name="swa_attention",
    )(q, kv, kv, kv, kv, bias, sinks)


def _t5_bucket(n):
    max_exact = NUM_BUCKETS // 2
    nf = jnp.maximum(n, 1).astype(_F32)
    large = max_exact + (jnp.log(nf / max_exact) / math.log(MAX_DISTANCE / max_exact)
                         * (NUM_BUCKETS - max_exact)).astype(jnp.int32)
    large = jnp.minimum(large, NUM_BUCKETS - 1)
    return jnp.where(n < max_exact, n, large)


def _bias_tiles(rel_bias):
    def lookup(table, idx):
        onehot = (idx[..., None] == jnp.arange(table.shape[0])).astype(_F32)
        return jnp.einsum("...d,dh->h...", onehot, table, precision=lax.Precision.HIGHEST)

    lut = lookup(rel_bias.astype(_F32), _t5_bucket(jnp.arange(MAX_DISTANCE))).T * LOG2E
    shifted = lut - lut[MAX_DISTANCE - 1]
    qi = jnp.arange(LANES)[:, None]
    kj = jnp.arange(LANES)[None, :]
    clip = lambda dist: jnp.clip(dist, 0, MAX_DISTANCE - 1)
    d0 = qi - kj
    td = jnp.where((d0 >= 0)[None], lookup(shifted, clip(d0)), NEG)
    ts = lookup(shifted, clip(LANES + qi - kj))
    dw = WINDOW + qi - jnp.arange(2 * WINDOW)[None, :]
    swa = jnp.where(((dw >= 0) & (dw < WINDOW))[None], lookup(lut, clip(dw)), NEG)
    return td, ts, swa


def _forward(x, rel_bias, g_attn_norm, g_ffn_norm, w_qkv_a, w_o_a, g_q_a, g_k_a, lam_qk_a, g_sub_a,
             g_kv_norm, w_kv, g_k_shared, w_q_b, w_o_b, g_q_b, sinks_b, w_gate, w_up, w_down, tiles):
    B, S, D = x.shape
    M = B * S
    T = tiles
    q_scale = HEAD_DIM ** -0.5 * LOG2E
    n_maps = D // HEAD_DIM
    kvw = w_kv.shape[1] // 2
    td, ts, swa_bias = _bias_tiles(rel_bias)
    bf = _cast_bf16
    x = x.reshape(M, D)

    def ffn(x, xb, inv_rms, layer, with_stats):
        act = _gate_up(xb, inv_rms, w_gate, w_up, layer, g_ffn_norm[layer], T["mm_tm"], T["gu_tn"])
        return _down_residual(act, bf(w_down, layer), x, with_stats, T["down_tm"], T["down_tk"], T["down_tn"])

    lambda_init = 0.8 - 0.6 * math.exp(-0.3 * 0)
    xb, inv_rms = _prep(x, T["rms_tm"])
    g_cols = jnp.concatenate([jnp.tile(g_q_a[0] * q_scale, n_maps), jnp.tile(g_k_a[0], n_maps),
                              jnp.ones((D,), _F32)])[None]
    qkv = _proj_headnorm(xb, inv_rms, bf(w_qkv_a, 0, g_attn_norm[0]), g_cols, 2 * D, T["mm_tm"], T["qkv_tn"])
    o = _diff_attention(qkv, td, ts, lam_qk_a[0], g_sub_a[0][None], B, S, D, T["attn_t"], lambda_init)
    x, xb, inv_rms = _proj_residual(o, bf(w_o_a, 0), x, T["mm_tm"], T["res_tn"])
    x, xb, inv_rms = ffn(x, xb, inv_rms, 0, True)

    g_kv_cols = jnp.concatenate([jnp.tile(g_k_shared, kvw // HEAD_DIM), jnp.ones((kvw,), _F32)])[None]
    kv = _proj_headnorm(xb, inv_rms, bf(w_kv[None], 0, g_kv_norm), g_kv_cols, kvw,
                        T["mm_tm"], min(T["qkv_tn"], kvw))
    q = _proj_headnorm(xb, inv_rms, bf(w_q_b, 0, g_attn_norm[1]), jnp.tile(g_q_b[0] * q_scale, n_maps)[None],
                       D, T["mm_tm"], T["qkv_tn"])
    sinks = jnp.broadcast_to((sinks_b[0].astype(_F32) * LOG2E).reshape(-1, SWA_GROUP, 1),
                             (n_maps // SWA_GROUP, SWA_GROUP, LANES))
    o = _swa_attention(q, kv, swa_bias, sinks, B, S, D, T["swa_tq"])
    x, xb, inv_rms = _proj_residual(o, bf(w_o_b, 0), x, T["mm_tm"], T["res_tn"])
    (x,) = ffn(x, xb, inv_rms, 1, False)
    return x.reshape(B, S, D)


def kernel(x, rel_bias, g_attn_norm, g_ffn_norm, w_qkv_a, w_o_a, g_q_a, g_k_a, lam_qk_a, g_sub_a,
           g_kv_norm, w_kv, g_k_shared, w_q_b, w_o_b, g_q_b, sinks_b, w_gate, w_up, w_down):
    B, S, D = x.shape
    tiles = _tiles(B * S, S, D, w_gate.shape[-1])
    return _forward(x, rel_bias, g_attn_norm, g_ffn_norm, w_qkv_a, w_o_a, g_q_a, g_k_a, lam_qk_a,
                    g_sub_a, g_kv_norm, w_kv, g_k_shared, w_q_b, w_o_b, g_q_b, sinks_b,
                    w_gate, w_up, w_down, tiles)
```

```python
import functools
import math

import jax
import jax.numpy as jnp
from jax import lax
from jax.experimental import pallas as pl
from jax.experimental.pallas import tpu as pltpu

HEAD_DIM = 128
WINDOW = 128
SWA_GROUP = 4
NUM_BUCKETS = 32
MAX_DISTANCE = 128
EPS = 1e-6
NEG = -1e30
LOG2E = math.log2(math.e)
LANES = 128
VMEM_LIMIT_BYTES = 56 * 1024 * 1024
CAST_BLOCK_BYTES = 8 * 1024 * 1024
MXU_COLS = 256

_F32 = jnp.float32
_BF16 = jnp.bfloat16
_NT = (((1,), (1,)), ((), ()))


def _params(*sem):
    return pltpu.CompilerParams(dimension_semantics=sem, vmem_limit_bytes=VMEM_LIMIT_BYTES)


def _tiles(M, S, D, D_FF):
    return dict(
        rms_tm=min(256, M),
        mm_tm=min(2048, M),
        qkv_tn=min(1024, D),
        res_tn=min(512, D),
        gu_tn=min(256, D_FF),
        down_tm=min(512, M),
        down_tk=D_FF // 2,
        down_tn=min(1024, D),
        attn_t=min(512, S // 4),
        swa_tq=min(1024, S),
    )


def _lane_tiles(a):
    return [a[:, u * LANES:(u + 1) * LANES] for u in range(a.shape[1] // LANES)]


def _rep(a, width):
    return a if width == LANES else jnp.concatenate([a] * (width // LANES), axis=1)


def _lane_rep(v):
    return jnp.broadcast_to(v.astype(_F32)[:, None], (v.shape[0], LANES))


def _cast_kernel(w_ref, *rest):
    o_ref = rest[-1]
    w = w_ref[...]
    if len(rest) == 2:
        w = w * _rep(rest[0][...], w.shape[1])
    o_ref[...] = w.astype(o_ref.dtype)


def _cast_bf16(w, layer, row_gain=None):
    _, R, C = w.shape
    tr = min(R, max(8, CAST_BLOCK_BYTES // (4 * C) // 8 * 8))
    while R % tr:
        tr -= 8
    args, specs = [w], [pl.BlockSpec((None, tr, C), lambda i: (layer, i, 0))]
    if row_gain is not None:
        args.append(_lane_rep(row_gain))
        specs.append(pl.BlockSpec((tr, LANES), lambda i: (i, 0)))
    return pl.pallas_call(
        _cast_kernel,
        grid=(R // tr,),
        in_specs=specs,
        out_specs=pl.BlockSpec((tr, C), lambda i: (i, 0)),
        out_shape=jax.ShapeDtypeStruct((R, C), _BF16),
        compiler_params=_params("arbitrary"),
        name="cast_bf16",
    )(*args)


def _prep_kernel(x_ref, xb_ref, r_ref):
    x = x_ref[...]
    xb_ref[...] = x.astype(xb_ref.dtype)
    r_ref[...] = jnp.broadcast_to(lax.rsqrt(jnp.mean(x * x, axis=-1, keepdims=True) + EPS), r_ref.shape)


def _prep(x, tm):
    M, D = x.shape
    return pl.pallas_call(
        _prep_kernel,
        grid=(M // tm,),
        in_specs=[pl.BlockSpec((tm, D), lambda i: (i, 0))],
        out_specs=[pl.BlockSpec((tm, D), lambda i: (i, 0)), pl.BlockSpec((tm, LANES), lambda i: (i, 0))],
        out_shape=[jax.ShapeDtypeStruct((M, D), _BF16), jax.ShapeDtypeStruct((M, LANES), _F32)],
        compiler_params=_params("arbitrary"),
        name="rms_stats",
    )(x)


def _accumulate_inv_rms(r_ref, ssq, first, last, width):
    @pl.when(first)
    def _():
        r_ref[...] = ssq

    @pl.when(jnp.logical_not(first))
    def _():
        r_ref[...] += ssq

    @pl.when(last)
    def _():
        total = jnp.sum(r_ref[...], axis=-1, keepdims=True)
        r_ref[...] = jnp.broadcast_to(lax.rsqrt(total / width + EPS), r_ref.shape)


def _row_tile_spec(tm, K):
    return pl.BlockSpec((tm, K), lambda i, j: (i, 0), pipeline_mode=pl.Buffered(1))


def _col_groups(width):
    sub = min(MXU_COLS, width)
    return [slice(u * sub, (u + 1) * sub) for u in range(width // sub)]


def _proj_headnorm_kernel(x_ref, r_ref, w_ref, g_ref, o_ref, *, n_norm_tiles, n_tiles):
    def sub_dots():
        for cols in _col_groups(o_ref.shape[1]):
            y = jnp.dot(x_ref[...], w_ref[:, cols], preferred_element_type=_F32)
            yield cols, _rep(r_ref[...], y.shape[1]) * y

    def normed():
        for cols, y in sub_dots():
            for c in range(y.shape[1] // HEAD_DIM):
                yc = y[:, c * HEAD_DIM:(c + 1) * HEAD_DIM]
                sl = slice(cols.start + c * HEAD_DIM, cols.start + (c + 1) * HEAD_DIM)
                r = lax.rsqrt(jnp.mean(yc * yc, axis=-1, keepdims=True) + EPS)
                o_ref[:, sl] = (yc * r * g_ref[:, sl]).astype(o_ref.dtype)

    def plain():
        for cols, y in sub_dots():
            o_ref[:, cols] = y.astype(o_ref.dtype)

    if n_norm_tiles == n_tiles:
        normed()
    else:
        j = pl.program_id(1)
        pl.when(j < n_norm_tiles)(normed)
        pl.when(j >= n_norm_tiles)(plain)


def _proj_headnorm(xb, inv_rms, w, g_cols, n_norm_cols, tm, tn):
    M, K = xb.shape
    N = w.shape[1]
    kern = functools.partial(_proj_headnorm_kernel, n_norm_tiles=n_norm_cols // tn, n_tiles=N // tn)
    return pl.pallas_call(
        kern,
        grid=(M // tm, N // tn),
        in_specs=[_row_tile_spec(tm, K),
                  pl.BlockSpec((tm, LANES), lambda i, j: (i, 0)),
                  pl.BlockSpec((K, tn), lambda i, j: (0, j)),
                  pl.BlockSpec((1, tn), lambda i, j: (0, j))],
        out_specs=pl.BlockSpec((tm, tn), lambda i, j: (i, j)),
        out_shape=jax.ShapeDtypeStruct((M, N), _BF16),
        compiler_params=_params("arbitrary", "arbitrary"),
        name="proj_headnorm",
    )(xb, inv_rms, w, g_cols)


def _store_residual_tile(cols, y, o_ref, stat_refs, ssq):
    o_ref[:, cols] = y
    if not stat_refs:
        return None
    stat_refs[0][:, cols] = y.astype(_BF16)
    part = functools.reduce(jnp.add, _lane_tiles(y * y))
    return part if ssq is None else ssq + part


def _proj_residual_kernel(x_ref, w_ref, res_ref, o_ref, ob_ref, r_ref, *, width):
    j = pl.program_id(1)
    ssq = None
    for cols in _col_groups(o_ref.shape[1]):
        y = res_ref[:, cols] + jnp.dot(x_ref[...], w_ref[:, cols], preferred_element_type=_F32)
        ssq = _store_residual_tile(cols, y, o_ref, (ob_ref,), ssq)
    _accumulate_inv_rms(r_ref, ssq, j == 0, j == pl.num_programs(1) - 1, width)


def _proj_residual(x, w, res, tm, tn):
    M, K = x.shape
    N = w.shape[1]
    tile = pl.BlockSpec((tm, tn), lambda i, j: (i, j))
    return pl.pallas_call(
        functools.partial(_proj_residual_kernel, width=N),
        grid=(M // tm, N // tn),
        in_specs=[_row_tile_spec(tm, K),
                  pl.BlockSpec((K, tn), lambda i, j: (0, j)),
                  tile],
        out_specs=[tile, tile, pl.BlockSpec((tm, LANES), lambda i, j: (i, 0))],
        out_shape=[jax.ShapeDtypeStruct((M, N), _F32), jax.ShapeDtypeStruct((M, N), _BF16),
                   jax.ShapeDtypeStruct((M, LANES), _F32)],
        compiler_params=_params("arbitrary", "arbitrary"),
        name="proj_residual",
    )(x, w, res)


def _gate_up_kernel(x_ref, r_ref, wg_ref, wu_ref, gain_ref, o_ref):
    x = x_ref[...]
    tn = o_ref.shape[1]
    gain = _rep(gain_ref[...], tn)
    r = _rep(r_ref[...], tn)
    g = r * jnp.dot(x, (wg_ref[...] * gain).astype(_BF16), preferred_element_type=_F32)
    u = r * jnp.dot(x, (wu_ref[...] * gain).astype(_BF16), preferred_element_type=_F32)
    o_ref[...] = (g * jax.nn.sigmoid(g) * u).astype(o_ref.dtype)


def _gate_up(xb, inv_rms, wg, wu, layer, gain, tm, tn):
    M, K = xb.shape
    N = wg.shape[2]
    w_spec = pl.BlockSpec((None, K, tn), lambda i, j: (layer, 0, j))
    return pl.pallas_call(
        _gate_up_kernel,
        grid=(M // tm, N // tn),
        in_specs=[_row_tile_spec(tm, K),
                  pl.BlockSpec((tm, LANES), lambda i, j: (i, 0)),
                  w_spec, w_spec,
                  pl.BlockSpec((K, LANES), lambda i, j: (0, 0))],
        out_specs=pl.BlockSpec((tm, tn), lambda i, j: (i, j)),
        out_shape=jax.ShapeDtypeStruct((M, N), _BF16),
        compiler_params=_params("arbitrary", "arbitrary"),
        name="ffn_gate_up",
    )(xb, inv_rms, wg, wu, _lane_rep(gain))


def _down_kernel(x_ref, w_ref, res_ref, o_ref, *rest, width):
    *stat_refs, acc_ref = rest
    k = pl.program_id(1)
    j = pl.program_id(2)

    def parts():
        for cols in _col_groups(o_ref.shape[1]):
            yield cols, jnp.dot(x_ref[...], w_ref[:, cols], preferred_element_type=_F32)

    @pl.when(k == 0)
    def _():
        for cols, part in parts():
            acc_ref[j, :, cols] = part

    @pl.when(k == pl.num_programs(1) - 1)
    def _():
        ssq = None
        for cols, part in parts():
            y = res_ref[:, cols] + acc_ref[j, :, cols] + part
            ssq = _store_residual_tile(cols, y, o_ref, stat_refs, ssq)
        if stat_refs:
            _accumulate_inv_rms(stat_refs[1], ssq, j == 0, j == pl.num_programs(2) - 1, width)


def _down_residual(x, w, res, with_stats, tm, tk, tn):
    M, K = x.shape
    N = w.shape[1]
    nk = K // tk
    assert nk == 2
    tile = pl.BlockSpec((tm, tn), lambda i, k, j: (i, j * k))
    out_specs, out_shape = [tile], [jax.ShapeDtypeStruct((M, N), _F32)]
    if with_stats:
        out_specs += [tile, pl.BlockSpec((tm, LANES), lambda i, k, j: (i, 0))]
        out_shape += [jax.ShapeDtypeStruct((M, N), _BF16), jax.ShapeDtypeStruct((M, LANES), _F32)]
    return pl.pallas_call(
        functools.partial(_down_kernel, width=N),
        grid=(M // tm, nk, N // tn),
        in_specs=[pl.BlockSpec((tm, tk), lambda i, k, j: (i, k)),
                  pl.BlockSpec((tk, tn), lambda i, k, j: (k, j)),
                  tile],
        out_specs=out_specs,
        out_shape=out_shape,
        scratch_shapes=[pltpu.VMEM((N // tn, tm, tn), _F32)],
        compiler_params=_params("arbitrary", "arbitrary", "arbitrary"),
        name="ffn_down_residual",
    )(x, w, res)


def _diff_attn_kernel(q_ref, k_ref, v_ref, td_ref, ts_ref, lam_ref, gsub_ref, o_ref,
                      bias_ref, sx_ref, sy_ref, sz_ref, mx_ref, my_ref, mz_ref, m_ref, l_ref, acc_ref,
                      *, t, lambda_init):
    i = pl.program_id(2)
    nt = t // LANES

    @pl.when(i == 0)
    def _build_bias():
        zeros = jnp.zeros((LANES, LANES), _F32)
        neg = jnp.full((LANES, LANES), NEG, _F32)
        for c in range(2):
            for a in range(nt):
                for b in range(nt):
                    rows = slice(a * LANES, (a + 1) * LANES)
                    cols = slice(b * LANES, (b + 1) * LANES)
                    if a == b:
                        tile = td_ref[c]
                    elif a == b + 1:
                        tile = ts_ref[c]
                    else:
                        tile = zeros if a > b else neg
                    bias_ref[c, 0, rows, cols] = tile
                    bias_ref[c, 1, rows, cols] = ts_ref[c] if (a == 0 and b == nt - 1) else zeros
                    bias_ref[c, 2, rows, cols] = neg

    def stage_a(iq, j, kind, s_out, m_out):
        q_start = pl.multiple_of(iq * t, t)
        start = pl.multiple_of(j * t, t)
        for c in range(2):
            cols = slice(c * HEAD_DIM, (c + 1) * HEAD_DIM)
            s = lax.dot_general(q_ref[pl.ds(q_start, t), cols], k_ref[pl.ds(start, t), cols], _NT,
                                preferred_element_type=_F32)
            if kind is not None:
                s = s + bias_ref[c, kind]
            s_out[c] = s
            part = functools.reduce(jnp.maximum, _lane_tiles(s))
            m_out[c] = jnp.broadcast_to(jnp.max(part, axis=-1, keepdims=True), (t, LANES))

    def stage_b(j, s_in, m_in, first=False):
        start = pl.multiple_of(j * t, t)
        vj = v_ref[pl.ds(start, t), :]
        for c in range(2):
            if first:
                m_new = m_in[c]
                psum = jnp.zeros((t, LANES), _F32)
                pv = jnp.zeros(acc_ref.shape[1:], _F32)
            else:
                m_old = m_ref[c]
                m_new = jnp.maximum(m_old, m_in[c])
                alpha = jnp.exp2(m_old - m_new)
                psum = alpha * l_ref[c]
                pv = _rep(alpha, acc_ref.shape[2]) * acc_ref[c]
            m_rep = _rep(m_new, MXU_COLS)
            for u in range(t // MXU_COLS):
                cols = slice(u * MXU_COLS, (u + 1) * MXU_COLS)
                p = jnp.exp2(s_in[c, :, cols] - m_rep)
                psum = psum + functools.reduce(jnp.add, _lane_tiles(p))
                pv = pv + jnp.dot(p.astype(_BF16), vj[cols, :], preferred_element_type=_F32)
            l_ref[c] = psum
            acc_ref[c] = pv
            m_ref[c] = m_new

    j_sub = jnp.maximum(i - 1, 0)
    n_far = jnp.maximum(i - 1, 0)
    n_pairs = n_far // 2
    i_next = jnp.minimum(i + 1, pl.num_programs(2) - 1)

    @pl.when(i == 0)
    def _():
        stage_a(0, 0, 0, sz_ref, mz_ref)

    stage_a(i, j_sub, jnp.where(i == 0, 2, 1), sy_ref, my_ref)
    stage_b(i, sz_ref, mz_ref, first=True)

    def pair_body(u, carry):
        stage_a(i, 2 * u, None, sx_ref, mx_ref)
        stage_b(jnp.where(u == 0, j_sub, 2 * u - 1), sy_ref, my_ref)
        stage_a(i, 2 * u + 1, None, sy_ref, my_ref)
        stage_b(2 * u, sx_ref, mx_ref)
        return carry

    lax.fori_loop(0, n_pairs, pair_body, 0)
    j_pending = jnp.where(n_pairs == 0, j_sub, 2 * n_pairs - 1)

    @pl.when(n_far % 2 == 1)
    def _():
        stage_a(i, n_far - 1, None, sx_ref, mx_ref)
        stage_b(j_pending, sy_ref, my_ref)
        stage_a(i_next, i_next, 0, sz_ref, mz_ref)
        stage_b(n_far - 1, sx_ref, mx_ref)

    @pl.when(n_far % 2 == 0)
    def _():
        stage_a(i_next, i_next, 0, sz_ref, mz_ref)
        stage_b(j_pending, sy_ref, my_ref)

    lf = lam_ref[...]
    lam = (jnp.exp(jnp.sum(lf[0:1] * lf[1:2], axis=-1, keepdims=True))
           - jnp.exp(jnp.sum(lf[2:3] * lf[3:4], axis=-1, keepdims=True)) + lambda_init)
    l0 = jnp.sum(l_ref[0], axis=-1, keepdims=True)
    l1 = jnp.sum(l_ref[1], axis=-1, keepdims=True)
    o = acc_ref[0] / l0 - lam * (acc_ref[1] / l1)
    o = o * lax.rsqrt(jnp.mean(o * o, axis=-1, keepdims=True) + EPS)
    o_ref[...] = (o * gsub_ref[...] * (1.0 - lambda_init)).astype(o_ref.dtype)


def _diff_attention(qkv, td, ts, lam_qk, g_sub, B, S, D, t, lambda_init):
    n_heads = D // (2 * HEAD_DIM)
    nq = S // t
    w = 2 * HEAD_DIM
    kern = functools.partial(_diff_attn_kernel, t=t, lambda_init=lambda_init)
    return pl.pallas_call(
        kern,
        grid=(B, n_heads, nq),
        in_specs=[pl.BlockSpec((S, w), lambda b, h, i: (b, h)),
                  pl.BlockSpec((S, w), lambda b, h, i: (b, n_heads + h)),
                  pl.BlockSpec((S, w), lambda b, h, i: (b, 2 * n_heads + h)),
                  pl.BlockSpec((2, LANES, LANES), lambda b, h, i: (h, 0, 0)),
                  pl.BlockSpec((2, LANES, LANES), lambda b, h, i: (h, 0, 0)),
                  pl.BlockSpec((4, HEAD_DIM), lambda b, h, i: (0, 0)),
                  pl.BlockSpec((1, w), lambda b, h, i: (0, 0))],
        out_specs=pl.BlockSpec((t, w), lambda b, h, i: (b * nq + i, h)),
        out_shape=jax.ShapeDtypeStruct((B * S, D), _BF16),
        scratch_shapes=[pltpu.VMEM((2, 3, t, t), _F32),
                        pltpu.VMEM((2, t, t), _F32),
                        pltpu.VMEM((2, t, t), _F32),
                        pltpu.VMEM((2, t, t), _F32),
                        pltpu.VMEM((2, t, LANES), _F32),
                        pltpu.VMEM((2, t, LANES), _F32),
                        pltpu.VMEM((2, t, LANES), _F32),
                        pltpu.VMEM((2, t, LANES), _F32),
                        pltpu.VMEM((2, t, LANES), _F32),
                        pltpu.VMEM((2, t, w), _F32)],
        compiler_params=_params("arbitrary", "arbitrary", "arbitrary"),
        name="diff_attention",
    )(qkv, qkv, qkv, td, ts, lam_qk, g_sub)


def _swa_kernel(q_ref, kc_ref, kp_ref, vc_ref, vp_ref, bias_ref, sink_ref, o_ref, *, tq):
    first = pl.program_id(2) == 0
    rows = SWA_GROUP * WINDOW
    bias = bias_ref[...].reshape(rows, 2 * WINDOW)
    sink = jnp.concatenate(
        [jnp.broadcast_to(sink_ref[r:r + 1, :], (WINDOW, LANES)) for r in range(SWA_GROUP)], axis=0)
    col = lax.broadcasted_iota(jnp.int32, (rows, 2 * WINDOW), 1)
    ones = jnp.ones((2 * WINDOW, LANES), _BF16)
    for c in range(tq // WINDOW):
        cur = slice(c * WINDOW, (c + 1) * WINDOW)
        if c == 0:
            k_prev, v_prev = kp_ref[...], vp_ref[...]
        else:
            prev = slice((c - 1) * WINDOW, c * WINDOW)
            k_prev, v_prev = kc_ref[prev, :], vc_ref[prev, :]
        kk = jnp.concatenate([k_prev, kc_ref[cur, :]], axis=0)
        vv = jnp.concatenate([jnp.concatenate([v_prev, vc_ref[cur, :]], axis=0), ones], axis=1)
        q4 = jnp.concatenate(
            [q_ref[cur, r * HEAD_DIM:(r + 1) * HEAD_DIM] for r in range(SWA_GROUP)], axis=0)
        s = lax.dot_general(q4, kk, _NT, preferred_element_type=_F32) + bias
        if c == 0:
            s = jnp.where(first & (col < WINDOW), NEG, s)
        part = functools.reduce(jnp.maximum, _lane_tiles(s))
        m = jnp.maximum(jnp.broadcast_to(jnp.max(part, axis=-1, keepdims=True), (rows, LANES)), sink)
        p = jnp.exp2(s - jnp.concatenate([m, m], axis=1))
        pv = jnp.dot(p.astype(_BF16), vv, preferred_element_type=_F32)
        o = pv[:, :HEAD_DIM] / (pv[:, HEAD_DIM:] + jnp.exp2(sink - m))
        for r in range(SWA_GROUP):
            o_ref[cur, r * HEAD_DIM:(r + 1) * HEAD_DIM] = (
                o[r * WINDOW:(r + 1) * WINDOW, :].astype(o_ref.dtype))


def _swa_attention(q, kv, bias, sinks, B, S, D, tq):
    n_kv = D // (SWA_GROUP * HEAD_DIM)
    nt = S // tq
    bpt = tq // WINDOW
    gw = SWA_GROUP * HEAD_DIM

    def prev_map(off):
        return lambda b, g, t: (jnp.maximum((b * nt + t) * bpt - 1, 0), off + g)

    kern = functools.partial(_swa_kernel, tq=tq)
    return pl.pallas_call(
        kern,
        grid=(B, n_kv, nt),
        in_specs=[pl.BlockSpec((tq, gw), lambda b, g, t: (b * nt + t, g)),
                  pl.BlockSpec((tq, HEAD_DIM), lambda b, g, t: (b * nt + t, g)),
                  pl.BlockSpec((WINDOW, HEAD_DIM), prev_map(0)),
                  pl.BlockSpec((tq, HEAD_DIM), lambda b, g, t: (b * nt + t, n_kv + g)),
                  pl.BlockSpec((WINDOW, HEAD_DIM), prev_map(n_kv)),
                  pl.BlockSpec((SWA_GROUP, WINDOW, 2 * WINDOW), lambda b, g, t: (g, 0, 0)),
                  pl.BlockSpec((None, SWA_GROUP, LANES), lambda b, g, t: (g, 0, 0))],
        out_specs=pl.BlockSpec((tq, gw), lambda b, g, t: (b * nt + t, g)),
        out_shape=jax.ShapeDtypeStruct((B * S, D), _BF16),
        compiler_params=_params("arbitrary", "arbitrary", "arbitrary"),
        name="swa_attention",
    )(q, kv, kv, kv, kv, bias, sinks)


def _t5_bucket(n):
    max_exact = NUM_BUCKETS // 2
    nf = jnp.maximum(n, 1).astype(_F32)
    large = max_exact + (jnp.log(nf / max_exact) / math.log(MAX_DISTANCE / max_exact)
                         * (NUM_BUCKETS - max_exact)).astype(jnp.int32)
    large = jnp.minimum(large, NUM_BUCKETS - 1)
    return jnp.where(n < max_exact, n, large)


def _bias_tiles(rel_bias):
    def lookup(table, idx):
        onehot = (idx[..., None] == jnp.arange(table.shape[0])).astype(_F32)
        return jnp.einsum("...d,dh->h...", onehot, table, precision=lax.Precision.HIGHEST)

    lut = lookup(rel_bias.astype(_F32), _t5_bucket(jnp.arange(MAX_DISTANCE))).T * LOG2E
    shifted = lut - lut[MAX_DISTANCE - 1]
    qi = jnp.arange(LANES)[:, None]
    kj = jnp.arange(LANES)[None, :]
    clip = lambda dist: jnp.clip(dist, 0, MAX_DISTANCE - 1)
    d0 = qi - kj
    td = jnp.where((d0 >= 0)[None], lookup(shifted, clip(d0)), NEG)
    ts = lookup(shifted, clip(LANES + qi - kj))
    dw = WINDOW + qi - jnp.arange(2 * WINDOW)[None, :]
    swa = jnp.where(((dw >= 0) & (dw < WINDOW))[None], lookup(lut, clip(dw)), NEG)
    return td, ts, swa


def _forward(x, rel_bias, g_attn_norm, g_ffn_norm, w_qkv_a, w_o_a, g_q_a, g_k_a, lam_qk_a, g_sub_a,
             g_kv_norm, w_kv, g_k_shared, w_q_b, w_o_b, g_q_b, sinks_b, w_gate, w_up, w_down, tiles):
    B, S, D = x.shape
    M = B * S
    T = tiles
    q_scale = HEAD_DIM ** -0.5 * LOG2E
    n_maps = D // HEAD_DIM
    kvw = w_kv.shape[1] // 2
    td, ts, swa_bias = _bias_tiles(rel_bias)
    bf = _cast_bf16
    x = x.reshape(M, D)

    def ffn(x, xb, inv_rms, layer, with_stats):
        act = _gate_up(xb, inv_rms, w_gate, w_up, layer, g_ffn_norm[layer], T["mm_tm"], T["gu_tn"])
        return _down_residual(act, bf(w_down, layer), x, with_stats, T["down_tm"], T["down_tk"], T["down_tn"])

    lambda_init = 0.8 - 0.6 * math.exp(-0.3 * 0)
    xb, inv_rms = _prep(x, T["rms_tm"])
    g_cols = jnp.concatenate([jnp.tile(g_q_a[0] * q_scale, n_maps), jnp.tile(g_k_a[0], n_maps),
                              jnp.ones((D,), _F32)])[None]
    qkv = _proj_headnorm(xb, inv_rms, bf(w_qkv_a, 0, g_attn_norm[0]), g_cols, 2 * D, T["mm_tm"], T["qkv_tn"])
    o = _diff_attention(qkv, td, ts, lam_qk_a[0], g_sub_a[0][None], B, S, D, T["attn_t"], lambda_init)
    x, xb, inv_rms = _proj_residual(o, bf(w_o_a, 0), x, T["mm_tm"], T["res_tn"])
    x, xb, inv_rms = ffn(x, xb, inv_rms, 0, True)

    g_kv_cols = jnp.concatenate([jnp.tile(g_k_shared, kvw // HEAD_DIM), jnp.ones((kvw,), _F32)])[None]
    kv = _proj_headnorm(xb, inv_rms, bf(w_kv[None], 0, g_kv_norm), g_kv_cols, kvw,
                        T["mm_tm"], min(T["qkv_tn"], kvw))
    q = _proj_headnorm(xb, inv_rms, bf(w_q_b, 0, g_attn_norm[1]), jnp.tile(g_q_b[0] * q_scale, n_maps)[None],
                       D, T["mm_tm"], T["qkv_tn"])
    sinks = jnp.broadcast_to((sinks_b[0].astype(_F32) * LOG2E).reshape(-1, SWA_GROUP, 1),
                             (n_maps // SWA_GROUP, SWA_GROUP, LANES))
    o = _swa_attention(q, kv, swa_bias, sinks, B, S, D, T["swa_tq"])
    x, xb, inv_rms = _proj_residual(o, bf(w_o_b, 0), x, T["mm_tm"], T["res_tn"])
    (x,) = ffn(x, xb, inv_rms, 1, False)
    return x.reshape(B, S, D)


def kernel(x, rel_bias, g_attn_norm, g_ffn_norm, w_qkv_a, w_o_a, g_q_a, g_k_a, lam_qk_a, g_sub_a,
           g_kv_norm, w_kv, g_k_shared, w_q_b, w_o_b, g_q_b, sinks_b, w_gate, w_up, w_down):
    B, S, D = x.shape
    tiles = _tiles(B * S, S, D, w_gate.shape[-1])
    return _forward(x, rel_bias, g_attn_norm, g_ffn_norm, w_qkv_a, w_o_a, g_q_a, g_k_a, lam_qk_a,
                    g_sub_a, g_kv_norm, w_kv, g_k_shared, w_q_b, w_o_b, g_q_b, sinks_b,
                    w_gate, w_up, w_down, tiles)
```

```python
import functools
import math

import jax
import jax.numpy as jnp
from jax import lax
from jax.experimental import pallas as pl
from jax.experimental.pallas import tpu as pltpu

HEAD_DIM = 128
WINDOW = 128
SWA_GROUP = 4
NUM_BUCKETS = 32
MAX_DISTANCE = 128
EPS = 1e-6
NEG = -1e30
LOG2E = math.log2(math.e)
LANES = 128
VMEM_LIMIT_BYTES = 56 * 1024 * 1024
CAST_BLOCK_BYTES = 8 * 1024 * 1024
MXU_COLS = 256

_F32 = jnp.float32
_BF16 = jnp.bfloat16
_NT = (((1,), (1,)), ((), ()))


def _params(*sem):
    return pltpu.CompilerParams(dimension_semantics=sem, vmem_limit_bytes=VMEM_LIMIT_BYTES)


def _tiles(M, S, D, D_FF):
    return dict(
        rms_tm=min(256, M),
        mm_tm=min(1024, M),
        qkv_tn=min(1024, D),
        res_tn=min(512, D),
        gu_tn=min(256, D_FF),
        down_tm=min(512, M),
        down_tk=D_FF // 2,
        down_tn=min(1024, D),
        attn_t=min(512, S // 4),
        swa_tq=min(1024, S),
    )


def _lane_tiles(a):
    return [a[:, u * LANES:(u + 1) * LANES] for u in range(a.shape[1] // LANES)]


def _rep(a, width):
    return a if width == LANES else jnp.concatenate([a] * (width // LANES), axis=1)


def _lane_rep(v):
    return jnp.broadcast_to(v.astype(_F32)[:, None], (v.shape[0], LANES))


def _cast_kernel(w_ref, *rest):
    o_ref = rest[-1]
    w = w_ref[...]
    if len(rest) == 2:
        w = w * _rep(rest[0][...], w.shape[1])
    o_ref[...] = w.astype(o_ref.dtype)


def _cast_bf16(w, layer, row_gain=None):
    _, R, C = w.shape
    tr = min(R, max(8, CAST_BLOCK_BYTES // (4 * C) // 8 * 8))
    while R % tr:
        tr -= 8
    args, specs = [w], [pl.BlockSpec((None, tr, C), lambda i: (layer, i, 0))]
    if row_gain is not None:
        args.append(_lane_rep(row_gain))
        specs.append(pl.BlockSpec((tr, LANES), lambda i: (i, 0)))
    return pl.pallas_call(
        _cast_kernel,
        grid=(R // tr,),
        in_specs=specs,
        out_specs=pl.BlockSpec((tr, C), lambda i: (i, 0)),
        out_shape=jax.ShapeDtypeStruct((R, C), _BF16),
        compiler_params=_params("arbitrary"),
        name="cast_bf16",
    )(*args)


def _prep_kernel(x_ref, xb_ref, r_ref):
    x = x_ref[...]
    xb_ref[...] = x.astype(xb_ref.dtype)
    r_ref[...] = jnp.broadcast_to(lax.rsqrt(jnp.mean(x * x, axis=-1, keepdims=True) + EPS), r_ref.shape)


def _prep(x, tm):
    M, D = x.shape
    return pl.pallas_call(
        _prep_kernel,
        grid=(M // tm,),
        in_specs=[pl.BlockSpec((tm, D), lambda i: (i, 0))],
        out_specs=[pl.BlockSpec((tm, D), lambda i: (i, 0)), pl.BlockSpec((tm, LANES), lambda i: (i, 0))],
        out_shape=[jax.ShapeDtypeStruct((M, D), _BF16), jax.ShapeDtypeStruct((M, LANES), _F32)],
        compiler_params=_params("arbitrary"),
        name="rms_stats",
    )(x)


def _accumulate_inv_rms(r_ref, ssq, first, last, width):
    @pl.when(first)
    def _():
        r_ref[...] = ssq

    @pl.when(jnp.logical_not(first))
    def _():
        r_ref[...] += ssq

    @pl.when(last)
    def _():
        total = jnp.sum(r_ref[...], axis=-1, keepdims=True)
        r_ref[...] = jnp.broadcast_to(lax.rsqrt(total / width + EPS), r_ref.shape)


def _row_tile_spec(tm, K):
    return pl.BlockSpec((tm, K), lambda i, j: (i, 0))


def _col_groups(width):
    sub = min(MXU_COLS, width)
    return [slice(u * sub, (u + 1) * sub) for u in range(width // sub)]


def _proj_headnorm_kernel(x_ref, r_ref, w_ref, g_ref, o_ref, *, n_norm_tiles, n_tiles):
    def sub_dots():
        for cols in _col_groups(o_ref.shape[1]):
            y = jnp.dot(x_ref[...], w_ref[:, cols], preferred_element_type=_F32)
            yield cols, _rep(r_ref[...], y.shape[1]) * y

    def normed():
        for cols, y in sub_dots():
            for c in range(y.shape[1] // HEAD_DIM):
                yc = y[:, c * HEAD_DIM:(c + 1) * HEAD_DIM]
                sl = slice(cols.start + c * HEAD_DIM, cols.start + (c + 1) * HEAD_DIM)
                r = lax.rsqrt(jnp.mean(yc * yc, axis=-1, keepdims=True) + EPS)
                o_ref[:, sl] = (yc * r * g_ref[:, sl]).astype(o_ref.dtype)

    def plain():
        for cols, y in sub_dots():
            o_ref[:, cols] = y.astype(o_ref.dtype)

    if n_norm_tiles == n_tiles:
        normed()
    else:
        j = pl.program_id(1)
        pl.when(j < n_norm_tiles)(normed)
        pl.when(j >= n_norm_tiles)(plain)


def _proj_headnorm(xb, inv_rms, w, g_cols, n_norm_cols, tm, tn):
    M, K = xb.shape
    N = w.shape[1]
    kern = functools.partial(_proj_headnorm_kernel, n_norm_tiles=n_norm_cols // tn, n_tiles=N // tn)
    return pl.pallas_call(
        kern,
        grid=(M // tm, N // tn),
        in_specs=[_row_tile_spec(tm, K),
                  pl.BlockSpec((tm, LANES), lambda i, j: (i, 0)),
                  pl.BlockSpec((K, tn), lambda i, j: (0, j)),
                  pl.BlockSpec((1, tn), lambda i, j: (0, j))],
        out_specs=pl.BlockSpec((tm, tn), lambda i, j: (i, j)),
        out_shape=jax.ShapeDtypeStruct((M, N), _BF16),
        compiler_params=_params("arbitrary", "arbitrary"),
        name="proj_headnorm",
    )(xb, inv_rms, w, g_cols)


def _store_residual_tile(cols, y, o_ref, stat_refs, ssq):
    o_ref[:, cols] = y
    if not stat_refs:
        return None
    stat_refs[0][:, cols] = y.astype(_BF16)
    part = functools.reduce(jnp.add, _lane_tiles(y * y))
    return part if ssq is None else ssq + part


def _proj_residual_kernel(x_ref, w_ref, res_ref, o_ref, ob_ref, r_ref, *, width):
    j = pl.program_id(1)
    ssq = None
    for cols in _col_groups(o_ref.shape[1]):
        y = res_ref[:, cols] + jnp.dot(x_ref[...], w_ref[:, cols], preferred_element_type=_F32)
        ssq = _store_residual_tile(cols, y, o_ref, (ob_ref,), ssq)
    _accumulate_inv_rms(r_ref, ssq, j == 0, j == pl.num_programs(1) - 1, width)


def _proj_residual(x, w, res, tm, tn):
    M, K = x.shape
    N = w.shape[1]
    tile = pl.BlockSpec((tm, tn), lambda i, j: (i, j))
    return pl.pallas_call(
        functools.partial(_proj_residual_kernel, width=N),
        grid=(M // tm, N // tn),
        in_specs=[_row_tile_spec(tm, K),
                  pl.BlockSpec((K, tn), lambda i, j: (0, j)),
                  tile],
        out_specs=[tile, tile, pl.BlockSpec((tm, LANES), lambda i, j: (i, 0))],
        out_shape=[jax.ShapeDtypeStruct((M, N), _F32), jax.ShapeDtypeStruct((M, N), _BF16),
                   jax.ShapeDtypeStruct((M, LANES), _F32)],
        compiler_params=_params("arbitrary", "arbitrary"),
        name="proj_residual",
    )(x, w, res)


def _gate_up_kernel(x_ref, r_ref, wg_ref, wu_ref, gain_ref, o_ref):
    x = x_ref[...]
    tn = o_ref.shape[1]
    gain = _rep(gain_ref[...], tn)
    r = _rep(r_ref[...], tn)
    g = r * jnp.dot(x, (wg_ref[...] * gain).astype(_BF16), preferred_element_type=_F32)
    u = r * jnp.dot(x, (wu_ref[...] * gain).astype(_BF16), preferred_element_type=_F32)
    o_ref[...] = (g * jax.nn.sigmoid(g) * u).astype(o_ref.dtype)


def _gate_up(xb, inv_rms, wg, wu, layer, gain, tm, tn):
    M, K = xb.shape
    N = wg.shape[2]
    w_spec = pl.BlockSpec((None, K, tn), lambda i, j: (layer, 0, j))
    return pl.pallas_call(
        _gate_up_kernel,
        grid=(M // tm, N // tn),
        in_specs=[_row_tile_spec(tm, K),
                  pl.BlockSpec((tm, LANES), lambda i, j: (i, 0)),
                  w_spec, w_spec,
                  pl.BlockSpec((K, LANES), lambda i, j: (0, 0))],
        out_specs=pl.BlockSpec((tm, tn), lambda i, j: (i, j)),
        out_shape=jax.ShapeDtypeStruct((M, N), _BF16),
        compiler_params=_params("arbitrary", "arbitrary"),
        name="ffn_gate_up",
    )(xb, inv_rms, wg, wu, _lane_rep(gain))


def _down_kernel(x_ref, w_ref, res_ref, o_ref, *rest, width):
    *stat_refs, acc_ref = rest
    k = pl.program_id(1)
    j = pl.program_id(2)

    def parts():
        for cols in _col_groups(o_ref.shape[1]):
            yield cols, jnp.dot(x_ref[...], w_ref[:, cols], preferred_element_type=_F32)

    @pl.when(k == 0)
    def _():
        for cols, part in parts():
            acc_ref[j, :, cols] = part

    @pl.when(k == pl.num_programs(1) - 1)
    def _():
        ssq = None
        for cols, part in parts():
            y = res_ref[:, cols] + acc_ref[j, :, cols] + part
            ssq = _store_residual_tile(cols, y, o_ref, stat_refs, ssq)
        if stat_refs:
            _accumulate_inv_rms(stat_refs[1], ssq, j == 0, j == pl.num_programs(2) - 1, width)


def _down_residual(x, w, res, with_stats, tm, tk, tn):
    M, K = x.shape
    N = w.shape[1]
    nk = K // tk
    assert nk == 2
    tile = pl.BlockSpec((tm, tn), lambda i, k, j: (i, j * k))
    out_specs, out_shape = [tile], [jax.ShapeDtypeStruct((M, N), _F32)]
    if with_stats:
        out_specs += [tile, pl.BlockSpec((tm, LANES), lambda i, k, j: (i, 0))]
        out_shape += [jax.ShapeDtypeStruct((M, N), _BF16), jax.ShapeDtypeStruct((M, LANES), _F32)]
    return pl.pallas_call(
        functools.partial(_down_kernel, width=N),
        grid=(M // tm, nk, N // tn),
        in_specs=[pl.BlockSpec((tm, tk), lambda i, k, j: (i, k)),
                  pl.BlockSpec((tk, tn), lambda i, k, j: (k, j)),
                  tile],
        out_specs=out_specs,
        out_shape=out_shape,
        scratch_shapes=[pltpu.VMEM((N // tn, tm, tn), _F32)],
        compiler_params=_params("arbitrary", "arbitrary", "arbitrary"),
        name="ffn_down_residual",
    )(x, w, res)


def _diff_attn_kernel(q_ref, k_ref, v_ref, td_ref, ts_ref, lam_ref, gsub_ref, o_ref,
                      bias_ref, sx_ref, sy_ref, sz_ref, mx_ref, my_ref, mz_ref, m_ref, l_ref, acc_ref,
                      *, t, lambda_init):
    i = pl.program_id(2)
    nt = t // LANES

    @pl.when(i == 0)
    def _build_bias():
        zeros = jnp.zeros((LANES, LANES), _F32)
        neg = jnp.full((LANES, LANES), NEG, _F32)
        for c in range(2):
            for a in range(nt):
                for b in range(nt):
                    rows = slice(a * LANES, (a + 1) * LANES)
                    cols = slice(b * LANES, (b + 1) * LANES)
                    if a == b:
                        tile = td_ref[c]
                    elif a == b + 1:
                        tile = ts_ref[c]
                    else:
                        tile = zeros if a > b else neg
                    bias_ref[c, 0, rows, cols] = tile
                    bias_ref[c, 1, rows, cols] = ts_ref[c] if (a == 0 and b == nt - 1) else zeros
                    bias_ref[c, 2, rows, cols] = neg

    def stage_a(blocks, s_out, m_out):
        for c in range(2):
            cols = slice(c * HEAD_DIM, (c + 1) * HEAD_DIM)
            part = None
            for n, (j, kind) in enumerate(blocks):
                start = pl.multiple_of(j * t, t)
                s = lax.dot_general(q_ref[:, cols], k_ref[pl.ds(start, t), cols], _NT,
                                    preferred_element_type=_F32)
                if kind is not None:
                    s = s + bias_ref[c, kind]
                s_out[c, :, n * t:(n + 1) * t] = s
                block_max = functools.reduce(jnp.maximum, _lane_tiles(s))
                part = block_max if part is None else jnp.maximum(part, block_max)
            m_out[c] = jnp.broadcast_to(jnp.max(part, axis=-1, keepdims=True), (t, LANES))

    def stage_b(blocks, s_in, m_in, first=False):
        for c in range(2):
            if first:
                m_new = m_in[c]
                psum = jnp.zeros((t, LANES), _F32)
                pv = jnp.zeros(acc_ref.shape[1:], _F32)
            else:
                m_old = m_ref[c]
                m_new = jnp.maximum(m_old, m_in[c])
                alpha = jnp.exp2(m_old - m_new)
                psum = alpha * l_ref[c]
                pv = _rep(alpha, acc_ref.shape[2]) * acc_ref[c]
            m_rep = _rep(m_new, MXU_COLS)
            for n, (j, _) in enumerate(blocks):
                for u in range(t // MXU_COLS):
                    off = u * MXU_COLS
                    p = jnp.exp2(s_in[c, :, n * t + off:n * t + off + MXU_COLS] - m_rep)
                    psum = psum + functools.reduce(jnp.add, _lane_tiles(p))
                    v_rows = v_ref[pl.ds(pl.multiple_of(j * t + off, MXU_COLS), MXU_COLS), :]
                    pv = pv + jnp.dot(p.astype(_BF16), v_rows, preferred_element_type=_F32)
            l_ref[c] = psum
            acc_ref[c] = pv
            m_ref[c] = m_new

    j_sub = jnp.maximum(i - 1, 0)
    n_far = jnp.maximum(i - 1, 0) // 2
    three = jnp.logical_and(i % 2 == 0, i > 0)
    near = [(i, 0), (j_sub, jnp.where(i == 0, 2, 1)), (jnp.maximum(i - 2, 0), None)]

    def far(f):
        return [(2 * f, None), (2 * f + 1, None)]

    for width, is_width in ((2, jnp.logical_not(three)), (3, three)):
        @pl.when(is_width)
        def _(width=width):
            stage_a(near[:width], sz_ref, mz_ref)

        @pl.when(jnp.logical_and(is_width, n_far == 0))
        def _(width=width):
            stage_b(near[:width], sz_ref, mz_ref, first=True)

        @pl.when(jnp.logical_and(is_width, n_far > 0))
        def _(width=width):
            stage_a(far(0), sy_ref, my_ref)
            stage_b(near[:width], sz_ref, mz_ref, first=True)

    n_rest = jnp.maximum(n_far - 1, 0)
    n_pairs = n_rest // 2

    def pair_body(u, carry):
        stage_a(far(2 * u + 1), sx_ref, mx_ref)
        stage_b(far(2 * u), sy_ref, my_ref)
        stage_a(far(2 * u + 2), sy_ref, my_ref)
        stage_b(far(2 * u + 1), sx_ref, mx_ref)
        return carry

    lax.fori_loop(0, n_pairs, pair_body, 0)

    @pl.when(jnp.logical_and(n_far > 0, n_rest % 2 == 1))
    def _():
        stage_a(far(n_far - 1), sx_ref, mx_ref)
        stage_b(far(2 * n_pairs), sy_ref, my_ref)
        stage_b(far(n_far - 1), sx_ref, mx_ref)

    @pl.when(jnp.logical_and(n_far > 0, n_rest % 2 == 0))
    def _():
        stage_b(far(2 * n_pairs), sy_ref, my_ref)

    lf = lam_ref[...]
    lam = (jnp.exp(jnp.sum(lf[0:1] * lf[1:2], axis=-1, keepdims=True))
           - jnp.exp(jnp.sum(lf[2:3] * lf[3:4], axis=-1, keepdims=True)) + lambda_init)
    l0 = jnp.sum(l_ref[0], axis=-1, keepdims=True)
    l1 = jnp.sum(l_ref[1], axis=-1, keepdims=True)
    o = acc_ref[0] / l0 - lam * (acc_ref[1] / l1)
    o = o * lax.rsqrt(jnp.mean(o * o, axis=-1, keepdims=True) + EPS)
    o_ref[...] = (o * gsub_ref[...] * (1.0 - lambda_init)).astype(o_ref.dtype)


def _diff_attention(qkv, td, ts, lam_qk, g_sub, B, S, D, t, lambda_init):
    n_heads = D // (2 * HEAD_DIM)
    nq = S // t
    w = 2 * HEAD_DIM
    kern = functools.partial(_diff_attn_kernel, t=t, lambda_init=lambda_init)
    return pl.pallas_call(
        kern,
        grid=(B, n_heads, nq),
        in_specs=[pl.BlockSpec((t, w), lambda b, h, i: (b * nq + i, h)),
                  pl.BlockSpec((S, w), lambda b, h, i: (b, n_heads + h)),
                  pl.BlockSpec((S, w), lambda b, h, i: (b, 2 * n_heads + h)),
                  pl.BlockSpec((2, LANES, LANES), lambda b, h, i: (h, 0, 0)),
                  pl.BlockSpec((2, LANES, LANES), lambda b, h, i: (h, 0, 0)),
                  pl.BlockSpec((4, HEAD_DIM), lambda b, h, i: (0, 0)),
                  pl.BlockSpec((1, w), lambda b, h, i: (0, 0))],
        out_specs=pl.BlockSpec((t, w), lambda b, h, i: (b * nq + i, h)),
        out_shape=jax.ShapeDtypeStruct((B * S, D), _BF16),
        scratch_shapes=[pltpu.VMEM((2, 3, t, t), _F32),
                        pltpu.VMEM((2, t, 2 * t), _F32),
                        pltpu.VMEM((2, t, 2 * t), _F32),
                        pltpu.VMEM((2, t, 3 * t), _F32),
                        pltpu.VMEM((2, t, LANES), _F32),
                        pltpu.VMEM((2, t, LANES), _F32),
                        pltpu.VMEM((2, t, LANES), _F32),
                        pltpu.VMEM((2, t, LANES), _F32),
                        pltpu.VMEM((2, t, LANES), _F32),
                        pltpu.VMEM((2, t, w), _F32)],
        compiler_params=_params("arbitrary", "arbitrary", "arbitrary"),
        name="diff_attention",
    )(qkv, qkv, qkv, td, ts, lam_qk, g_sub)


def _swa_kernel(q_ref, kc_ref, kp_ref, vc_ref, vp_ref, bias_ref, sink_ref, o_ref, *, tq):
    first = pl.program_id(2) == 0
    rows = SWA_GROUP * WINDOW
    bias = bias_ref[...].reshape(rows, 2 * WINDOW)
    sink = jnp.concatenate(
        [jnp.broadcast_to(sink_ref[r:r + 1, :], (WINDOW, LANES)) for r in range(SWA_GROUP)], axis=0)
    col = lax.broadcasted_iota(jnp.int32, (rows, 2 * WINDOW), 1)
    ones = jnp.ones((2 * WINDOW, LANES), _BF16)
    for c in range(tq // WINDOW):
        cur = slice(c * WINDOW, (c + 1) * WINDOW)
        if c == 0:
            k_prev, v_prev = kp_ref[...], vp_ref[...]
        else:
            prev = slice((c - 1) * WINDOW, c * WINDOW)
            k_prev, v_prev = kc_ref[prev, :], vc_ref[prev, :]
        kk = jnp.concatenate([k_prev, kc_ref[cur, :]], axis=0)
        vv = jnp.concatenate([jnp.concatenate([v_prev, vc_ref[cur, :]], axis=0), ones], axis=1)
        q4 = jnp.concatenate(
            [q_ref[cur, r * HEAD_DIM:(r + 1) * HEAD_DIM] for r in range(SWA_GROUP)], axis=0)
        s = lax.dot_general(q4, kk, _NT, preferred_element_type=_F32) + bias
        if c == 0:
            s = jnp.where(first & (col < WINDOW), NEG, s)
        part = functools.reduce(jnp.maximum, _lane_tiles(s))
        m = jnp.maximum(jnp.broadcast_to(jnp.max(part, axis=-1, keepdims=True), (rows, LANES)), sink)
        p = jnp.exp2(s - jnp.concatenate([m, m], axis=1))
        pv = jnp.dot(p.astype(_BF16), vv, preferred_element_type=_F32)
        o = pv[:, :HEAD_DIM] / (pv[:, HEAD_DIM:] + jnp.exp2(sink - m))
        for r in range(SWA_GROUP):
            o_ref[cur, r * HEAD_DIM:(r + 1) * HEAD_DIM] = (
                o[r * WINDOW:(r + 1) * WINDOW, :].astype(o_ref.dtype))


def _swa_attention(q, kv, bias, sinks, B, S, D, tq):
    n_kv = D // (SWA_GROUP * HEAD_DIM)
    nt = S // tq
    bpt = tq // WINDOW
    gw = SWA_GROUP * HEAD_DIM

    def prev_map(off):
        return lambda b, g, t: (jnp.maximum((b * nt + t) * bpt - 1, 0), off + g)

    kern = functools.partial(_swa_kernel, tq=tq)
    return pl.pallas_call(
        kern,
        grid=(B, n_kv, nt),
        in_specs=[pl.BlockSpec((tq, gw), lambda b, g, t: (b * nt + t, g)),
                  pl.BlockSpec((tq, HEAD_DIM), lambda b, g, t: (b * nt + t, g)),
                  pl.BlockSpec((WINDOW, HEAD_DIM), prev_map(0)),
                  pl.BlockSpec((tq, HEAD_DIM), lambda b, g, t: (b * nt + t, n_kv + g)),
                  pl.BlockSpec((WINDOW, HEAD_DIM), prev_map(n_kv)),
                  pl.BlockSpec((SWA_GROUP, WINDOW, 2 * WINDOW), lambda b, g, t: (g, 0, 0)),
                  pl.BlockSpec((None, SWA_GROUP, LANES), lambda b, g, t: (g, 0, 0))],
        out_specs=pl.BlockSpec((tq, gw), lambda b, g, t: (b * nt + t, g)),
        out_shape=jax.ShapeDtypeStruct((B * S, D), _BF16),
        compiler_params=_params("arbitrary", "arbitrary", "arbitrary"),
        name="swa_attention",
    )(q, kv, kv, kv, kv, bias, sinks)


def _t5_bucket(n):
    max_exact = NUM_BUCKETS // 2
    nf = jnp.maximum(n, 1).astype(_F32)
    large = max_exact + (jnp.log(nf / max_exact) / math.log(MAX_DISTANCE / max_exact)
                         * (NUM_BUCKETS - max_exact)).astype(jnp.int32)
    large = jnp.minimum(large, NUM_BUCKETS - 1)
    return jnp.where(n < max_exact, n, large)


def _bias_tiles(rel_bias):
    def lookup(table, idx):
        onehot = (idx[..., None] == jnp.arange(table.shape[0])).astype(_F32)
        return jnp.einsum("...d,dh->h...", onehot, table, precision=lax.Precision.HIGHEST)

    lut = lookup(rel_bias.astype(_F32), _t5_bucket(jnp.arange(MAX_DISTANCE))).T * LOG2E
    shifted = lut - lut[MAX_DISTANCE - 1]
    qi = jnp.arange(LANES)[:, None]
    kj = jnp.arange(LANES)[None, :]
    clip = lambda dist: jnp.clip(dist, 0, MAX_DISTANCE - 1)
    d0 = qi - kj
    td = jnp.where((d0 >= 0)[None], lookup(shifted, clip(d0)), NEG)
    ts = lookup(shifted, clip(LANES + qi - kj))
    dw = WINDOW + qi - jnp.arange(2 * WINDOW)[None, :]
    swa = jnp.where(((dw >= 0) & (dw < WINDOW))[None], lookup(lut, clip(dw)), NEG)
    return td, ts, swa


def _forward(x, rel_bias, g_attn_norm, g_ffn_norm, w_qkv_a, w_o_a, g_q_a, g_k_a, lam_qk_a, g_sub_a,
             g_kv_norm, w_kv, g_k_shared, w_q_b, w_o_b, g_q_b, sinks_b, w_gate, w_up, w_down, tiles):
    B, S, D = x.shape
    M = B * S
    T = tiles
    q_scale = HEAD_DIM ** -0.5 * LOG2E
    n_maps = D // HEAD_DIM
    kvw = w_kv.shape[1] // 2
    td, ts, swa_bias = _bias_tiles(rel_bias)
    bf = _cast_bf16
    x = x.reshape(M, D)

    def ffn(x, xb, inv_rms, layer, with_stats):
        act = _gate_up(xb, inv_rms, w_gate, w_up, layer, g_ffn_norm[layer], T["mm_tm"], T["gu_tn"])
        return _down_residual(act, bf(w_down, layer), x, with_stats, T["down_tm"], T["down_tk"], T["down_tn"])

    lambda_init = 0.8 - 0.6 * math.exp(-0.3 * 0)
    xb, inv_rms = _prep(x, T["rms_tm"])
    g_cols = jnp.concatenate([jnp.tile(g_q_a[0] * q_scale, n_maps), jnp.tile(g_k_a[0], n_maps),
                              jnp.ones((D,), _F32)])[None]
    qkv = _proj_headnorm(xb, inv_rms, bf(w_qkv_a, 0, g_attn_norm[0]), g_cols, 2 * D, T["mm_tm"], T["qkv_tn"])
    o = _diff_attention(qkv, td, ts, lam_qk_a[0], g_sub_a[0][None], B, S, D, T["attn_t"], lambda_init)
    x, xb, inv_rms = _proj_residual(o, bf(w_o_a, 0), x, T["mm_tm"], T["res_tn"])
    x, xb, inv_rms = ffn(x, xb, inv_rms, 0, True)

    g_kv_cols = jnp.concatenate([jnp.tile(g_k_shared, kvw // HEAD_DIM), jnp.ones((kvw,), _F32)])[None]
    kv = _proj_headnorm(xb, inv_rms, bf(w_kv[None], 0, g_kv_norm), g_kv_cols, kvw,
                        T["mm_tm"], min(T["qkv_tn"], kvw))
    q = _proj_headnorm(xb, inv_rms, bf(w_q_b, 0, g_attn_norm[1]), jnp.tile(g_q_b[0] * q_scale, n_maps)[None],
                       D, T["mm_tm"], T["qkv_tn"])
    sinks = jnp.broadcast_to((sinks_b[0].astype(_F32) * LOG2E).reshape(-1, SWA_GROUP, 1),
                             (n_maps // SWA_GROUP, SWA_GROUP, LANES))
    o = _swa_attention(q, kv, swa_bias, sinks, B, S, D, T["swa_tq"])
    x, xb, inv_rms = _proj_residual(o, bf(w_o_b, 0), x, T["mm_tm"], T["res_tn"])
    (x,) = ffn(x, xb, inv_rms, 1, False)
    return x.reshape(B, S, D)


def kernel(x, rel_bias, g_attn_norm, g_ffn_norm, w_qkv_a, w_o_a, g_q_a, g_k_a, lam_qk_a, g_sub_a,
           g_kv_norm, w_kv, g_k_shared, w_q_b, w_o_b, g_q_b, sinks_b, w_gate, w_up, w_down):
    B, S, D = x.shape
    tiles = _tiles(B * S, S, D, w_gate.shape[-1])
    return _forward(x, rel_bias, g_attn_norm, g_ffn_norm, w_qkv_a, w_o_a, g_q_a, g_k_a, lam_qk_a,
                    g_sub_a, g_kv_norm, w_kv, g_k_shared, w_q_b, w_o_b, g_q_b, sinks_b,
                    w_gate, w_up, w_down, tiles)
```

```python
import functools
import math

import jax
import jax.numpy as jnp
from jax import lax
from jax.experimental import pallas as pl
from jax.experimental.pallas import tpu as pltpu

HEAD_DIM = 128
WINDOW = 128
SWA_GROUP = 4
NUM_BUCKETS = 32
MAX_DISTANCE = 128
EPS = 1e-6
NEG = -1e30
LOG2E = math.log2(math.e)
LANES = 128
VMEM_LIMIT_BYTES = 56 * 1024 * 1024
CAST_BLOCK_BYTES = 8 * 1024 * 1024
MXU_COLS = 256

_F32 = jnp.float32
_BF16 = jnp.bfloat16
_NT = (((1,), (1,)), ((), ()))


def _params(*sem):
    return pltpu.CompilerParams(dimension_semantics=sem, vmem_limit_bytes=VMEM_LIMIT_BYTES)


def _tiles(M, S, D, D_FF):
    return dict(
        rms_tm=min(256, M),
        mm_tm=min(1024, M),
        qkv_tn=min(1024, D),
        res_tn=min(512, D),
        gu_tn=min(256, D_FF),
        down_tm=min(512, M),
        down_tk=D_FF // 2,
        down_tn=min(1024, D),
        attn_t=min(512, S // 4),
        attn_stage_blocks=3,
        swa_tq=min(1024, S),
    )


def _lane_tiles(a):
    return [a[:, u * LANES:(u + 1) * LANES] for u in range(a.shape[1] // LANES)]


def _rep(a, width):
    return a if width == LANES else jnp.concatenate([a] * (width // LANES), axis=1)


def _lane_rep(v):
    return jnp.broadcast_to(v.astype(_F32)[:, None], (v.shape[0], LANES))


def _cast_kernel(w_ref, *rest):
    o_ref = rest[-1]
    w = w_ref[...]
    if len(rest) == 2:
        w = w * _rep(rest[0][...], w.shape[1])
    o_ref[...] = w.astype(o_ref.dtype)


def _cast_bf16(w, layer, row_gain=None):
    _, R, C = w.shape
    tr = min(R, max(8, CAST_BLOCK_BYTES // (4 * C) // 8 * 8))
    while R % tr:
        tr -= 8
    args, specs = [w], [pl.BlockSpec((None, tr, C), lambda i: (layer, i, 0))]
    if row_gain is not None:
        args.append(_lane_rep(row_gain))
        specs.append(pl.BlockSpec((tr, LANES), lambda i: (i, 0)))
    return pl.pallas_call(
        _cast_kernel,
        grid=(R // tr,),
        in_specs=specs,
        out_specs=pl.BlockSpec((tr, C), lambda i: (i, 0)),
        out_shape=jax.ShapeDtypeStruct((R, C), _BF16),
        compiler_params=_params("arbitrary"),
        name="cast_bf16",
    )(*args)


def _prep_kernel(x_ref, xb_ref, r_ref):
    x = x_ref[...]
    xb_ref[...] = x.astype(xb_ref.dtype)
    r_ref[...] = jnp.broadcast_to(lax.rsqrt(jnp.mean(x * x, axis=-1, keepdims=True) + EPS), r_ref.shape)


def _prep(x, tm):
    M, D = x.shape
    return pl.pallas_call(
        _prep_kernel,
        grid=(M // tm,),
        in_specs=[pl.BlockSpec((tm, D), lambda i: (i, 0))],
        out_specs=[pl.BlockSpec((tm, D), lambda i: (i, 0)), pl.BlockSpec((tm, LANES), lambda i: (i, 0))],
        out_shape=[jax.ShapeDtypeStruct((M, D), _BF16), jax.ShapeDtypeStruct((M, LANES), _F32)],
        compiler_params=_params("arbitrary"),
        name="rms_stats",
    )(x)


def _accumulate_inv_rms(r_ref, ssq, first, last, width):
    @pl.when(first)
    def _():
        r_ref[...] = ssq

    @pl.when(jnp.logical_not(first))
    def _():
        r_ref[...] += ssq

    @pl.when(last)
    def _():
        total = jnp.sum(r_ref[...], axis=-1, keepdims=True)
        r_ref[...] = jnp.broadcast_to(lax.rsqrt(total / width + EPS), r_ref.shape)


def _row_tile_spec(tm, K):
    return pl.BlockSpec((tm, K), lambda i, j: (i, 0))


def _col_groups(width):
    sub = min(MXU_COLS, width)
    return [slice(u * sub, (u + 1) * sub) for u in range(width // sub)]


def _proj_headnorm_kernel(x_ref, r_ref, w_ref, g_ref, o_ref, *, n_norm_tiles, n_tiles):
    def sub_dots():
        for cols in _col_groups(o_ref.shape[1]):
            y = jnp.dot(x_ref[...], w_ref[:, cols], preferred_element_type=_F32)
            yield cols, _rep(r_ref[...], y.shape[1]) * y

    def normed():
        for cols, y in sub_dots():
            for c in range(y.shape[1] // HEAD_DIM):
                yc = y[:, c * HEAD_DIM:(c + 1) * HEAD_DIM]
                sl = slice(cols.start + c * HEAD_DIM, cols.start + (c + 1) * HEAD_DIM)
                r = lax.rsqrt(jnp.mean(yc * yc, axis=-1, keepdims=True) + EPS)
                o_ref[:, sl] = (yc * r * g_ref[:, sl]).astype(o_ref.dtype)

    def plain():
        for cols, y in sub_dots():
            o_ref[:, cols] = y.astype(o_ref.dtype)

    if n_norm_tiles == n_tiles:
        normed()
    else:
        j = pl.program_id(1)
        pl.when(j < n_norm_tiles)(normed)
        pl.when(j >= n_norm_tiles)(plain)


def _proj_headnorm(xb, inv_rms, w, g_cols, n_norm_cols, tm, tn):
    M, K = xb.shape
    N = w.shape[1]
    kern = functools.partial(_proj_headnorm_kernel, n_norm_tiles=n_norm_cols // tn, n_tiles=N // tn)
    return pl.pallas_call(
        kern,
        grid=(M // tm, N // tn),
        in_specs=[_row_tile_spec(tm, K),
                  pl.BlockSpec((tm, LANES), lambda i, j: (i, 0)),
                  pl.BlockSpec((K, tn), lambda i, j: (0, j)),
                  pl.BlockSpec((1, tn), lambda i, j: (0, j))],
        out_specs=pl.BlockSpec((tm, tn), lambda i, j: (i, j)),
        out_shape=jax.ShapeDtypeStruct((M, N), _BF16),
        compiler_params=_params("arbitrary", "arbitrary"),
        name="proj_headnorm",
    )(xb, inv_rms, w, g_cols)


def _store_residual_tile(cols, y, o_ref, stat_refs, ssq):
    o_ref[:, cols] = y
    if not stat_refs:
        return None
    stat_refs[0][:, cols] = y.astype(_BF16)
    part = functools.reduce(jnp.add, _lane_tiles(y * y))
    return part if ssq is None else ssq + part


def _proj_residual_kernel(x_ref, w_ref, res_ref, o_ref, ob_ref, r_ref, *, width):
    j = pl.program_id(1)
    ssq = None
    for cols in _col_groups(o_ref.shape[1]):
        y = res_ref[:, cols] + jnp.dot(x_ref[...], w_ref[:, cols], preferred_element_type=_F32)
        ssq = _store_residual_tile(cols, y, o_ref, (ob_ref,), ssq)
    _accumulate_inv_rms(r_ref, ssq, j == 0, j == pl.num_programs(1) - 1, width)


def _proj_residual(x, w, res, tm, tn):
    M, K = x.shape
    N = w.shape[1]
    tile = pl.BlockSpec((tm, tn), lambda i, j: (i, j))
    return pl.pallas_call(
        functools.partial(_proj_residual_kernel, width=N),
        grid=(M // tm, N // tn),
        in_specs=[_row_tile_spec(tm, K),
                  pl.BlockSpec((K, tn), lambda i, j: (0, j)),
                  tile],
        out_specs=[tile, tile, pl.BlockSpec((tm, LANES), lambda i, j: (i, 0))],
        out_shape=[jax.ShapeDtypeStruct((M, N), _F32), jax.ShapeDtypeStruct((M, N), _BF16),
                   jax.ShapeDtypeStruct((M, LANES), _F32)],
        compiler_params=_params("arbitrary", "arbitrary"),
        name="proj_residual",
    )(x, w, res)


def _gate_up_kernel(x_ref, r_ref, wg_ref, wu_ref, gain_ref, o_ref):
    x = x_ref[...]
    tn = o_ref.shape[1]
    gain = _rep(gain_ref[...], tn)
    r = _rep(r_ref[...], tn)
    g = r * jnp.dot(x, (wg_ref[...] * gain).astype(_BF16), preferred_element_type=_F32)
    u = r * jnp.dot(x, (wu_ref[...] * gain).astype(_BF16), preferred_element_type=_F32)
    o_ref[...] = (g * jax.nn.sigmoid(g) * u).astype(o_ref.dtype)


def _gate_up(xb, inv_rms, wg, wu, layer, gain, tm, tn):
    M, K = xb.shape
    N = wg.shape[2]
    w_spec = pl.BlockSpec((None, K, tn), lambda i, j: (layer, 0, j))
    return pl.pallas_call(
        _gate_up_kernel,
        grid=(M // tm, N // tn),
        in_specs=[_row_tile_spec(tm, K),
                  pl.BlockSpec((tm, LANES), lambda i, j: (i, 0)),
                  w_spec, w_spec,
                  pl.BlockSpec((K, LANES), lambda i, j: (0, 0))],
        out_specs=pl.BlockSpec((tm, tn), lambda i, j: (i, j)),
        out_shape=jax.ShapeDtypeStruct((M, N), _BF16),
        compiler_params=_params("arbitrary", "arbitrary"),
        name="ffn_gate_up",
    )(xb, inv_rms, wg, wu, _lane_rep(gain))


def _down_kernel(x_ref, w_ref, res_ref, o_ref, *rest, width):
    *stat_refs, acc_ref = rest
    k = pl.program_id(1)
    j = pl.program_id(2)

    def parts():
        for cols in _col_groups(o_ref.shape[1]):
            yield cols, jnp.dot(x_ref[...], w_ref[:, cols], preferred_element_type=_F32)

    @pl.when(k == 0)
    def _():
        for cols, part in parts():
            acc_ref[j, :, cols] = part

    @pl.when(k == pl.num_programs(1) - 1)
    def _():
        ssq = None
        for cols, part in parts():
            y = res_ref[:, cols] + acc_ref[j, :, cols] + part
            ssq = _store_residual_tile(cols, y, o_ref, stat_refs, ssq)
        if stat_refs:
            _accumulate_inv_rms(stat_refs[1], ssq, j == 0, j == pl.num_programs(2) - 1, width)


def _down_residual(x, w, res, with_stats, tm, tk, tn):
    M, K = x.shape
    N = w.shape[1]
    nk = K // tk
    assert nk == 2
    tile = pl.BlockSpec((tm, tn), lambda i, k, j: (i, j * k))
    out_specs, out_shape = [tile], [jax.ShapeDtypeStruct((M, N), _F32)]
    if with_stats:
        out_specs += [tile, pl.BlockSpec((tm, LANES), lambda i, k, j: (i, 0))]
        out_shape += [jax.ShapeDtypeStruct((M, N), _BF16), jax.ShapeDtypeStruct((M, LANES), _F32)]
    return pl.pallas_call(
        functools.partial(_down_kernel, width=N),
        grid=(M // tm, nk, N // tn),
        in_specs=[pl.BlockSpec((tm, tk), lambda i, k, j: (i, k)),
                  pl.BlockSpec((tk, tn), lambda i, k, j: (k, j)),
                  tile],
        out_specs=out_specs,
        out_shape=out_shape,
        scratch_shapes=[pltpu.VMEM((N // tn, tm, tn), _F32)],
        compiler_params=_params("arbitrary", "arbitrary", "arbitrary"),
        name="ffn_down_residual",
    )(x, w, res)


def _diff_attn_kernel(q_ref, k_ref, v_ref, td_ref, ts_ref, lam_ref, gsub_ref, o_ref,
                      bias_ref, sx_ref, sy_ref, sz_ref, mx_ref, my_ref, mz_ref, m_ref, l_ref, acc_ref,
                      *, t, stage_blocks, lambda_init):
    i = pl.program_id(2)
    nt = t // LANES

    @pl.when(i == 0)
    def _build_bias():
        zeros = jnp.zeros((LANES, LANES), _F32)
        neg = jnp.full((LANES, LANES), NEG, _F32)
        for c in range(2):
            for a in range(nt):
                for b in range(nt):
                    rows = slice(a * LANES, (a + 1) * LANES)
                    cols = slice(b * LANES, (b + 1) * LANES)
                    if a == b:
                        tile = td_ref[c]
                    elif a == b + 1:
                        tile = ts_ref[c]
                    else:
                        tile = zeros if a > b else neg
                    bias_ref[c, 0, rows, cols] = tile
                    bias_ref[c, 1, rows, cols] = ts_ref[c] if (a == 0 and b == nt - 1) else zeros
                    bias_ref[c, 2, rows, cols] = neg

    def stage_a(blocks, s_out, m_out):
        for c in range(2):
            cols = slice(c * HEAD_DIM, (c + 1) * HEAD_DIM)
            part = None
            for n, (j, kind) in enumerate(blocks):
                start = pl.multiple_of(j * t, t)
                s = lax.dot_general(q_ref[:, cols], k_ref[pl.ds(start, t), cols], _NT,
                                    preferred_element_type=_F32)
                if kind is not None:
                    s = s + bias_ref[c, kind]
                s_out[c, :, n * t:(n + 1) * t] = s
                block_max = functools.reduce(jnp.maximum, _lane_tiles(s))
                part = block_max if part is None else jnp.maximum(part, block_max)
            m_out[c] = jnp.broadcast_to(jnp.max(part, axis=-1, keepdims=True), (t, LANES))

    def stage_b(blocks, s_in, m_in, first=False):
        for c in range(2):
            if first:
                m_new = m_in[c]
                psum = jnp.zeros((t, LANES), _F32)
                pv = jnp.zeros(acc_ref.shape[1:], _F32)
            else:
                m_old = m_ref[c]
                m_new = jnp.maximum(m_old, m_in[c])
                alpha = jnp.exp2(m_old - m_new)
                psum = alpha * l_ref[c]
                pv = _rep(alpha, acc_ref.shape[2]) * acc_ref[c]
            m_rep = _rep(m_new, MXU_COLS)
            for n, (j, _) in enumerate(blocks):
                for u in range(t // MXU_COLS):
                    off = u * MXU_COLS
                    p = jnp.exp2(s_in[c, :, n * t + off:n * t + off + MXU_COLS] - m_rep)
                    psum = psum + functools.reduce(jnp.add, _lane_tiles(p))
                    v_rows = v_ref[pl.ds(pl.multiple_of(j * t + off, MXU_COLS), MXU_COLS), :]
                    pv = pv + jnp.dot(p.astype(_BF16), v_rows, preferred_element_type=_F32)
            l_ref[c] = psum
            acc_ref[c] = pv
            m_ref[c] = m_new

    g = stage_blocks
    j_sub = jnp.maximum(i - 1, 0)
    n_far = jnp.maximum(i - 1, 0) // g
    left_over = jnp.maximum(i - 1, 0) % g
    near = [(i, 0), (j_sub, jnp.where(i == 0, 2, 1))] + [(jnp.maximum(i - 2 - n, 0), None) for n in range(g - 1)]

    def far(f):
        return [(g * f + n, None) for n in range(g)]

    for width in range(2, g + 2):
        is_width = left_over == width - 2

        @pl.when(is_width)
        def _(width=width):
            stage_a(near[:width], sz_ref, mz_ref)

        @pl.when(jnp.logical_and(is_width, n_far == 0))
        def _(width=width):
            stage_b(near[:width], sz_ref, mz_ref, first=True)

        @pl.when(jnp.logical_and(is_width, n_far > 0))
        def _(width=width):
            stage_a(far(0), sy_ref, my_ref)
            stage_b(near[:width], sz_ref, mz_ref, first=True)

    n_rest = jnp.maximum(n_far - 1, 0)
    n_pairs = n_rest // 2

    def pair_body(u, carry):
        stage_a(far(2 * u + 1), sx_ref, mx_ref)
        stage_b(far(2 * u), sy_ref, my_ref)
        stage_a(far(2 * u + 2), sy_ref, my_ref)
        stage_b(far(2 * u + 1), sx_ref, mx_ref)
        return carry

    lax.fori_loop(0, n_pairs, pair_body, 0)

    @pl.when(jnp.logical_and(n_far > 0, n_rest % 2 == 1))
    def _():
        stage_a(far(n_far - 1), sx_ref, mx_ref)
        stage_b(far(2 * n_pairs), sy_ref, my_ref)
        stage_b(far(n_far - 1), sx_ref, mx_ref)

    @pl.when(jnp.logical_and(n_far > 0, n_rest % 2 == 0))
    def _():
        stage_b(far(2 * n_pairs), sy_ref, my_ref)

    lf = lam_ref[...]
    lam = (jnp.exp(jnp.sum(lf[0:1] * lf[1:2], axis=-1, keepdims=True))
           - jnp.exp(jnp.sum(lf[2:3] * lf[3:4], axis=-1, keepdims=True)) + lambda_init)
    l0 = jnp.sum(l_ref[0], axis=-1, keepdims=True)
    l1 = jnp.sum(l_ref[1], axis=-1, keepdims=True)
    o = acc_ref[0] / l0 - lam * (acc_ref[1] / l1)
    o = o * lax.rsqrt(jnp.mean(o * o, axis=-1, keepdims=True) + EPS)
    o_ref[...] = (o * gsub_ref[...] * (1.0 - lambda_init)).astype(o_ref.dtype)


def _diff_attention(qkv, td, ts, lam_qk, g_sub, B, S, D, t, stage_blocks, lambda_init):
    n_heads = D // (2 * HEAD_DIM)
    nq = S // t
    w = 2 * HEAD_DIM
    kern = functools.partial(_diff_attn_kernel, t=t, stage_blocks=stage_blocks, lambda_init=lambda_init)
    return pl.pallas_call(
        kern,
        grid=(B, n_heads, nq),
        in_specs=[pl.BlockSpec((t, w), lambda b, h, i: (b * nq + i, h)),
                  pl.BlockSpec((S, w), lambda b, h, i: (b, n_heads + h)),
                  pl.BlockSpec((S, w), lambda b, h, i: (b, 2 * n_heads + h)),
                  pl.BlockSpec((2, LANES, LANES), lambda b, h, i: (h, 0, 0)),
                  pl.BlockSpec((2, LANES, LANES), lambda b, h, i: (h, 0, 0)),
                  pl.BlockSpec((4, HEAD_DIM), lambda b, h, i: (0, 0)),
                  pl.BlockSpec((1, w), lambda b, h, i: (0, 0))],
        out_specs=pl.BlockSpec((t, w), lambda b, h, i: (b * nq + i, h)),
        out_shape=jax.ShapeDtypeStruct((B * S, D), _BF16),
        scratch_shapes=[pltpu.VMEM((2, 3, t, t), _F32),
                        pltpu.VMEM((2, t, stage_blocks * t), _F32),
                        pltpu.VMEM((2, t, stage_blocks * t), _F32),
                        pltpu.VMEM((2, t, (stage_blocks + 1) * t), _F32),
                        pltpu.VMEM((2, t, LANES), _F32),
                        pltpu.VMEM((2, t, LANES), _F32),
                        pltpu.VMEM((2, t, LANES), _F32),
                        pltpu.VMEM((2, t, LANES), _F32),
                        pltpu.VMEM((2, t, LANES), _F32),
                        pltpu.VMEM((2, t, w), _F32)],
        compiler_params=_params("arbitrary", "arbitrary", "arbitrary"),
        name="diff_attention",
    )(qkv, qkv, qkv, td, ts, lam_qk, g_sub)


def _swa_kernel(q_ref, kc_ref, kp_ref, vc_ref, vp_ref, bias_ref, sink_ref, o_ref, *, tq):
    first = pl.program_id(2) == 0
    rows = SWA_GROUP * WINDOW
    bias = bias_ref[...].reshape(rows, 2 * WINDOW)
    sink = jnp.concatenate(
        [jnp.broadcast_to(sink_ref[r:r + 1, :], (WINDOW, LANES)) for r in range(SWA_GROUP)], axis=0)
    col = lax.broadcasted_iota(jnp.int32, (rows, 2 * WINDOW), 1)
    ones = jnp.ones((2 * WINDOW, LANES), _BF16)
    for c in range(tq // WINDOW):
        cur = slice(c * WINDOW, (c + 1) * WINDOW)
        if c == 0:
            k_prev, v_prev = kp_ref[...], vp_ref[...]
        else:
            prev = slice((c - 1) * WINDOW, c * WINDOW)
            k_prev, v_prev = kc_ref[prev, :], vc_ref[prev, :]
        kk = jnp.concatenate([k_prev, kc_ref[cur, :]], axis=0)
        vv = jnp.concatenate([jnp.concatenate([v_prev, vc_ref[cur, :]], axis=0), ones], axis=1)
        q4 = jnp.concatenate(
            [q_ref[cur, r * HEAD_DIM:(r + 1) * HEAD_DIM] for r in range(SWA_GROUP)], axis=0)
        s = lax.dot_general(q4, kk, _NT, preferred_element_type=_F32) + bias
        if c == 0:
            s = jnp.where(first & (col < WINDOW), NEG, s)
        part = functools.reduce(jnp.maximum, _lane_tiles(s))
        m = jnp.maximum(jnp.broadcast_to(jnp.max(part, axis=-1, keepdims=True), (rows, LANES)), sink)
        p = jnp.exp2(s - jnp.concatenate([m, m], axis=1))
        pv = jnp.dot(p.astype(_BF16), vv, preferred_element_type=_F32)
        o = pv[:, :HEAD_DIM] / (pv[:, HEAD_DIM:] + jnp.exp2(sink - m))
        for r in range(SWA_GROUP):
            o_ref[cur, r * HEAD_DIM:(r + 1) * HEAD_DIM] = (
                o[r * WINDOW:(r + 1) * WINDOW, :].astype(o_ref.dtype))


def _swa_attention(q, kv, bias, sinks, B, S, D, tq):
    n_kv = D // (SWA_GROUP * HEAD_DIM)
    nt = S // tq
    bpt = tq // WINDOW
    gw = SWA_GROUP * HEAD_DIM

    def prev_map(off):
        return lambda b, g, t: (jnp.maximum((b * nt + t) * bpt - 1, 0), off + g)

    kern = functools.partial(_swa_kernel, tq=tq)
    return pl.pallas_call(
        kern,
        grid=(B, n_kv, nt),
        in_specs=[pl.BlockSpec((tq, gw), lambda b, g, t: (b * nt + t, g)),
                  pl.BlockSpec((tq, HEAD_DIM), lambda b, g, t: (b * nt + t, g)),
                  pl.BlockSpec((WINDOW, HEAD_DIM), prev_map(0)),
                  pl.BlockSpec((tq, HEAD_DIM), lambda b, g, t: (b * nt + t, n_kv + g)),
                  pl.BlockSpec((WINDOW, HEAD_DIM), prev_map(n_kv)),
                  pl.BlockSpec((SWA_GROUP, WINDOW, 2 * WINDOW), lambda b, g, t: (g, 0, 0)),
                  pl.BlockSpec((None, SWA_GROUP, LANES), lambda b, g, t: (g, 0, 0))],
        out_specs=pl.BlockSpec((tq, gw), lambda b, g, t: (b * nt + t, g)),
        out_shape=jax.ShapeDtypeStruct((B * S, D), _BF16),
        compiler_params=_params("arbitrary", "arbitrary", "arbitrary"),
        name="swa_attention",
    )(q, kv, kv, kv, kv, bias, sinks)


def _t5_bucket(n):
    max_exact = NUM_BUCKETS // 2
    nf = jnp.maximum(n, 1).astype(_F32)
    large = max_exact + (jnp.log(nf / max_exact) / math.log(MAX_DISTANCE / max_exact)
                         * (NUM_BUCKETS - max_exact)).astype(jnp.int32)
    large = jnp.minimum(large, NUM_BUCKETS - 1)
    return jnp.where(n < max_exact, n, large)


def _bias_tiles(rel_bias):
    def lookup(table, idx):
        onehot = (idx[..., None] == jnp.arange(table.shape[0])).astype(_F32)
        return jnp.einsum("...d,dh->h...", onehot, table, precision=lax.Precision.HIGHEST)

    lut = lookup(rel_bias.astype(_F32), _t5_bucket(jnp.arange(MAX_DISTANCE))).T * LOG2E
    shifted = lut - lut[MAX_DISTANCE - 1]
    qi = jnp.arange(LANES)[:, None]
    kj = jnp.arange(LANES)[None, :]
    clip = lambda dist: jnp.clip(dist, 0, MAX_DISTANCE - 1)
    d0 = qi - kj
    td = jnp.where((d0 >= 0)[None], lookup(shifted, clip(d0)), NEG)
    ts = lookup(shifted, clip(LANES + qi - kj))
    dw = WINDOW + qi - jnp.arange(2 * WINDOW)[None, :]
    swa = jnp.where(((dw >= 0) & (dw < WINDOW))[None], lookup(lut, clip(dw)), NEG)
    return td, ts, swa


def _forward(x, rel_bias, g_attn_norm, g_ffn_norm, w_qkv_a, w_o_a, g_q_a, g_k_a, lam_qk_a, g_sub_a,
             g_kv_norm, w_kv, g_k_shared, w_q_b, w_o_b, g_q_b, sinks_b, w_gate, w_up, w_down, tiles):
    B, S, D = x.shape
    M = B * S
    T = tiles
    q_scale = HEAD_DIM ** -0.5 * LOG2E
    n_maps = D // HEAD_DIM
    kvw = w_kv.shape[1] // 2
    td, ts, swa_bias = _bias_tiles(rel_bias)
    bf = _cast_bf16
    x = x.reshape(M, D)

    def ffn(x, xb, inv_rms, layer, with_stats):
        act = _gate_up(xb, inv_rms, w_gate, w_up, layer, g_ffn_norm[layer], T["mm_tm"], T["gu_tn"])
        return _down_residual(act, bf(w_down, layer), x, with_stats, T["down_tm"], T["down_tk"], T["down_tn"])

    lambda_init = 0.8 - 0.6 * math.exp(-0.3 * 0)
    xb, inv_rms = _prep(x, T["rms_tm"])
    g_cols = jnp.concatenate([jnp.tile(g_q_a[0] * q_scale, n_maps), jnp.tile(g_k_a[0], n_maps),
                              jnp.ones((D,), _F32)])[None]
    qkv = _proj_headnorm(xb, inv_rms, bf(w_qkv_a, 0, g_attn_norm[0]), g_cols, 2 * D, T["mm_tm"], T["qkv_tn"])
    o = _diff_attention(qkv, td, ts, lam_qk_a[0], g_sub_a[0][None], B, S, D, T["attn_t"], T["attn_stage_blocks"],
                        lambda_init)
    x, xb, inv_rms = _proj_residual(o, bf(w_o_a, 0), x, T["mm_tm"], T["res_tn"])
    x, xb, inv_rms = ffn(x, xb, inv_rms, 0, True)

    g_kv_cols = jnp.concatenate([jnp.tile(g_k_shared, kvw // HEAD_DIM), jnp.ones((kvw,), _F32)])[None]
    kv = _proj_headnorm(xb, inv_rms, bf(w_kv[None], 0, g_kv_norm), g_kv_cols, kvw,
                        T["mm_tm"], min(T["qkv_tn"], kvw))
    q = _proj_headnorm(xb, inv_rms, bf(w_q_b, 0, g_attn_norm[1]), jnp.tile(g_q_b[0] * q_scale, n_maps)[None],
                       D, T["mm_tm"], T["qkv_tn"])
    sinks = jnp.broadcast_to((sinks_b[0].astype(_F32) * LOG2E).reshape(-1, SWA_GROUP, 1),
                             (n_maps // SWA_GROUP, SWA_GROUP, LANES))
    o = _swa_attention(q, kv, swa_bias, sinks, B, S, D, T["swa_tq"])
    x, xb, inv_rms = _proj_residual(o, bf(w_o_b, 0), x, T["mm_tm"], T["res_tn"])
    (x,) = ffn(x, xb, inv_rms, 1, False)
    return x.reshape(B, S, D)


def kernel(x, rel_bias, g_attn_norm, g_ffn_norm, w_qkv_a, w_o_a, g_q_a, g_k_a, lam_qk_a, g_sub_a,
           g_kv_norm, w_kv, g_k_shared, w_q_b, w_o_b, g_q_b, sinks_b, w_gate, w_up, w_down):
    B, S, D = x.shape
    tiles = _tiles(B * S, S, D, w_gate.shape[-1])
    return _forward(x, rel_bias, g_attn_norm, g_ffn_norm, w_qkv_a, w_o_a, g_q_a, g_k_a, lam_qk_a,
                    g_sub_a, g_kv_norm, w_kv, g_k_shared, w_q_b, w_o_b, g_q_b, sinks_b,
                    w_gate, w_up, w_down, tiles)
```

```python
import functools
import math

import jax
import jax.numpy as jnp
from jax import lax
from jax.experimental import pallas as pl
from jax.experimental.pallas import tpu as pltpu

HEAD_DIM = 128
WINDOW = 128
SWA_GROUP = 4
NUM_BUCKETS = 32
MAX_DISTANCE = 128
EPS = 1e-6
NEG = -1e30
LOG2E = math.log2(math.e)
LANES = 128
VMEM_LIMIT_BYTES = 56 * 1024 * 1024
CAST_BLOCK_BYTES = 8 * 1024 * 1024
MXU_COLS = 256

_F32 = jnp.float32
_BF16 = jnp.bfloat16
_NT = (((1,), (1,)), ((), ()))


def _params(*sem):
    return pltpu.CompilerParams(dimension_semantics=sem, vmem_limit_bytes=VMEM_LIMIT_BYTES)


def _tiles(M, S, D, D_FF):
    return dict(
        mm_tm=min(1024, M),
        qkv_tm=min(512, M),
        qkv_tn=min(2048, D),
        qkv_raw_tn=min(1024, D),
        res_tm=min(512, M),
        res_tn=min(1024, D),
        gu_tn=min(256, D_FF),
        down_tm=min(512, M),
        down_tk=D_FF // 2,
        down_tn=min(1024, D),
        attn_t=min(512, S // 4),
        attn_stage_blocks=3,
        swa_tq=min(1024, S),
    )


def _lane_tiles(a):
    return [a[:, u * LANES:(u + 1) * LANES] for u in range(a.shape[1] // LANES)]


def _rep(a, width):
    return a if width == LANES else jnp.concatenate([a] * (width // LANES), axis=1)


def _lane_rep(v):
    return jnp.broadcast_to(v.astype(_F32)[:, None], (v.shape[0], LANES))


def _cast_kernel(w_ref, *rest):
    o_ref = rest[-1]
    w = w_ref[...]
    if len(rest) == 2:
        w = w * _rep(rest[0][...], w.shape[1])
    o_ref[...] = w.astype(o_ref.dtype)


def _cast_bf16(w, layer, row_gain=None):
    _, R, C = w.shape
    tr = min(R, max(8, CAST_BLOCK_BYTES // (4 * C) // 8 * 8))
    while R % tr:
        tr -= 8
    args, specs = [w], [pl.BlockSpec((None, tr, C), lambda i: (layer, i, 0))]
    if row_gain is not None:
        args.append(_lane_rep(row_gain))
        specs.append(pl.BlockSpec((tr, LANES), lambda i: (i, 0)))
    return pl.pallas_call(
        _cast_kernel,
        grid=(R // tr,),
        in_specs=specs,
        out_specs=pl.BlockSpec((tr, C), lambda i: (i, 0)),
        out_shape=jax.ShapeDtypeStruct((R, C), _BF16),
        compiler_params=_params("arbitrary"),
        name="cast_bf16",
    )(*args)


def _bf16_and_inv_rms(x_ref, xb_ref, r_ref):
    x = x_ref[...]
    xb_ref[...] = x.astype(xb_ref.dtype)
    r_ref[...] = jnp.broadcast_to(lax.rsqrt(jnp.mean(x * x, axis=-1, keepdims=True) + EPS), r_ref.shape)


def _accumulate_inv_rms(r_ref, ssq, first, last, width):
    @pl.when(first)
    def _():
        r_ref[...] = ssq

    @pl.when(jnp.logical_not(first))
    def _():
        r_ref[...] += ssq

    @pl.when(last)
    def _():
        total = jnp.sum(r_ref[...], axis=-1, keepdims=True)
        r_ref[...] = jnp.broadcast_to(lax.rsqrt(total / width + EPS), r_ref.shape)


def _row_tile_spec(tm, K):
    return pl.BlockSpec((tm, K), lambda i, j: (i, 0))


def _col_groups(width):
    sub = min(MXU_COLS, width)
    return [slice(u * sub, (u + 1) * sub) for u in range(width // sub)]


def _proj_headnorm_kernel(x_ref, *refs, n_norm_tiles, n_tiles, raw_input):
    if raw_input:
        w_ref, g_ref, o_ref, xb_ref, r_ref = refs

        @pl.when(pl.program_id(1) == 0)
        def _():
            _bf16_and_inv_rms(x_ref, xb_ref, r_ref)

        x_ref = xb_ref
    else:
        r_ref, w_ref, g_ref, o_ref = refs

    def sub_dots():
        for cols in _col_groups(o_ref.shape[1]):
            y = jnp.dot(x_ref[...], w_ref[:, cols], preferred_element_type=_F32)
            yield cols, _rep(r_ref[...], y.shape[1]) * y

    def normed():
        for cols, y in sub_dots():
            for c in range(y.shape[1] // HEAD_DIM):
                yc = y[:, c * HEAD_DIM:(c + 1) * HEAD_DIM]
                sl = slice(cols.start + c * HEAD_DIM, cols.start + (c + 1) * HEAD_DIM)
                r = lax.rsqrt(jnp.mean(yc * yc, axis=-1, keepdims=True) + EPS)
                o_ref[:, sl] = (yc * r * g_ref[:, sl]).astype(o_ref.dtype)

    def plain():
        for cols, y in sub_dots():
            o_ref[:, cols] = y.astype(o_ref.dtype)

    if n_norm_tiles == n_tiles:
        normed()
    else:
        j = pl.program_id(1)
        pl.when(j < n_norm_tiles)(normed)
        pl.when(j >= n_norm_tiles)(plain)


def _proj_headnorm(xb, inv_rms, w, g_cols, n_norm_cols, tm, tn):
    M, K = xb.shape
    N = w.shape[1]
    raw_input = inv_rms is None
    kern = functools.partial(_proj_headnorm_kernel, n_norm_tiles=n_norm_cols // tn, n_tiles=N // tn,
                             raw_input=raw_input)
    stats = [] if raw_input else [inv_rms]
    return pl.pallas_call(
        kern,
        grid=(M // tm, N // tn),
        in_specs=[_row_tile_spec(tm, K)]
                 + [pl.BlockSpec((tm, LANES), lambda i, j: (i, 0))] * len(stats)
                 + [pl.BlockSpec((K, tn), lambda i, j: (0, j)),
                    pl.BlockSpec((1, tn), lambda i, j: (0, j))],
        out_specs=pl.BlockSpec((tm, tn), lambda i, j: (i, j)),
        out_shape=jax.ShapeDtypeStruct((M, N), _BF16),
        scratch_shapes=[pltpu.VMEM((tm, K), _BF16), pltpu.VMEM((tm, LANES), _F32)] if raw_input else [],
        compiler_params=_params("arbitrary", "arbitrary"),
        name="proj_headnorm",
    )(xb, *stats, w, g_cols)


def _store_residual_tile(cols, y, o_ref, stat_refs, ssq):
    o_ref[:, cols] = y
    if not stat_refs:
        return None
    stat_refs[0][:, cols] = y.astype(_BF16)
    part = functools.reduce(jnp.add, _lane_tiles(y * y))
    return part if ssq is None else ssq + part


def _proj_residual_kernel(x_ref, w_ref, res_ref, o_ref, ob_ref, r_ref, *, width):
    j = pl.program_id(1)
    ssq = None
    for cols in _col_groups(o_ref.shape[1]):
        y = res_ref[:, cols] + jnp.dot(x_ref[...], w_ref[:, cols], preferred_element_type=_F32)
        ssq = _store_residual_tile(cols, y, o_ref, (ob_ref,), ssq)
    _accumulate_inv_rms(r_ref, ssq, j == 0, j == pl.num_programs(1) - 1, width)


def _proj_residual(x, w, res, tm, tn):
    M, K = x.shape
    N = w.shape[1]
    tile = pl.BlockSpec((tm, tn), lambda i, j: (i, j))
    return pl.pallas_call(
        functools.partial(_proj_residual_kernel, width=N),
        grid=(M // tm, N // tn),
        in_specs=[_row_tile_spec(tm, K),
                  pl.BlockSpec((K, tn), lambda i, j: (0, j)),
                  tile],
        out_specs=[tile, tile, pl.BlockSpec((tm, LANES), lambda i, j: (i, 0))],
        out_shape=[jax.ShapeDtypeStruct((M, N), _F32), jax.ShapeDtypeStruct((M, N), _BF16),
                   jax.ShapeDtypeStruct((M, LANES), _F32)],
        compiler_params=_params("arbitrary", "arbitrary"),
        name="proj_residual",
    )(x, w, res)


def _gate_up_kernel(x_ref, r_ref, wg_ref, wu_ref, gain_ref, o_ref):
    x = x_ref[...]
    tn = o_ref.shape[1]
    gain = _rep(gain_ref[...], tn)
    r = _rep(r_ref[...], tn)
    g = r * jnp.dot(x, (wg_ref[...] * gain).astype(_BF16), preferred_element_type=_F32)
    u = r * jnp.dot(x, (wu_ref[...] * gain).astype(_BF16), preferred_element_type=_F32)
    o_ref[...] = (g * jax.nn.sigmoid(g) * u).astype(o_ref.dtype)


def _gate_up(xb, inv_rms, wg, wu, layer, gain, tm, tn):
    M, K = xb.shape
    N = wg.shape[2]
    w_spec = pl.BlockSpec((None, K, tn), lambda i, j: (layer, 0, j))
    return pl.pallas_call(
        _gate_up_kernel,
        grid=(M // tm, N // tn),
        in_specs=[_row_tile_spec(tm, K),
                  pl.BlockSpec((tm, LANES), lambda i, j: (i, 0)),
                  w_spec, w_spec,
                  pl.BlockSpec((K, LANES), lambda i, j: (0, 0))],
        out_specs=pl.BlockSpec((tm, tn), lambda i, j: (i, j)),
        out_shape=jax.ShapeDtypeStruct((M, N), _BF16),
        compiler_params=_params("arbitrary", "arbitrary"),
        name="ffn_gate_up",
    )(xb, inv_rms, wg, wu, _lane_rep(gain))


def _down_kernel(x_ref, w_ref, res_ref, o_ref, *rest, width):
    *stat_refs, acc_ref = rest
    k = pl.program_id(1)
    j = pl.program_id(2)

    def parts():
        for cols in _col_groups(o_ref.shape[1]):
            yield cols, jnp.dot(x_ref[...], w_ref[:, cols], preferred_element_type=_F32)

    @pl.when(k == 0)
    def _():
        for cols, part in parts():
            acc_ref[j, :, cols] = part

    @pl.when(k == pl.num_programs(1) - 1)
    def _():
        ssq = None
        for cols, part in parts():
            y = res_ref[:, cols] + acc_ref[j, :, cols] + part
            ssq = _store_residual_tile(cols, y, o_ref, stat_refs, ssq)
        if stat_refs:
            _accumulate_inv_rms(stat_refs[1], ssq, j == 0, j == pl.num_programs(2) - 1, width)


def _down_residual(x, w, res, with_stats, tm, tk, tn):
    M, K = x.shape
    N = w.shape[1]
    nk = K // tk
    assert nk == 2
    tile = pl.BlockSpec((tm, tn), lambda i, k, j: (i, j * k))
    out_specs, out_shape = [tile], [jax.ShapeDtypeStruct((M, N), _F32)]
    if with_stats:
        out_specs += [tile, pl.BlockSpec((tm, LANES), lambda i, k, j: (i, 0))]
        out_shape += [jax.ShapeDtypeStruct((M, N), _BF16), jax.ShapeDtypeStruct((M, LANES), _F32)]
    return pl.pallas_call(
        functools.partial(_down_kernel, width=N),
        grid=(M // tm, nk, N // tn),
        in_specs=[pl.BlockSpec((tm, tk), lambda i, k, j: (i, k)),
                  pl.BlockSpec((tk, tn), lambda i, k, j: (k, j)),
                  tile],
        out_specs=out_specs,
        out_shape=out_shape,
        scratch_shapes=[pltpu.VMEM((N // tn, tm, tn), _F32)],
        compiler_params=_params("arbitrary", "arbitrary", "arbitrary"),
        name="ffn_down_residual",
    )(x, w, res)


def _diff_attn_kernel(q_ref, k_ref, v_ref, td_ref, ts_ref, lam_ref, gsub_ref, o_ref,
                      bias_ref, sx_ref, sy_ref, sz_ref, mx_ref, my_ref, mz_ref, m_ref, l_ref, acc_ref,
                      *, t, stage_blocks, lambda_init):
    i = pl.program_id(2)
    nt = t // LANES

    @pl.when(i == 0)
    def _build_bias():
        zeros = jnp.zeros((LANES, LANES), _F32)
        neg = jnp.full((LANES, LANES), NEG, _F32)
        for c in range(2):
            for a in range(nt):
                for b in range(nt):
                    rows = slice(a * LANES, (a + 1) * LANES)
                    cols = slice(b * LANES, (b + 1) * LANES)
                    if a == b:
                        tile = td_ref[c]
                    elif a == b + 1:
                        tile = ts_ref[c]
                    else:
                        tile = zeros if a > b else neg
                    bias_ref[c, 0, rows, cols] = tile
                    bias_ref[c, 1, rows, cols] = ts_ref[c] if (a == 0 and b == nt - 1) else zeros
                    bias_ref[c, 2, rows, cols] = neg

    def stage_a(blocks, s_out, m_out):
        for c in range(2):
            cols = slice(c * HEAD_DIM, (c + 1) * HEAD_DIM)
            part = None
            for n, (j, kind) in enumerate(blocks):
                start = pl.multiple_of(j * t, t)
                s = lax.dot_general(q_ref[:, cols], k_ref[pl.ds(start, t), cols], _NT,
                                    preferred_element_type=_F32)
                if kind is not None:
                    s = s + bias_ref[c, kind]
                s_out[c, :, n * t:(n + 1) * t] = s
                block_max = functools.reduce(jnp.maximum, _lane_tiles(s))
                part = block_max if part is None else jnp.maximum(part, block_max)
            m_out[c] = jnp.broadcast_to(jnp.max(part, axis=-1, keepdims=True), (t, LANES))

    def stage_b(blocks, s_in, m_in, first=False):
        for c in range(2):
            if first:
                m_new = m_in[c]
                psum = jnp.zeros((t, LANES), _F32)
                pv = jnp.zeros(acc_ref.shape[1:], _F32)
            else:
                m_old = m_ref[c]
                m_new = jnp.maximum(m_old, m_in[c])
                alpha = jnp.exp2(m_old - m_new)
                psum = alpha * l_ref[c]
                pv = _rep(alpha, acc_ref.shape[2]) * acc_ref[c]
            m_rep = _rep(m_new, MXU_COLS)
            for n, (j, _) in enumerate(blocks):
                for u in range(t // MXU_COLS):
                    off = u * MXU_COLS
                    p = jnp.exp2(s_in[c, :, n * t + off:n * t + off + MXU_COLS] - m_rep)
                    psum = psum + functools.reduce(jnp.add, _lane_tiles(p))
                    v_rows = v_ref[pl.ds(pl.multiple_of(j * t + off, MXU_COLS), MXU_COLS), :]
                    pv = pv + jnp.dot(p.astype(_BF16), v_rows, preferred_element_type=_F32)
            l_ref[c] = psum
            acc_ref[c] = pv
            m_ref[c] = m_new

    g = stage_blocks
    j_sub = jnp.maximum(i - 1, 0)
    n_far = jnp.maximum(i - 1, 0) // g
    left_over = jnp.maximum(i - 1, 0) % g
    near = [(i, 0), (j_sub, jnp.where(i == 0, 2, 1))] + [(jnp.maximum(i - 2 - n, 0), None) for n in range(g - 1)]

    def far(f):
        return [(g * f + n, None) for n in range(g)]

    for width in range(2, g + 2):
        is_width = left_over == width - 2

        @pl.when(is_width)
        def _(width=width):
            stage_a(near[:width], sz_ref, mz_ref)

        @pl.when(jnp.logical_and(is_width, n_far == 0))
        def _(width=width):
            stage_b(near[:width], sz_ref, mz_ref, first=True)

        @pl.when(jnp.logical_and(is_width, n_far > 0))
        def _(width=width):
            stage_a(far(0), sy_ref, my_ref)
            stage_b(near[:width], sz_ref, mz_ref, first=True)

    n_rest = jnp.maximum(n_far - 1, 0)
    n_pairs = n_rest // 2

    def pair_body(u, carry):
        stage_a(far(2 * u + 1), sx_ref, mx_ref)
        stage_b(far(2 * u), sy_ref, my_ref)
        stage_a(far(2 * u + 2), sy_ref, my_ref)
        stage_b(far(2 * u + 1), sx_ref, mx_ref)
        return carry

    lax.fori_loop(0, n_pairs, pair_body, 0)

    @pl.when(jnp.logical_and(n_far > 0, n_rest % 2 == 1))
    def _():
        stage_a(far(n_far - 1), sx_ref, mx_ref)
        stage_b(far(2 * n_pairs), sy_ref, my_ref)
        stage_b(far(n_far - 1), sx_ref, mx_ref)

    @pl.when(jnp.logical_and(n_far > 0, n_rest % 2 == 0))
    def _():
        stage_b(far(2 * n_pairs), sy_ref, my_ref)

    lf = lam_ref[...]
    lam = (jnp.exp(jnp.sum(lf[0:1] * lf[1:2], axis=-1, keepdims=True))
           - jnp.exp(jnp.sum(lf[2:3] * lf[3:4], axis=-1, keepdims=True)) + lambda_init)
    l0 = jnp.sum(l_ref[0], axis=-1, keepdims=True)
    l1 = jnp.sum(l_ref[1], axis=-1, keepdims=True)
    o = acc_ref[0] / l0 - lam * (acc_ref[1] / l1)
    o = o * lax.rsqrt(jnp.mean(o * o, axis=-1, keepdims=True) + EPS)
    o_ref[...] = (o * gsub_ref[...] * (1.0 - lambda_init)).astype(o_ref.dtype)


def _diff_attention(qkv, td, ts, lam_qk, g_sub, B, S, D, t, stage_blocks, lambda_init):
    n_heads = D // (2 * HEAD_DIM)
    nq = S // t
    w = 2 * HEAD_DIM
    kern = functools.partial(_diff_attn_kernel, t=t, stage_blocks=stage_blocks, lambda_init=lambda_init)
    return pl.pallas_call(
        kern,
        grid=(B, n_heads, nq),
        in_specs=[pl.BlockSpec((t, w), lambda b, h, i: (b * nq + i, h)),
                  pl.BlockSpec((S, w), lambda b, h, i: (b, n_heads + h)),
                  pl.BlockSpec((S, w), lambda b, h, i: (b, 2 * n_heads + h)),
                  pl.BlockSpec((2, LANES, LANES), lambda b, h, i: (h, 0, 0)),
                  pl.BlockSpec((2, LANES, LANES), lambda b, h, i: (h, 0, 0)),
                  pl.BlockSpec((4, HEAD_DIM), lambda b, h, i: (0, 0)),
                  pl.BlockSpec((1, w), lambda b, h, i: (0, 0))],
        out_specs=pl.BlockSpec((t, w), lambda b, h, i: (b * nq + i, h)),
        out_shape=jax.ShapeDtypeStruct((B * S, D), _BF16),
        scratch_shapes=[pltpu.VMEM((2, 3, t, t), _F32),
                        pltpu.VMEM((2, t, stage_blocks * t), _F32),
                        pltpu.VMEM((2, t, stage_blocks * t), _F32),
                        pltpu.VMEM((2, t, (stage_blocks + 1) * t), _F32),
                        pltpu.VMEM((2, t, LANES), _F32),
                        pltpu.VMEM((2, t, LANES), _F32),
                        pltpu.VMEM((2, t, LANES), _F32),
                        pltpu.VMEM((2, t, LANES), _F32),
                        pltpu.VMEM((2, t, LANES), _F32),
                        pltpu.VMEM((2, t, w), _F32)],
        compiler_params=_params("arbitrary", "arbitrary", "arbitrary"),
        name="diff_attention",
    )(qkv, qkv, qkv, td, ts, lam_qk, g_sub)


def _swa_kernel(q_ref, kc_ref, kp_ref, vc_ref, vp_ref, bias_ref, sink_ref, o_ref, *, tq):
    first = pl.program_id(2) == 0
    rows = SWA_GROUP * WINDOW
    bias = bias_ref[...].reshape(rows, 2 * WINDOW)
    sink = jnp.concatenate(
        [jnp.broadcast_to(sink_ref[r:r + 1, :], (WINDOW, LANES)) for r in range(SWA_GROUP)], axis=0)
    col = lax.broadcasted_iota(jnp.int32, (rows, 2 * WINDOW), 1)
    ones = jnp.ones((2 * WINDOW, LANES), _BF16)
    for c in range(tq // WINDOW):
        cur = slice(c * WINDOW, (c + 1) * WINDOW)
        if c == 0:
            k_prev, v_prev = kp_ref[...], vp_ref[...]
        else:
            prev = slice((c - 1) * WINDOW, c * WINDOW)
            k_prev, v_prev = kc_ref[prev, :], vc_ref[prev, :]
        kk = jnp.concatenate([k_prev, kc_ref[cur, :]], axis=0)
        vv = jnp.concatenate([jnp.concatenate([v_prev, vc_ref[cur, :]], axis=0), ones], axis=1)
        q4 = jnp.concatenate(
            [q_ref[cur, r * HEAD_DIM:(r + 1) * HEAD_DIM] for r in range(SWA_GROUP)], axis=0)
        s = lax.dot_general(q4, kk, _NT, preferred_element_type=_F32) + bias
        if c == 0:
            s = jnp.where(first & (col < WINDOW), NEG, s)
        part = functools.reduce(jnp.maximum, _lane_tiles(s))
        m = jnp.maximum(jnp.broadcast_to(jnp.max(part, axis=-1, keepdims=True), (rows, LANES)), sink)
        p = jnp.exp2(s - jnp.concatenate([m, m], axis=1))
        pv = jnp.dot(p.astype(_BF16), vv, preferred_element_type=_F32)
        o = pv[:, :HEAD_DIM] / (pv[:, HEAD_DIM:] + jnp.exp2(sink - m))
        for r in range(SWA_GROUP):
            o_ref[cur, r * HEAD_DIM:(r + 1) * HEAD_DIM] = (
                o[r * WINDOW:(r + 1) * WINDOW, :].astype(o_ref.dtype))


def _swa_attention(q, kv, bias, sinks, B, S, D, tq):
    n_kv = D // (SWA_GROUP * HEAD_DIM)
    nt = S // tq
    bpt = tq // WINDOW
    gw = SWA_GROUP * HEAD_DIM

    def prev_map(off):
        return lambda b, g, t: (jnp.maximum((b * nt + t) * bpt - 1, 0), off + g)

    kern = functools.partial(_swa_kernel, tq=tq)
    return pl.pallas_call(
        kern,
        grid=(B, n_kv, nt),
        in_specs=[pl.BlockSpec((tq, gw), lambda b, g, t: (b * nt + t, g)),
                  pl.BlockSpec((tq, HEAD_DIM), lambda b, g, t: (b * nt + t, g)),
                  pl.BlockSpec((WINDOW, HEAD_DIM), prev_map(0)),
                  pl.BlockSpec((tq, HEAD_DIM), lambda b, g, t: (b * nt + t, n_kv + g)),
                  pl.BlockSpec((WINDOW, HEAD_DIM), prev_map(n_kv)),
                  pl.BlockSpec((SWA_GROUP, WINDOW, 2 * WINDOW), lambda b, g, t: (g, 0, 0)),
                  pl.BlockSpec((None, SWA_GROUP, LANES), lambda b, g, t: (g, 0, 0))],
        out_specs=pl.BlockSpec((tq, gw), lambda b, g, t: (b * nt + t, g)),
        out_shape=jax.ShapeDtypeStruct((B * S, D), _BF16),
        compiler_params=_params("arbitrary", "arbitrary", "arbitrary"),
        name="swa_attention",
    )(q, kv, kv, kv, kv, bias, sinks)


def _t5_bucket(n):
    max_exact = NUM_BUCKETS // 2
    nf = jnp.maximum(n, 1).astype(_F32)
    large = max_exact + (jnp.log(nf / max_exact) / math.log(MAX_DISTANCE / max_exact)
                         * (NUM_BUCKETS - max_exact)).astype(jnp.int32)
    large = jnp.minimum(large, NUM_BUCKETS - 1)
    return jnp.where(n < max_exact, n, large)


def _bias_tiles(rel_bias):
    def lookup(table, idx):
        onehot = (idx[..., None] == jnp.arange(table.shape[0])).astype(_F32)
        return jnp.einsum("...d,dh->h...", onehot, table, precision=lax.Precision.HIGHEST)

    lut = lookup(rel_bias.astype(_F32), _t5_bucket(jnp.arange(MAX_DISTANCE))).T * LOG2E
    shifted = lut - lut[MAX_DISTANCE - 1]
    qi = jnp.arange(LANES)[:, None]
    kj = jnp.arange(LANES)[None, :]
    clip = lambda dist: jnp.clip(dist, 0, MAX_DISTANCE - 1)
    d0 = qi - kj
    td = jnp.where((d0 >= 0)[None], lookup(shifted, clip(d0)), NEG)
    ts = lookup(shifted, clip(LANES + qi - kj))
    dw = WINDOW + qi - jnp.arange(2 * WINDOW)[None, :]
    swa = jnp.where(((dw >= 0) & (dw < WINDOW))[None], lookup(lut, clip(dw)), NEG)
    return td, ts, swa


def _forward(x, rel_bias, g_attn_norm, g_ffn_norm, w_qkv_a, w_o_a, g_q_a, g_k_a, lam_qk_a, g_sub_a,
             g_kv_norm, w_kv, g_k_shared, w_q_b, w_o_b, g_q_b, sinks_b, w_gate, w_up, w_down, tiles):
    B, S, D = x.shape
    M = B * S
    T = tiles
    q_scale = HEAD_DIM ** -0.5 * LOG2E
    n_maps = D // HEAD_DIM
    kvw = w_kv.shape[1] // 2
    td, ts, swa_bias = _bias_tiles(rel_bias)
    bf = _cast_bf16
    x = x.reshape(M, D)

    def ffn(x, xb, inv_rms, layer, with_stats):
        act = _gate_up(xb, inv_rms, w_gate, w_up, layer, g_ffn_norm[layer], T["mm_tm"], T["gu_tn"])
        return _down_residual(act, bf(w_down, layer), x, with_stats, T["down_tm"], T["down_tk"], T["down_tn"])

    lambda_init = 0.8 - 0.6 * math.exp(-0.3 * 0)
    g_cols = jnp.concatenate([jnp.tile(g_q_a[0] * q_scale, n_maps), jnp.tile(g_k_a[0], n_maps),
                              jnp.ones((D,), _F32)])[None]
    qkv = _proj_headnorm(x, None, bf(w_qkv_a, 0, g_attn_norm[0]), g_cols, 2 * D, T["qkv_tm"], T["qkv_raw_tn"])
    o = _diff_attention(qkv, td, ts, lam_qk_a[0], g_sub_a[0][None], B, S, D, T["attn_t"], T["attn_stage_blocks"],
                        lambda_init)
    x, xb, inv_rms = _proj_residual(o, bf(w_o_a, 0), x, T["res_tm"], T["res_tn"])
    x, xb, inv_rms = ffn(x, xb, inv_rms, 0, True)

    g_kv_cols = jnp.concatenate([jnp.tile(g_k_shared, kvw // HEAD_DIM), jnp.ones((kvw,), _F32)])[None]
    kv = _proj_headnorm(xb, inv_rms, bf(w_kv[None], 0, g_kv_norm), g_kv_cols, kvw,
                        T["qkv_tm"], min(T["qkv_tn"], kvw))
    q = _proj_headnorm(xb, inv_rms, bf(w_q_b, 0, g_attn_norm[1]), jnp.tile(g_q_b[0] * q_scale, n_maps)[None],
                       D, T["qkv_tm"], T["qkv_tn"])
    sinks = jnp.broadcast_to((sinks_b[0].astype(_F32) * LOG2E).reshape(-1, SWA_GROUP, 1),
                             (n_maps // SWA_GROUP, SWA_GROUP, LANES))
    o = _swa_attention(q, kv, swa_bias, sinks, B, S, D, T["swa_tq"])
    x, xb, inv_rms = _proj_residual(o, bf(w_o_b, 0), x, T["res_tm"], T["res_tn"])
    (x,) = ffn(x, xb, inv_rms, 1, False)
    return x.reshape(B, S, D)


def kernel(x, rel_bias, g_attn_norm, g_ffn_norm, w_qkv_a, w_o_a, g_q_a, g_k_a, lam_qk_a, g_sub_a,
           g_kv_norm, w_kv, g_k_shared, w_q_b, w_o_b, g_q_b, sinks_b, w_gate, w_up, w_down):
    B, S, D = x.shape
    tiles = _tiles(B * S, S, D, w_gate.shape[-1])
    return _forward(x, rel_bias, g_attn_norm, g_ffn_norm, w_qkv_a, w_o_a, g_q_a, g_k_a, lam_qk_a,
                    g_sub_a, g_kv_norm, w_kv, g_k_shared, w_q_b, w_o_b, g_q_b, sinks_b,
                    w_gate, w_up, w_down, tiles)
```

```python
import functools
import math

import jax
import jax.numpy as jnp
from jax import lax
from jax.experimental import pallas as pl
from jax.experimental.pallas import tpu as pltpu

HEAD_DIM = 128
WINDOW = 128
SWA_GROUP = 4
NUM_BUCKETS = 32
MAX_DISTANCE = 128
EPS = 1e-6
NEG = -1e30
LOG2E = math.log2(math.e)
LANES = 128
VMEM_LIMIT_BYTES = 56 * 1024 * 1024
CAST_BLOCK_BYTES = 8 * 1024 * 1024
MXU_COLS = 256
BF16_SUBLANES = 16

_F32 = jnp.float32
_BF16 = jnp.bfloat16
_NT = (((1,), (1,)), ((), ()))


def _params(*sem):
    return pltpu.CompilerParams(dimension_semantics=sem, vmem_limit_bytes=VMEM_LIMIT_BYTES)


def _tiles(M, S, D, D_FF):
    return dict(
        rms_tm=min(256, M),
        mm_tm=min(1024, M),
        qkv_tn=min(1024, D),
        res_tn=min(512, D),
        gu_tn=min(256, D_FF),
        down_tm=min(512, M),
        down_tk=D_FF // 2,
        down_tn=min(1024, D),
        attn_t=min(512, S // 4),
        attn_stage_blocks=3,
        swa_tq=min(1024, S),
    )


def _lane_tiles(a):
    return [a[:, u * LANES:(u + 1) * LANES] for u in range(a.shape[1] // LANES)]


def _rep(a, width):
    return a if width == LANES else jnp.concatenate([a] * (width // LANES), axis=1)


def _lane_rep(v):
    return jnp.broadcast_to(v.astype(_F32)[:, None], (v.shape[0], LANES))


def _cast_kernel(w_ref, *rest):
    o_ref = rest[-1]
    w = w_ref[...]
    if len(rest) == 2:
        w = w * _rep(rest[0][...], w.shape[1])
    o_ref[...] = w.astype(o_ref.dtype)


def _cast_bf16(w, layer, row_gain=None):
    _, R, C = w.shape
    tr = min(R, max(8, CAST_BLOCK_BYTES // (4 * C) // 8 * 8))
    while R % tr:
        tr -= 8
    args, specs = [w], [pl.BlockSpec((None, tr, C), lambda i: (layer, i, 0))]
    if row_gain is not None:
        args.append(_lane_rep(row_gain))
        specs.append(pl.BlockSpec((tr, LANES), lambda i: (i, 0)))
    return pl.pallas_call(
        _cast_kernel,
        grid=(R // tr,),
        in_specs=specs,
        out_specs=pl.BlockSpec((tr, C), lambda i: (i, 0)),
        out_shape=jax.ShapeDtypeStruct((R, C), _BF16),
        compiler_params=_params("arbitrary"),
        name="cast_bf16",
    )(*args)


def _prep_kernel(x_ref, xb_ref, r_ref):
    x = x_ref[...]
    xb_ref[...] = x.astype(xb_ref.dtype)
    r_ref[...] = jnp.broadcast_to(lax.rsqrt(jnp.mean(x * x, axis=-1, keepdims=True) + EPS), r_ref.shape)


def _prep(x, tm):
    M, D = x.shape
    return pl.pallas_call(
        _prep_kernel,
        grid=(M // tm,),
        in_specs=[pl.BlockSpec((tm, D), lambda i: (i, 0))],
        out_specs=[pl.BlockSpec((tm, D), lambda i: (i, 0)), pl.BlockSpec((tm, LANES), lambda i: (i, 0))],
        out_shape=[jax.ShapeDtypeStruct((M, D), _BF16), jax.ShapeDtypeStruct((M, LANES), _F32)],
        compiler_params=_params("arbitrary"),
        name="rms_stats",
    )(x)


def _accumulate_inv_rms(r_ref, ssq, first, last, width):
    @pl.when(first)
    def _():
        r_ref[...] = ssq

    @pl.when(jnp.logical_not(first))
    def _():
        r_ref[...] += ssq

    @pl.when(last)
    def _():
        total = jnp.sum(r_ref[...], axis=-1, keepdims=True)
        r_ref[...] = jnp.broadcast_to(lax.rsqrt(total / width + EPS), r_ref.shape)


def _row_tile_spec(tm, K):
    return pl.BlockSpec((tm, K), lambda i, j: (i, 0))


def _col_groups(width):
    sub = min(MXU_COLS, width)
    return [slice(u * sub, (u + 1) * sub) for u in range(width // sub)]


def _proj_headnorm_kernel(x_ref, r_ref, w_ref, g_ref, o_ref, *, n_norm_tiles, n_tiles):
    def sub_dots():
        for cols in _col_groups(o_ref.shape[1]):
            y = jnp.dot(x_ref[...], w_ref[:, cols], preferred_element_type=_F32)
            yield cols, _rep(r_ref[...], y.shape[1]) * y

    def normed():
        for cols, y in sub_dots():
            for c in range(y.shape[1] // HEAD_DIM):
                yc = y[:, c * HEAD_DIM:(c + 1) * HEAD_DIM]
                sl = slice(cols.start + c * HEAD_DIM, cols.start + (c + 1) * HEAD_DIM)
                r = lax.rsqrt(jnp.mean(yc * yc, axis=-1, keepdims=True) + EPS)
                o_ref[:, sl] = (yc * r * g_ref[:, sl]).astype(o_ref.dtype)

    def plain():
        for cols, y in sub_dots():
            o_ref[:, cols] = y.astype(o_ref.dtype)

    if n_norm_tiles == n_tiles:
        normed()
    else:
        j = pl.program_id(1)
        pl.when(j < n_norm_tiles)(normed)
        pl.when(j >= n_norm_tiles)(plain)


def _proj_headnorm(xb, inv_rms, w, g_cols, n_norm_cols, tm, tn):
    M, K = xb.shape
    N = w.shape[1]
    kern = functools.partial(_proj_headnorm_kernel, n_norm_tiles=n_norm_cols // tn, n_tiles=N // tn)
    return pl.pallas_call(
        kern,
        grid=(M // tm, N // tn),
        in_specs=[_row_tile_spec(tm, K),
                  pl.BlockSpec((tm, LANES), lambda i, j: (i, 0)),
                  pl.BlockSpec((K, tn), lambda i, j: (0, j)),
                  pl.BlockSpec((1, tn), lambda i, j: (0, j))],
        out_specs=pl.BlockSpec((tm, tn), lambda i, j: (i, j)),
        out_shape=jax.ShapeDtypeStruct((M, N), _BF16),
        compiler_params=_params("arbitrary", "arbitrary"),
        name="proj_headnorm",
    )(xb, inv_rms, w, g_cols)


def _store_residual_tile(cols, y, o_ref, stat_refs, ssq):
    o_ref[:, cols] = y
    if not stat_refs:
        return None
    stat_refs[0][:, cols] = y.astype(_BF16)
    part = functools.reduce(jnp.add, _lane_tiles(y * y))
    return part if ssq is None else ssq + part


def _proj_residual_kernel(x_ref, w_ref, res_ref, o_ref, ob_ref, r_ref, *, width):
    j = pl.program_id(1)
    ssq = None
    for cols in _col_groups(o_ref.shape[1]):
        y = res_ref[:, cols] + jnp.dot(x_ref[...], w_ref[:, cols], preferred_element_type=_F32)
        ssq = _store_residual_tile(cols, y, o_ref, (ob_ref,), ssq)
    _accumulate_inv_rms(r_ref, ssq, j == 0, j == pl.num_programs(1) - 1, width)


def _proj_residual(x, w, res, tm, tn):
    M, K = x.shape
    N = w.shape[1]
    tile = pl.BlockSpec((tm, tn), lambda i, j: (i, j))
    return pl.pallas_call(
        functools.partial(_proj_residual_kernel, width=N),
        grid=(M // tm, N // tn),
        in_specs=[_row_tile_spec(tm, K),
                  pl.BlockSpec((K, tn), lambda i, j: (0, j)),
                  tile],
        out_specs=[tile, tile, pl.BlockSpec((tm, LANES), lambda i, j: (i, 0))],
        out_shape=[jax.ShapeDtypeStruct((M, N), _F32), jax.ShapeDtypeStruct((M, N), _BF16),
                   jax.ShapeDtypeStruct((M, LANES), _F32)],
        compiler_params=_params("arbitrary", "arbitrary"),
        name="proj_residual",
    )(x, w, res)


def _gate_up_kernel(x_ref, r_ref, wg_ref, wu_ref, gain_ref, *rest):
    o_ref = rest[-2] if len(rest) == 3 else rest[0]
    if len(rest) == 3:
        rest[2][...] = rest[0][...].astype(_BF16)
    x = x_ref[...]
    tn = o_ref.shape[1]
    gain = _rep(gain_ref[...], tn)
    r = _rep(r_ref[...], tn)
    g = r * jnp.dot(x, (wg_ref[...] * gain).astype(_BF16), preferred_element_type=_F32)
    u = r * jnp.dot(x, (wu_ref[...] * gain).astype(_BF16), preferred_element_type=_F32)
    o_ref[...] = (g * jax.nn.sigmoid(g) * u).astype(o_ref.dtype)


def _gate_up(xb, inv_rms, wg, wu, layer, gain, w_side, tm, tn):
    M, K = xb.shape
    N = wg.shape[2]
    nj = N // tn
    steps = (M // tm) * nj
    _, R, C = w_side.shape
    if R % steps or (R // steps) % BF16_SUBLANES:
        side_out = _cast_bf16(w_side, layer)
        side_args, side_in, side_spec, side_shape = [], [], [], []
    else:
        rows = R // steps
        side_args = [w_side]
        side_in = [pl.BlockSpec((None, rows, C), lambda i, j: (layer, i * nj + j, 0))]
        side_spec = [pl.BlockSpec((rows, C), lambda i, j: (i * nj + j, 0))]
        side_shape = [jax.ShapeDtypeStruct((R, C), _BF16)]
    w_spec = pl.BlockSpec((None, K, tn), lambda i, j: (layer, 0, j))
    outs = pl.pallas_call(
        _gate_up_kernel,
        grid=(M // tm, nj),
        in_specs=[_row_tile_spec(tm, K),
                  pl.BlockSpec((tm, LANES), lambda i, j: (i, 0)),
                  w_spec, w_spec,
                  pl.BlockSpec((K, LANES), lambda i, j: (0, 0))] + side_in,
        out_specs=[pl.BlockSpec((tm, tn), lambda i, j: (i, j))] + side_spec,
        out_shape=[jax.ShapeDtypeStruct((M, N), _BF16)] + side_shape,
        compiler_params=_params("arbitrary", "arbitrary"),
        name="ffn_gate_up",
    )(xb, inv_rms, wg, wu, _lane_rep(gain), *side_args)
    return (outs[0], outs[1]) if side_args else (outs[0], side_out)


def _down_kernel(x_ref, w_ref, res_ref, o_ref, *rest, width):
    *stat_refs, acc_ref = rest
    k = pl.program_id(1)
    j = pl.program_id(2)

    def parts():
        for cols in _col_groups(o_ref.shape[1]):
            yield cols, jnp.dot(x_ref[...], w_ref[:, cols], preferred_element_type=_F32)

    @pl.when(k == 0)
    def _():
        for cols, part in parts():
            acc_ref[j, :, cols] = part

    @pl.when(k == pl.num_programs(1) - 1)
    def _():
        ssq = None
        for cols, part in parts():
            y = res_ref[:, cols] + acc_ref[j, :, cols] + part
            ssq = _store_residual_tile(cols, y, o_ref, stat_refs, ssq)
        if stat_refs:
            _accumulate_inv_rms(stat_refs[1], ssq, j == 0, j == pl.num_programs(2) - 1, width)


def _down_residual(x, w, res, with_stats, tm, tk, tn):
    M, K = x.shape
    N = w.shape[1]
    nk = K // tk
    assert nk == 2
    tile = pl.BlockSpec((tm, tn), lambda i, k, j: (i, j * k))
    out_specs, out_shape = [tile], [jax.ShapeDtypeStruct((M, N), _F32)]
    if with_stats:
        out_specs += [tile, pl.BlockSpec((tm, LANES), lambda i, k, j: (i, 0))]
        out_shape += [jax.ShapeDtypeStruct((M, N), _BF16), jax.ShapeDtypeStruct((M, LANES), _F32)]
    return pl.pallas_call(
        functools.partial(_down_kernel, width=N),
        grid=(M // tm, nk, N // tn),
        in_specs=[pl.BlockSpec((tm, tk), lambda i, k, j: (i, k)),
                  pl.BlockSpec((tk, tn), lambda i, k, j: (k, j)),
                  tile],
        out_specs=out_specs,
        out_shape=out_shape,
        scratch_shapes=[pltpu.VMEM((N // tn, tm, tn), _F32)],
        compiler_params=_params("arbitrary", "arbitrary", "arbitrary"),
        name="ffn_down_residual",
    )(x, w, res)


def _diff_attn_kernel(q_ref, k_ref, v_ref, td_ref, ts_ref, lam_ref, gsub_ref, o_ref,
                      bias_ref, sx_ref, sy_ref, sz_ref, mx_ref, my_ref, mz_ref, m_ref, l_ref, acc_ref,
                      *, t, stage_blocks, lambda_init):
    i = pl.program_id(2)
    nt = t // LANES

    @pl.when(i == 0)
    def _build_bias():
        zeros = jnp.zeros((LANES, LANES), _F32)
        neg = jnp.full((LANES, LANES), NEG, _F32)
        for c in range(2):
            for a in range(nt):
                for b in range(nt):
                    rows = slice(a * LANES, (a + 1) * LANES)
                    cols = slice(b * LANES, (b + 1) * LANES)
                    if a == b:
                        tile = td_ref[c]
                    elif a == b + 1:
                        tile = ts_ref[c]
                    else:
                        tile = zeros if a > b else neg
                    bias_ref[c, 0, rows, cols] = tile
                    bias_ref[c, 1, rows, cols] = ts_ref[c] if (a == 0 and b == nt - 1) else zeros
                    bias_ref[c, 2, rows, cols] = neg

    def stage_a(blocks, s_out, m_out):
        for c in range(2):
            cols = slice(c * HEAD_DIM, (c + 1) * HEAD_DIM)
            part = None
            for n, (j, kind) in enumerate(blocks):
                start = pl.multiple_of(j * t, t)
                s = lax.dot_general(q_ref[:, cols], k_ref[pl.ds(start, t), cols], _NT,
                                    preferred_element_type=_F32)
                if kind is not None:
                    s = s + bias_ref[c, kind]
                s_out[c, :, n * t:(n + 1) * t] = s
                block_max = functools.reduce(jnp.maximum, _lane_tiles(s))
                part = block_max if part is None else jnp.maximum(part, block_max)
            m_out[c] = jnp.broadcast_to(jnp.max(part, axis=-1, keepdims=True), (t, LANES))

    def stage_b(blocks, s_in, m_in, first=False):
        for c in range(2):
            if first:
                m_new = m_in[c]
                psum = jnp.zeros((t, LANES), _F32)
                pv = jnp.zeros(acc_ref.shape[1:], _F32)
            else:
                m_old = m_ref[c]
                m_new = jnp.maximum(m_old, m_in[c])
                alpha = jnp.exp2(m_old - m_new)
                psum = alpha * l_ref[c]
                pv = _rep(alpha, acc_ref.shape[2]) * acc_ref[c]
            m_rep = _rep(m_new, MXU_COLS)
            for n, (j, _) in enumerate(blocks):
                for u in range(t // MXU_COLS):
                    off = u * MXU_COLS
                    p = jnp.exp2(s_in[c, :, n * t + off:n * t + off + MXU_COLS] - m_rep)
                    psum = psum + functools.reduce(jnp.add, _lane_tiles(p))
                    v_rows = v_ref[pl.ds(pl.multiple_of(j * t + off, MXU_COLS), MXU_COLS), :]
                    pv = pv + jnp.dot(p.astype(_BF16), v_rows, preferred_element_type=_F32)
            l_ref[c] = psum
            acc_ref[c] = pv
            m_ref[c] = m_new

    g = stage_blocks
    j_sub = jnp.maximum(i - 1, 0)
    n_far = jnp.maximum(i - 1, 0) // g
    left_over = jnp.maximum(i - 1, 0) % g
    near = [(i, 0), (j_sub, jnp.where(i == 0, 2, 1))] + [(jnp.maximum(i - 2 - n, 0), None) for n in range(g - 1)]

    def far(f):
        return [(g * f + n, None) for n in range(g)]

    for width in range(2, g + 2):
        is_width = left_over == width - 2

        @pl.when(is_width)
        def _(width=width):
            stage_a(near[:width], sz_ref, mz_ref)

        @pl.when(jnp.logical_and(is_width, n_far == 0))
        def _(width=width):
            stage_b(near[:width], sz_ref, mz_ref, first=True)

        @pl.when(jnp.logical_and(is_width, n_far > 0))
        def _(width=width):
            stage_a(far(0), sy_ref, my_ref)
            stage_b(near[:width], sz_ref, mz_ref, first=True)

    n_rest = jnp.maximum(n_far - 1, 0)
    n_pairs = n_rest // 2

    def pair_body(u, carry):
        stage_a(far(2 * u + 1), sx_ref, mx_ref)
        stage_b(far(2 * u), sy_ref, my_ref)
        stage_a(far(2 * u + 2), sy_ref, my_ref)
        stage_b(far(2 * u + 1), sx_ref, mx_ref)
        return carry

    lax.fori_loop(0, n_pairs, pair_body, 0)

    @pl.when(jnp.logical_and(n_far > 0, n_rest % 2 == 1))
    def _():
        stage_a(far(n_far - 1), sx_ref, mx_ref)
        stage_b(far(2 * n_pairs), sy_ref, my_ref)
        stage_b(far(n_far - 1), sx_ref, mx_ref)

    @pl.when(jnp.logical_and(n_far > 0, n_rest % 2 == 0))
    def _():
        stage_b(far(2 * n_pairs), sy_ref, my_ref)

    lf = lam_ref[...]
    lam = (jnp.exp(jnp.sum(lf[0:1] * lf[1:2], axis=-1, keepdims=True))
           - jnp.exp(jnp.sum(lf[2:3] * lf[3:4], axis=-1, keepdims=True)) + lambda_init)
    l0 = jnp.sum(l_ref[0], axis=-1, keepdims=True)
    l1 = jnp.sum(l_ref[1], axis=-1, keepdims=True)
    o = acc_ref[0] / l0 - lam * (acc_ref[1] / l1)
    o = o * lax.rsqrt(jnp.mean(o * o, axis=-1, keepdims=True) + EPS)
    o_ref[...] = (o * gsub_ref[...] * (1.0 - lambda_init)).astype(o_ref.dtype)


def _diff_attention(qkv, td, ts, lam_qk, g_sub, B, S, D, t, stage_blocks, lambda_init):
    n_heads = D // (2 * HEAD_DIM)
    nq = S // t
    w = 2 * HEAD_DIM
    kern = functools.partial(_diff_attn_kernel, t=t, stage_blocks=stage_blocks, lambda_init=lambda_init)
    return pl.pallas_call(
        kern,
        grid=(B, n_heads, nq),
        in_specs=[pl.BlockSpec((t, w), lambda b, h, i: (b * nq + i, h)),
                  pl.BlockSpec((S, w), lambda b, h, i: (b, n_heads + h)),
                  pl.BlockSpec((S, w), lambda b, h, i: (b, 2 * n_heads + h)),
                  pl.BlockSpec((2, LANES, LANES), lambda b, h, i: (h, 0, 0)),
                  pl.BlockSpec((2, LANES, LANES), lambda b, h, i: (h, 0, 0)),
                  pl.BlockSpec((4, HEAD_DIM), lambda b, h, i: (0, 0)),
                  pl.BlockSpec((1, w), lambda b, h, i: (0, 0))],
        out_specs=pl.BlockSpec((t, w), lambda b, h, i: (b * nq + i, h)),
        out_shape=jax.ShapeDtypeStruct((B * S, D), _BF16),
        scratch_shapes=[pltpu.VMEM((2, 3, t, t), _F32),
                        pltpu.VMEM((2, t, stage_blocks * t), _F32),
                        pltpu.VMEM((2, t, stage_blocks * t), _F32),
                        pltpu.VMEM((2, t, (stage_blocks + 1) * t), _F32),
                        pltpu.VMEM((2, t, LANES), _F32),
                        pltpu.VMEM((2, t, LANES), _F32),
                        pltpu.VMEM((2, t, LANES), _F32),
                        pltpu.VMEM((2, t, LANES), _F32),
                        pltpu.VMEM((2, t, LANES), _F32),
                        pltpu.VMEM((2, t, w), _F32)],
        compiler_params=_params("arbitrary", "arbitrary", "arbitrary"),
        name="diff_attention",
    )(qkv, qkv, qkv, td, ts, lam_qk, g_sub)


def _swa_kernel(q_ref, kc_ref, kp_ref, vc_ref, vp_ref, bias_ref, sink_ref, o_ref, *, tq):
    first = pl.program_id(2) == 0
    rows = SWA_GROUP * WINDOW
    bias = bias_ref[...].reshape(rows, 2 * WINDOW)
    sink = jnp.concatenate(
        [jnp.broadcast_to(sink_ref[r:r + 1, :], (WINDOW, LANES)) for r in range(SWA_GROUP)], axis=0)
    col = lax.broadcasted_iota(jnp.int32, (rows, 2 * WINDOW), 1)
    ones = jnp.ones((2 * WINDOW, LANES), _BF16)
    for c in range(tq // WINDOW):
        cur = slice(c * WINDOW, (c + 1) * WINDOW)
        if c == 0:
            k_prev, v_prev = kp_ref[...], vp_ref[...]
        else:
            prev = slice((c - 1) * WINDOW, c * WINDOW)
            k_prev, v_prev = kc_ref[prev, :], vc_ref[prev, :]
        kk = jnp.concatenate([k_prev, kc_ref[cur, :]], axis=0)
        vv = jnp.concatenate([jnp.concatenate([v_prev, vc_ref[cur, :]], axis=0), ones], axis=1)
        q4 = jnp.concatenate(
            [q_ref[cur, r * HEAD_DIM:(r + 1) * HEAD_DIM] for r in range(SWA_GROUP)], axis=0)
        s = lax.dot_general(q4, kk, _NT, preferred_element_type=_F32) + bias
        if c == 0:
            s = jnp.where(first & (col < WINDOW), NEG, s)
        part = functools.reduce(jnp.maximum, _lane_tiles(s))
        m = jnp.maximum(jnp.broadcast_to(jnp.max(part, axis=-1, keepdims=True), (rows, LANES)), sink)
        p = jnp.exp2(s - jnp.concatenate([m, m], axis=1))
        pv = jnp.dot(p.astype(_BF16), vv, preferred_element_type=_F32)
        o = pv[:, :HEAD_DIM] / (pv[:, HEAD_DIM:] + jnp.exp2(sink - m))
        for r in range(SWA_GROUP):
            o_ref[cur, r * HEAD_DIM:(r + 1) * HEAD_DIM] = (
                o[r * WINDOW:(r + 1) * WINDOW, :].astype(o_ref.dtype))


def _swa_attention(q, kv, bias, sinks, B, S, D, tq):
    n_kv = D // (SWA_GROUP * HEAD_DIM)
    nt = S // tq
    bpt = tq // WINDOW
    gw = SWA_GROUP * HEAD_DIM

    def prev_map(off):
        return lambda b, g, t: (jnp.maximum((b * nt + t) * bpt - 1, 0), off + g)

    kern = functools.partial(_swa_kernel, tq=tq)
    return pl.pallas_call(
        kern,
        grid=(B, n_kv, nt),
        in_specs=[pl.BlockSpec((tq, gw), lambda b, g, t: (b * nt + t, g)),
                  pl.BlockSpec((tq, HEAD_DIM), lambda b, g, t: (b * nt + t, g)),
                  pl.BlockSpec((WINDOW, HEAD_DIM), prev_map(0)),
                  pl.BlockSpec((tq, HEAD_DIM), lambda b, g, t: (b * nt + t, n_kv + g)),
                  pl.BlockSpec((WINDOW, HEAD_DIM), prev_map(n_kv)),
                  pl.BlockSpec((SWA_GROUP, WINDOW, 2 * WINDOW), lambda b, g, t: (g, 0, 0)),
                  pl.BlockSpec((None, SWA_GROUP, LANES), lambda b, g, t: (g, 0, 0))],
        out_specs=pl.BlockSpec((tq, gw), lambda b, g, t: (b * nt + t, g)),
        out_shape=jax.ShapeDtypeStruct((B * S, D), _BF16),
        compiler_params=_params("arbitrary", "arbitrary", "arbitrary"),
        name="swa_attention",
    )(q, kv, kv, kv, kv, bias, sinks)


def _t5_bucket(n):
    max_exact = NUM_BUCKETS // 2
    nf = jnp.maximum(n, 1).astype(_F32)
    large = max_exact + (jnp.log(nf / max_exact) / math.log(MAX_DISTANCE / max_exact)
                         * (NUM_BUCKETS - max_exact)).astype(jnp.int32)
    large = jnp.minimum(large, NUM_BUCKETS - 1)
    return jnp.where(n < max_exact, n, large)


def _bias_tiles(rel_bias):
    def lookup(table, idx):
        onehot = (idx[..., None] == jnp.arange(table.shape[0])).astype(_F32)
        return jnp.einsum("...d,dh->h...", onehot, table, precision=lax.Precision.HIGHEST)

    lut = lookup(rel_bias.astype(_F32), _t5_bucket(jnp.arange(MAX_DISTANCE))).T * LOG2E
    shifted = lut - lut[MAX_DISTANCE - 1]
    qi = jnp.arange(LANES)[:, None]
    kj = jnp.arange(LANES)[None, :]
    clip = lambda dist: jnp.clip(dist, 0, MAX_DISTANCE - 1)
    d0 = qi - kj
    td = jnp.where((d0 >= 0)[None], lookup(shifted, clip(d0)), NEG)
    ts = lookup(shifted, clip(LANES + qi - kj))
    dw = WINDOW + qi - jnp.arange(2 * WINDOW)[None, :]
    swa = jnp.where(((dw >= 0) & (dw < WINDOW))[None], lookup(lut, clip(dw)), NEG)
    return td, ts, swa


def _forward(x, rel_bias, g_attn_norm, g_ffn_norm, w_qkv_a, w_o_a, g_q_a, g_k_a, lam_qk_a, g_sub_a,
             g_kv_norm, w_kv, g_k_shared, w_q_b, w_o_b, g_q_b, sinks_b, w_gate, w_up, w_down, tiles):
    B, S, D = x.shape
    M = B * S
    T = tiles
    q_scale = HEAD_DIM ** -0.5 * LOG2E
    n_maps = D // HEAD_DIM
    kvw = w_kv.shape[1] // 2
    td, ts, swa_bias = _bias_tiles(rel_bias)
    bf = _cast_bf16
    x = x.reshape(M, D)

    def ffn(x, xb, inv_rms, layer, with_stats):
        act, w_down_bf16 = _gate_up(xb, inv_rms, w_gate, w_up, layer, g_ffn_norm[layer], w_down,
                                    T["mm_tm"], T["gu_tn"])
        return _down_residual(act, w_down_bf16, x, with_stats, T["down_tm"], T["down_tk"], T["down_tn"])

    lambda_init = 0.8 - 0.6 * math.exp(-0.3 * 0)
    xb, inv_rms = _prep(x, T["rms_tm"])
    g_cols = jnp.concatenate([jnp.tile(g_q_a[0] * q_scale, n_maps), jnp.tile(g_k_a[0], n_maps),
                              jnp.ones((D,), _F32)])[None]
    qkv = _proj_headnorm(xb, inv_rms, bf(w_qkv_a, 0, g_attn_norm[0]), g_cols, 2 * D, T["mm_tm"], T["qkv_tn"])
    o = _diff_attention(qkv, td, ts, lam_qk_a[0], g_sub_a[0][None], B, S, D, T["attn_t"], T["attn_stage_blocks"],
                        lambda_init)
    x, xb, inv_rms = _proj_residual(o, bf(w_o_a, 0), x, T["mm_tm"], T["res_tn"])
    x, xb, inv_rms = ffn(x, xb, inv_rms, 0, True)

    g_kv_cols = jnp.concatenate([jnp.tile(g_k_shared, kvw // HEAD_DIM), jnp.ones((kvw,), _F32)])[None]
    kv = _proj_headnorm(xb, inv_rms, bf(w_kv[None], 0, g_kv_norm), g_kv_cols, kvw,
                        T["mm_tm"], min(T["qkv_tn"], kvw))
    q = _proj_headnorm(xb, inv_rms, bf(w_q_b, 0, g_attn_norm[1]), jnp.tile(g_q_b[0] * q_scale, n_maps)[None],
                       D, T["mm_tm"], T["qkv_tn"])
    sinks = jnp.broadcast_to((sinks_b[0].astype(_F32) * LOG2E).reshape(-1, SWA_GROUP, 1),
                             (n_maps // SWA_GROUP, SWA_GROUP, LANES))
    o = _swa_attention(q, kv, swa_bias, sinks, B, S, D, T["swa_tq"])
    x, xb, inv_rms = _proj_residual(o, bf(w_o_b, 0), x, T["mm_tm"], T["res_tn"])
    (x,) = ffn(x, xb, inv_rms, 1, False)
    return x.reshape(B, S, D)


def kernel(x, rel_bias, g_attn_norm, g_ffn_norm, w_qkv_a, w_o_a, g_q_a, g_k_a, lam_qk_a, g_sub_a,
           g_kv_norm, w_kv, g_k_shared, w_q_b, w_o_b, g_q_b, sinks_b, w_gate, w_up, w_down):
    B, S, D = x.shape
    tiles = _tiles(B * S, S, D, w_gate.shape[-1])
    return _forward(x, rel_bias, g_attn_norm, g_ffn_norm, w_qkv_a, w_o_a, g_q_a, g_k_a, lam_qk_a,
                    g_sub_a, g_kv_norm, w_kv, g_k_shared, w_q_b, w_o_b, g_q_b, sinks_b,
                    w_gate, w_up, w_down, tiles)
```

```python
import functools
import math

import jax
import jax.numpy as jnp
from jax import lax
from jax.experimental import pallas as pl
from jax.experimental.pallas import tpu as pltpu

HEAD_DIM = 128
WINDOW = 128
SWA_GROUP = 4
NUM_BUCKETS = 32
MAX_DISTANCE = 128
EPS = 1e-6
NEG = -1e30
LOG2E = math.log2(math.e)
LANES = 128
VMEM_LIMIT_BYTES = 56 * 1024 * 1024
CAST_BLOCK_BYTES = 8 * 1024 * 1024
MXU_COLS = 256
BF16_SUBLANES = 16

_F32 = jnp.float32
_BF16 = jnp.bfloat16
_NT = (((1,), (1,)), ((), ()))


def _params(*sem):
    return pltpu.CompilerParams(dimension_semantics=sem, vmem_limit_bytes=VMEM_LIMIT_BYTES)


def _tiles(M, S, D, D_FF):
    return dict(
        rms_tm=min(256, M),
        mm_tm=min(1024, M),
        qkv_tn=min(1024, D),
        res_tn=min(512, D),
        gu_tn=min(256, D_FF),
        down_tm=min(512, M),
        down_tk=D_FF // 2,
        down_tn=min(1024, D),
        attn_t=min(512, S // 4),
        attn_stage_blocks=3,
        swa_tq=min(1024, S),
    )


def _lane_tiles(a):
    return [a[:, u * LANES:(u + 1) * LANES] for u in range(a.shape[1] // LANES)]


def _rep(a, width):
    return a if width == LANES else jnp.concatenate([a] * (width // LANES), axis=1)


def _lane_rep(v):
    return jnp.broadcast_to(v.astype(_F32)[:, None], (v.shape[0], LANES))


def _cast_kernel(w_ref, *rest):
    o_ref = rest[-1]
    w = w_ref[...]
    if len(rest) == 2:
        w = w * _rep(rest[0][...], w.shape[1])
    o_ref[...] = w.astype(o_ref.dtype)


def _cast_bf16(w, layer, row_gain=None):
    _, R, C = w.shape
    tr = min(R, max(8, CAST_BLOCK_BYTES // (4 * C) // 8 * 8))
    while R % tr:
        tr -= 8
    args, specs = [w], [pl.BlockSpec((None, tr, C), lambda i: (layer, i, 0))]
    if row_gain is not None:
        args.append(_lane_rep(row_gain))
        specs.append(pl.BlockSpec((tr, LANES), lambda i: (i, 0)))
    return pl.pallas_call(
        _cast_kernel,
        grid=(R // tr,),
        in_specs=specs,
        out_specs=pl.BlockSpec((tr, C), lambda i: (i, 0)),
        out_shape=jax.ShapeDtypeStruct((R, C), _BF16),
        compiler_params=_params("arbitrary"),
        name="cast_bf16",
    )(*args)


class _SideCast:
    def __init__(self, w, layer, row_gain, steps, step_index):
        _, R, C = w.shape
        f = 1
        while (R * f) % steps or ((R * f) // steps) % BF16_SUBLANES:
            f *= 2
            assert C % (f * LANES) == 0, (R, C, steps)
        rows, cols = R * f // steps, C // f
        self.shape = (R, C)
        self.has_gain = row_gain is not None
        self.args = [w.reshape(w.shape[0], R * f, cols)]
        self.in_specs = [pl.BlockSpec((None, rows, cols), lambda *ids: (layer, step_index(*ids), 0))]
        if self.has_gain:
            self.args.append(_lane_rep(jnp.repeat(row_gain, f)))
            self.in_specs.append(pl.BlockSpec((rows, LANES), lambda *ids: (step_index(*ids), 0)))
        self.out_spec = pl.BlockSpec((rows, cols), lambda *ids: (step_index(*ids), 0))
        self.out_shape = jax.ShapeDtypeStruct((R * f, cols), _BF16)


def _run_side_casts(jobs_have_gain, in_refs, out_refs):
    pos = 0
    for has_gain, o_ref in zip(jobs_have_gain, out_refs):
        n = 2 if has_gain else 1
        _cast_kernel(*in_refs[pos:pos + n], o_ref)
        pos += n


def _prep_kernel(x_ref, xb_ref, r_ref):
    x = x_ref[...]
    xb_ref[...] = x.astype(xb_ref.dtype)
    r_ref[...] = jnp.broadcast_to(lax.rsqrt(jnp.mean(x * x, axis=-1, keepdims=True) + EPS), r_ref.shape)


def _prep(x, tm):
    M, D = x.shape
    return pl.pallas_call(
        _prep_kernel,
        grid=(M // tm,),
        in_specs=[pl.BlockSpec((tm, D), lambda i: (i, 0))],
        out_specs=[pl.BlockSpec((tm, D), lambda i: (i, 0)), pl.BlockSpec((tm, LANES), lambda i: (i, 0))],
        out_shape=[jax.ShapeDtypeStruct((M, D), _BF16), jax.ShapeDtypeStruct((M, LANES), _F32)],
        compiler_params=_params("arbitrary"),
        name="rms_stats",
    )(x)


def _accumulate_inv_rms(r_ref, ssq, first, last, width):
    @pl.when(first)
    def _():
        r_ref[...] = ssq

    @pl.when(jnp.logical_not(first))
    def _():
        r_ref[...] += ssq

    @pl.when(last)
    def _():
        total = jnp.sum(r_ref[...], axis=-1, keepdims=True)
        r_ref[...] = jnp.broadcast_to(lax.rsqrt(total / width + EPS), r_ref.shape)


def _row_tile_spec(tm, K):
    return pl.BlockSpec((tm, K), lambda i, j: (i, 0))


def _col_groups(width):
    sub = min(MXU_COLS, width)
    return [slice(u * sub, (u + 1) * sub) for u in range(width // sub)]


def _proj_headnorm_kernel(x_ref, r_ref, w_ref, g_ref, o_ref, *, n_norm_tiles, n_tiles):
    def sub_dots():
        for cols in _col_groups(o_ref.shape[1]):
            y = jnp.dot(x_ref[...], w_ref[:, cols], preferred_element_type=_F32)
            yield cols, _rep(r_ref[...], y.shape[1]) * y

    def normed():
        for cols, y in sub_dots():
            for c in range(y.shape[1] // HEAD_DIM):
                yc = y[:, c * HEAD_DIM:(c + 1) * HEAD_DIM]
                sl = slice(cols.start + c * HEAD_DIM, cols.start + (c + 1) * HEAD_DIM)
                r = lax.rsqrt(jnp.mean(yc * yc, axis=-1, keepdims=True) + EPS)
                o_ref[:, sl] = (yc * r * g_ref[:, sl]).astype(o_ref.dtype)

    def plain():
        for cols, y in sub_dots():
            o_ref[:, cols] = y.astype(o_ref.dtype)

    if n_norm_tiles == n_tiles:
        normed()
    else:
        j = pl.program_id(1)
        pl.when(j < n_norm_tiles)(normed)
        pl.when(j >= n_norm_tiles)(plain)


def _proj_headnorm(xb, inv_rms, w, g_cols, n_norm_cols, tm, tn):
    M, K = xb.shape
    N = w.shape[1]
    kern = functools.partial(_proj_headnorm_kernel, n_norm_tiles=n_norm_cols // tn, n_tiles=N // tn)
    return pl.pallas_call(
        kern,
        grid=(M // tm, N // tn),
        in_specs=[_row_tile_spec(tm, K),
                  pl.BlockSpec((tm, LANES), lambda i, j: (i, 0)),
                  pl.BlockSpec((K, tn), lambda i, j: (0, j)),
                  pl.BlockSpec((1, tn), lambda i, j: (0, j))],
        out_specs=pl.BlockSpec((tm, tn), lambda i, j: (i, j)),
        out_shape=jax.ShapeDtypeStruct((M, N), _BF16),
        compiler_params=_params("arbitrary", "arbitrary"),
        name="proj_headnorm",
    )(xb, inv_rms, w, g_cols)


def _store_residual_tile(cols, y, o_ref, stat_refs, ssq):
    o_ref[:, cols] = y
    if not stat_refs:
        return None
    stat_refs[0][:, cols] = y.astype(_BF16)
    part = functools.reduce(jnp.add, _lane_tiles(y * y))
    return part if ssq is None else ssq + part


def _proj_residual_kernel(x_ref, w_ref, res_ref, o_ref, ob_ref, r_ref, *, width):
    j = pl.program_id(1)
    ssq = None
    for cols in _col_groups(o_ref.shape[1]):
        y = res_ref[:, cols] + jnp.dot(x_ref[...], w_ref[:, cols], preferred_element_type=_F32)
        ssq = _store_residual_tile(cols, y, o_ref, (ob_ref,), ssq)
    _accumulate_inv_rms(r_ref, ssq, j == 0, j == pl.num_programs(1) - 1, width)


def _proj_residual(x, w, res, tm, tn):
    M, K = x.shape
    N = w.shape[1]
    tile = pl.BlockSpec((tm, tn), lambda i, j: (i, j))
    return pl.pallas_call(
        functools.partial(_proj_residual_kernel, width=N),
        grid=(M // tm, N // tn),
        in_specs=[_row_tile_spec(tm, K),
                  pl.BlockSpec((K, tn), lambda i, j: (0, j)),
                  tile],
        out_specs=[tile, tile, pl.BlockSpec((tm, LANES), lambda i, j: (i, 0))],
        out_shape=[jax.ShapeDtypeStruct((M, N), _F32), jax.ShapeDtypeStruct((M, N), _BF16),
                   jax.ShapeDtypeStruct((M, LANES), _F32)],
        compiler_params=_params("arbitrary", "arbitrary"),
        name="proj_residual",
    )(x, w, res)


def _gate_up_kernel(x_ref, r_ref, wg_ref, wu_ref, gain_ref, *rest):
    o_ref = rest[-2] if len(rest) == 3 else rest[0]
    if len(rest) == 3:
        rest[2][...] = rest[0][...].astype(_BF16)
    x = x_ref[...]
    tn = o_ref.shape[1]
    gain = _rep(gain_ref[...], tn)
    r = _rep(r_ref[...], tn)
    g = r * jnp.dot(x, (wg_ref[...] * gain).astype(_BF16), preferred_element_type=_F32)
    u = r * jnp.dot(x, (wu_ref[...] * gain).astype(_BF16), preferred_element_type=_F32)
    o_ref[...] = (g * jax.nn.sigmoid(g) * u).astype(o_ref.dtype)


def _gate_up(xb, inv_rms, wg, wu, layer, gain, w_side, tm, tn):
    M, K = xb.shape
    N = wg.shape[2]
    nj = N // tn
    steps = (M // tm) * nj
    _, R, C = w_side.shape
    if R % steps or (R // steps) % BF16_SUBLANES:
        side_out = _cast_bf16(w_side, layer)
        side_args, side_in, side_spec, side_shape = [], [], [], []
    else:
        rows = R // steps
        side_args = [w_side]
        side_in = [pl.BlockSpec((None, rows, C), lambda i, j: (layer, i * nj + j, 0))]
        side_spec = [pl.BlockSpec((rows, C), lambda i, j: (i * nj + j, 0))]
        side_shape = [jax.ShapeDtypeStruct((R, C), _BF16)]
    w_spec = pl.BlockSpec((None, K, tn), lambda i, j: (layer, 0, j))
    outs = pl.pallas_call(
        _gate_up_kernel,
        grid=(M // tm, nj),
        in_specs=[_row_tile_spec(tm, K),
                  pl.BlockSpec((tm, LANES), lambda i, j: (i, 0)),
                  w_spec, w_spec,
                  pl.BlockSpec((K, LANES), lambda i, j: (0, 0))] + side_in,
        out_specs=[pl.BlockSpec((tm, tn), lambda i, j: (i, j))] + side_spec,
        out_shape=[jax.ShapeDtypeStruct((M, N), _BF16)] + side_shape,
        compiler_params=_params("arbitrary", "arbitrary"),
        name="ffn_gate_up",
    )(xb, inv_rms, wg, wu, _lane_rep(gain), *side_args)
    return (outs[0], outs[1]) if side_args else (outs[0], side_out)


def _down_kernel(x_ref, w_ref, res_ref, o_ref, *rest, width):
    *stat_refs, acc_ref = rest
    k = pl.program_id(1)
    j = pl.program_id(2)

    def parts():
        for cols in _col_groups(o_ref.shape[1]):
            yield cols, jnp.dot(x_ref[...], w_ref[:, cols], preferred_element_type=_F32)

    @pl.when(k == 0)
    def _():
        for cols, part in parts():
            acc_ref[j, :, cols] = part

    @pl.when(k == pl.num_programs(1) - 1)
    def _():
        ssq = None
        for cols, part in parts():
            y = res_ref[:, cols] + acc_ref[j, :, cols] + part
            ssq = _store_residual_tile(cols, y, o_ref, stat_refs, ssq)
        if stat_refs:
            _accumulate_inv_rms(stat_refs[1], ssq, j == 0, j == pl.num_programs(2) - 1, width)


def _down_residual(x, w, res, with_stats, tm, tk, tn):
    M, K = x.shape
    N = w.shape[1]
    nk = K // tk
    assert nk == 2
    tile = pl.BlockSpec((tm, tn), lambda i, k, j: (i, j * k))
    out_specs, out_shape = [tile], [jax.ShapeDtypeStruct((M, N), _F32)]
    if with_stats:
        out_specs += [tile, pl.BlockSpec((tm, LANES), lambda i, k, j: (i, 0))]
        out_shape += [jax.ShapeDtypeStruct((M, N), _BF16), jax.ShapeDtypeStruct((M, LANES), _F32)]
    return pl.pallas_call(
        functools.partial(_down_kernel, width=N),
        grid=(M // tm, nk, N // tn),
        in_specs=[pl.BlockSpec((tm, tk), lambda i, k, j: (i, k)),
                  pl.BlockSpec((tk, tn), lambda i, k, j: (k, j)),
                  tile],
        out_specs=out_specs,
        out_shape=out_shape,
        scratch_shapes=[pltpu.VMEM((N // tn, tm, tn), _F32)],
        compiler_params=_params("arbitrary", "arbitrary", "arbitrary"),
        name="ffn_down_residual",
    )(x, w, res)


def _diff_attn_kernel(q_ref, k_ref, v_ref, td_ref, ts_ref, lam_ref, gsub_ref, *refs,
                      t, stage_blocks, lambda_init, side_gains):
    n_side_in = sum(2 if has_gain else 1 for has_gain in side_gains)
    side_in, o_ref = refs[:n_side_in], refs[n_side_in]
    side_out = refs[n_side_in + 1:n_side_in + 1 + len(side_gains)]
    bias_ref, sx_ref, sy_ref, sz_ref, mx_ref, my_ref, mz_ref, m_ref, l_ref, acc_ref = (
        refs[n_side_in + 1 + len(side_gains):])
    _attention_step(q_ref, k_ref, v_ref, td_ref, ts_ref, lam_ref, gsub_ref, o_ref,
                    bias_ref, sx_ref, sy_ref, sz_ref, mx_ref, my_ref, mz_ref, m_ref, l_ref, acc_ref,
                    t=t, stage_blocks=stage_blocks, lambda_init=lambda_init)
    _run_side_casts(side_gains, side_in, side_out)


def _attention_step(q_ref, k_ref, v_ref, td_ref, ts_ref, lam_ref, gsub_ref, o_ref,
                    bias_ref, sx_ref, sy_ref, sz_ref, mx_ref, my_ref, mz_ref, m_ref, l_ref, acc_ref,
                    *, t, stage_blocks, lambda_init):
    i = pl.program_id(2)
    nt = t // LANES

    @pl.when(i == 0)
    def _build_bias():
        zeros = jnp.zeros((LANES, LANES), _F32)
        neg = jnp.full((LANES, LANES), NEG, _F32)
        for c in range(2):
            for a in range(nt):
                for b in range(nt):
                    rows = slice(a * LANES, (a + 1) * LANES)
                    cols = slice(b * LANES, (b + 1) * LANES)
                    if a == b:
                        tile = td_ref[c]
                    elif a == b + 1:
                        tile = ts_ref[c]
                    else:
                        tile = zeros if a > b else neg
                    bias_ref[c, 0, rows, cols] = tile
                    bias_ref[c, 1, rows, cols] = ts_ref[c] if (a == 0 and b == nt - 1) else zeros
                    bias_ref[c, 2, rows, cols] = neg

    def stage_a(blocks, s_out, m_out):
        for c in range(2):
            cols = slice(c * HEAD_DIM, (c + 1) * HEAD_DIM)
            part = None
            for n, (j, kind) in enumerate(blocks):
                start = pl.multiple_of(j * t, t)
                s = lax.dot_general(q_ref[:, cols], k_ref[pl.ds(start, t), cols], _NT,
                                    preferred_element_type=_F32)
                if kind is not None:
                    s = s + bias_ref[c, kind]
                s_out[c, :, n * t:(n + 1) * t] = s
                block_max = functools.reduce(jnp.maximum, _lane_tiles(s))
                part = block_max if part is None else jnp.maximum(part, block_max)
            m_out[c] = jnp.broadcast_to(jnp.max(part, axis=-1, keepdims=True), (t, LANES))

    def stage_b(blocks, s_in, m_in, first=False):
        for c in range(2):
            if first:
                m_new = m_in[c]
                psum = jnp.zeros((t, LANES), _F32)
                pv = jnp.zeros(acc_ref.shape[1:], _F32)
            else:
                m_old = m_ref[c]
                m_new = jnp.maximum(m_old, m_in[c])
                alpha = jnp.exp2(m_old - m_new)
                psum = alpha * l_ref[c]
                pv = _rep(alpha, acc_ref.shape[2]) * acc_ref[c]
            m_rep = _rep(m_new, MXU_COLS)
            for n, (j, _) in enumerate(blocks):
                for u in range(t // MXU_COLS):
                    off = u * MXU_COLS
                    p = jnp.exp2(s_in[c, :, n * t + off:n * t + off + MXU_COLS] - m_rep)
                    psum = psum + functools.reduce(jnp.add, _lane_tiles(p))
                    v_rows = v_ref[pl.ds(pl.multiple_of(j * t + off, MXU_COLS), MXU_COLS), :]
                    pv = pv + jnp.dot(p.astype(_BF16), v_rows, preferred_element_type=_F32)
            l_ref[c] = psum
            acc_ref[c] = pv
            m_ref[c] = m_new

    g = stage_blocks
    j_sub = jnp.maximum(i - 1, 0)
    n_far = jnp.maximum(i - 1, 0) // g
    left_over = jnp.maximum(i - 1, 0) % g
    near = [(i, 0), (j_sub, jnp.where(i == 0, 2, 1))] + [(jnp.maximum(i - 2 - n, 0), None) for n in range(g - 1)]

    def far(f):
        return [(g * f + n, None) for n in range(g)]

    for width in range(2, g + 2):
        is_width = left_over == width - 2

        @pl.when(is_width)
        def _(width=width):
            stage_a(near[:width], sz_ref, mz_ref)

        @pl.when(jnp.logical_and(is_width, n_far == 0))
        def _(width=width):
            stage_b(near[:width], sz_ref, mz_ref, first=True)

        @pl.when(jnp.logical_and(is_width, n_far > 0))
        def _(width=width):
            stage_a(far(0), sy_ref, my_ref)
            stage_b(near[:width], sz_ref, mz_ref, first=True)

    n_rest = jnp.maximum(n_far - 1, 0)
    n_pairs = n_rest // 2

    def pair_body(u, carry):
        stage_a(far(2 * u + 1), sx_ref, mx_ref)
        stage_b(far(2 * u), sy_ref, my_ref)
        stage_a(far(2 * u + 2), sy_ref, my_ref)
        stage_b(far(2 * u + 1), sx_ref, mx_ref)
        return carry

    lax.fori_loop(0, n_pairs, pair_body, 0)

    @pl.when(jnp.logical_and(n_far > 0, n_rest % 2 == 1))
    def _():
        stage_a(far(n_far - 1), sx_ref, mx_ref)
        stage_b(far(2 * n_pairs), sy_ref, my_ref)
        stage_b(far(n_far - 1), sx_ref, mx_ref)

    @pl.when(jnp.logical_and(n_far > 0, n_rest % 2 == 0))
    def _():
        stage_b(far(2 * n_pairs), sy_ref, my_ref)

    lf = lam_ref[...]
    lam = (jnp.exp(jnp.sum(lf[0:1] * lf[1:2], axis=-1, keepdims=True))
           - jnp.exp(jnp.sum(lf[2:3] * lf[3:4], axis=-1, keepdims=True)) + lambda_init)
    l0 = jnp.sum(l_ref[0], axis=-1, keepdims=True)
    l1 = jnp.sum(l_ref[1], axis=-1, keepdims=True)
    o = acc_ref[0] / l0 - lam * (acc_ref[1] / l1)
    o = o * lax.rsqrt(jnp.mean(o * o, axis=-1, keepdims=True) + EPS)
    o_ref[...] = (o * gsub_ref[...] * (1.0 - lambda_init)).astype(o_ref.dtype)


def _diff_attention(qkv, td, ts, lam_qk, g_sub, side_weights, B, S, D, t, stage_blocks, lambda_init):
    n_heads = D // (2 * HEAD_DIM)
    nq = S // t
    w = 2 * HEAD_DIM
    jobs = [_SideCast(sw, layer, gain, B * n_heads * nq, lambda b, h, i: (b * n_heads + h) * nq + i)
            for sw, layer, gain in side_weights]
    kern = functools.partial(_diff_attn_kernel, t=t, stage_blocks=stage_blocks, lambda_init=lambda_init,
                             side_gains=tuple(job.has_gain for job in jobs))
    outs = pl.pallas_call(
        kern,
        grid=(B, n_heads, nq),
        in_specs=[pl.BlockSpec((t, w), lambda b, h, i: (b * nq + i, h)),
                  pl.BlockSpec((S, w), lambda b, h, i: (b, n_heads + h)),
                  pl.BlockSpec((S, w), lambda b, h, i: (b, 2 * n_heads + h)),
                  pl.BlockSpec((2, LANES, LANES), lambda b, h, i: (h, 0, 0)),
                  pl.BlockSpec((2, LANES, LANES), lambda b, h, i: (h, 0, 0)),
                  pl.BlockSpec((4, HEAD_DIM), lambda b, h, i: (0, 0)),
                  pl.BlockSpec((1, w), lambda b, h, i: (0, 0))] + [s for job in jobs for s in job.in_specs],
        out_specs=[pl.BlockSpec((t, w), lambda b, h, i: (b * nq + i, h))] + [job.out_spec for job in jobs],
        out_shape=[jax.ShapeDtypeStruct((B * S, D), _BF16)] + [job.out_shape for job in jobs],
        scratch_shapes=[pltpu.VMEM((2, 3, t, t), _F32),
                        pltpu.VMEM((2, t, stage_blocks * t), _F32),
                        pltpu.VMEM((2, t, stage_blocks * t), _F32),
                        pltpu.VMEM((2, t, (stage_blocks + 1) * t), _F32),
                        pltpu.VMEM((2, t, LANES), _F32),
                        pltpu.VMEM((2, t, LANES), _F32),
                        pltpu.VMEM((2, t, LANES), _F32),
                        pltpu.VMEM((2, t, LANES), _F32),
                        pltpu.VMEM((2, t, LANES), _F32),
                        pltpu.VMEM((2, t, w), _F32)],
        compiler_params=_params("arbitrary", "arbitrary", "arbitrary"),
        name="diff_attention",
    )(qkv, qkv, qkv, td, ts, lam_qk, g_sub, *[a for job in jobs for a in job.args])
    return outs[0], [o.reshape(job.shape) for o, job in zip(outs[1:], jobs)]


def _swa_kernel(q_ref, kc_ref, kp_ref, vc_ref, vp_ref, bias_ref, sink_ref, o_ref, *, tq):
    first = pl.program_id(2) == 0
    rows = SWA_GROUP * WINDOW
    bias = bias_ref[...].reshape(rows, 2 * WINDOW)
    sink = jnp.concatenate(
        [jnp.broadcast_to(sink_ref[r:r + 1, :], (WINDOW, LANES)) for r in range(SWA_GROUP)], axis=0)
    col = lax.broadcasted_iota(jnp.int32, (rows, 2 * WINDOW), 1)
    ones = jnp.ones((2 * WINDOW, LANES), _BF16)
    for c in range(tq // WINDOW):
        cur = slice(c * WINDOW, (c + 1) * WINDOW)
        if c == 0:
            k_prev, v_prev = kp_ref[...], vp_ref[...]
        else:
            prev = slice((c - 1) * WINDOW, c * WINDOW)
            k_prev, v_prev = kc_ref[prev, :], vc_ref[prev, :]
        kk = jnp.concatenate([k_prev, kc_ref[cur, :]], axis=0)
        vv = jnp.concatenate([jnp.concatenate([v_prev, vc_ref[cur, :]], axis=0), ones], axis=1)
        q4 = jnp.concatenate(
            [q_ref[cur, r * HEAD_DIM:(r + 1) * HEAD_DIM] for r in range(SWA_GROUP)], axis=0)
        s = lax.dot_general(q4, kk, _NT, preferred_element_type=_F32) + bias
        if c == 0:
            s = jnp.where(first & (col < WINDOW), NEG, s)
        part = functools.reduce(jnp.maximum, _lane_tiles(s))
        m = jnp.maximum(jnp.broadcast_to(jnp.max(part, axis=-1, keepdims=True), (rows, LANES)), sink)
        p = jnp.exp2(s - jnp.concatenate([m, m], axis=1))
        pv = jnp.dot(p.astype(_BF16), vv, preferred_element_type=_F32)
        o = pv[:, :HEAD_DIM] / (pv[:, HEAD_DIM:] + jnp.exp2(sink - m))
        for r in range(SWA_GROUP):
            o_ref[cur, r * HEAD_DIM:(r + 1) * HEAD_DIM] = (
                o[r * WINDOW:(r + 1) * WINDOW, :].astype(o_ref.dtype))


def _swa_attention(q, kv, bias, sinks, B, S, D, tq):
    n_kv = D // (SWA_GROUP * HEAD_DIM)
    nt = S // tq
    bpt = tq // WINDOW
    gw = SWA_GROUP * HEAD_DIM

    def prev_map(off):
        return lambda b, g, t: (jnp.maximum((b * nt + t) * bpt - 1, 0), off + g)

    kern = functools.partial(_swa_kernel, tq=tq)
    return pl.pallas_call(
        kern,
        grid=(B, n_kv, nt),
        in_specs=[pl.BlockSpec((tq, gw), lambda b, g, t: (b * nt + t, g)),
                  pl.BlockSpec((tq, HEAD_DIM), lambda b, g, t: (b * nt + t, g)),
                  pl.BlockSpec((WINDOW, HEAD_DIM), prev_map(0)),
                  pl.BlockSpec((tq, HEAD_DIM), lambda b, g, t: (b * nt + t, n_kv + g)),
                  pl.BlockSpec((WINDOW, HEAD_DIM), prev_map(n_kv)),
                  pl.BlockSpec((SWA_GROUP, WINDOW, 2 * WINDOW), lambda b, g, t: (g, 0, 0)),
                  pl.BlockSpec((None, SWA_GROUP, LANES), lambda b, g, t: (g, 0, 0))],
        out_specs=pl.BlockSpec((tq, gw), lambda b, g, t: (b * nt + t, g)),
        out_shape=jax.ShapeDtypeStruct((B * S, D), _BF16),
        compiler_params=_params("arbitrary", "arbitrary", "arbitrary"),
        name="swa_attention",
    )(q, kv, kv, kv, kv, bias, sinks)


def _t5_bucket(n):
    max_exact = NUM_BUCKETS // 2
    nf = jnp.maximum(n, 1).astype(_F32)
    large = max_exact + (jnp.log(nf / max_exact) / math.log(MAX_DISTANCE / max_exact)
                         * (NUM_BUCKETS - max_exact)).astype(jnp.int32)
    large = jnp.minimum(large, NUM_BUCKETS - 1)
    return jnp.where(n < max_exact, n, large)


def _bias_tiles(rel_bias):
    def lookup(table, idx):
        onehot = (idx[..., None] == jnp.arange(table.shape[0])).astype(_F32)
        return jnp.einsum("...d,dh->h...", onehot, table, precision=lax.Precision.HIGHEST)

    lut = lookup(rel_bias.astype(_F32), _t5_bucket(jnp.arange(MAX_DISTANCE))).T * LOG2E
    shifted = lut - lut[MAX_DISTANCE - 1]
    qi = jnp.arange(LANES)[:, None]
    kj = jnp.arange(LANES)[None, :]
    clip = lambda dist: jnp.clip(dist, 0, MAX_DISTANCE - 1)
    d0 = qi - kj
    td = jnp.where((d0 >= 0)[None], lookup(shifted, clip(d0)), NEG)
    ts = lookup(shifted, clip(LANES + qi - kj))
    dw = WINDOW + qi - jnp.arange(2 * WINDOW)[None, :]
    swa = jnp.where(((dw >= 0) & (dw < WINDOW))[None], lookup(lut, clip(dw)), NEG)
    return td, ts, swa


def _forward(x, rel_bias, g_attn_norm, g_ffn_norm, w_qkv_a, w_o_a, g_q_a, g_k_a, lam_qk_a, g_sub_a,
             g_kv_norm, w_kv, g_k_shared, w_q_b, w_o_b, g_q_b, sinks_b, w_gate, w_up, w_down, tiles):
    B, S, D = x.shape
    M = B * S
    T = tiles
    q_scale = HEAD_DIM ** -0.5 * LOG2E
    n_maps = D // HEAD_DIM
    kvw = w_kv.shape[1] // 2
    td, ts, swa_bias = _bias_tiles(rel_bias)
    bf = _cast_bf16
    x = x.reshape(M, D)

    def ffn(x, xb, inv_rms, layer, with_stats):
        act, w_down_bf16 = _gate_up(xb, inv_rms, w_gate, w_up, layer, g_ffn_norm[layer], w_down,
                                    T["mm_tm"], T["gu_tn"])
        return _down_residual(act, w_down_bf16, x, with_stats, T["down_tm"], T["down_tk"], T["down_tn"])

    lambda_init = 0.8 - 0.6 * math.exp(-0.3 * 0)
    xb, inv_rms = _prep(x, T["rms_tm"])
    g_cols = jnp.concatenate([jnp.tile(g_q_a[0] * q_scale, n_maps), jnp.tile(g_k_a[0], n_maps),
                              jnp.ones((D,), _F32)])[None]
    qkv = _proj_headnorm(xb, inv_rms, bf(w_qkv_a, 0, g_attn_norm[0]), g_cols, 2 * D, T["mm_tm"], T["qkv_tn"])
    o, (w_o_a_bf, w_kv_bf, w_q_b_bf, w_o_b_bf) = _diff_attention(
        qkv, td, ts, lam_qk_a[0], g_sub_a[0][None],
        [(w_o_a, 0, None), (w_kv[None], 0, g_kv_norm), (w_q_b, 0, g_attn_norm[1]), (w_o_b, 0, None)],
        B, S, D, T["attn_t"], T["attn_stage_blocks"], lambda_init)
    x, xb, inv_rms = _proj_residual(o, w_o_a_bf, x, T["mm_tm"], T["res_tn"])
    x, xb, inv_rms = ffn(x, xb, inv_rms, 0, True)

    g_kv_cols = jnp.concatenate([jnp.tile(g_k_shared, kvw // HEAD_DIM), jnp.ones((kvw,), _F32)])[None]
    kv = _proj_headnorm(xb, inv_rms, w_kv_bf, g_kv_cols, kvw, T["mm_tm"], min(T["qkv_tn"], kvw))
    q = _proj_headnorm(xb, inv_rms, w_q_b_bf, jnp.tile(g_q_b[0] * q_scale, n_maps)[None],
                       D, T["mm_tm"], T["qkv_tn"])
    sinks = jnp.broadcast_to((sinks_b[0].astype(_F32) * LOG2E).reshape(-1, SWA_GROUP, 1),
                             (n_maps // SWA_GROUP, SWA_GROUP, LANES))
    o = _swa_attention(q, kv, swa_bias, sinks, B, S, D, T["swa_tq"])
    x, xb, inv_rms = _proj_residual(o, w_o_b_bf, x, T["mm_tm"], T["res_tn"])
    (x,) = ffn(x, xb, inv_rms, 1, False)
    return x.reshape(B, S, D)


def kernel(x, rel_bias, g_attn_norm, g_ffn_norm, w_qkv_a, w_o_a, g_q_a, g_k_a, lam_qk_a, g_sub_a,
           g_kv_norm, w_kv, g_k_shared, w_q_b, w_o_b, g_q_b, sinks_b, w_gate, w_up, w_down):
    B, S, D = x.shape
    tiles = _tiles(B * S, S, D, w_gate.shape[-1])
    return _forward(x, rel_bias, g_attn_norm, g_ffn_norm, w_qkv_a, w_o_a, g_q_a, g_k_a, lam_qk_a,
                    g_sub_a, g_kv_norm, w_kv, g_k_shared, w_q_b, w_o_b, g_q_b, sinks_b,
                    w_gate, w_up, w_down, tiles)
```

```python
import functools
import math

import jax
import jax.numpy as jnp
from jax import lax
from jax.experimental import pallas as pl
from jax.experimental.pallas import tpu as pltpu

HEAD_DIM = 128
WINDOW = 128
SWA_GROUP = 4
NUM_BUCKETS = 32
MAX_DISTANCE = 128
EPS = 1e-6
NEG = -1e30
LOG2E = math.log2(math.e)
LANES = 128
VMEM_LIMIT_BYTES = 56 * 1024 * 1024
CAST_BLOCK_BYTES = 8 * 1024 * 1024
MXU_COLS = 256
BF16_SUBLANES = 16

_F32 = jnp.float32
_BF16 = jnp.bfloat16
_NT = (((1,), (1,)), ((), ()))


def _params(*sem):
    return pltpu.CompilerParams(dimension_semantics=sem, vmem_limit_bytes=VMEM_LIMIT_BYTES)


def _tiles(M, S, D, D_FF):
    return dict(
        rms_tm=min(256, M),
        mm_tm=min(1024, M),
        qkv_tn=min(1024, D),
        res_tn=min(512, D),
        gu_tn=min(256, D_FF),
        down_tm=min(512, M),
        down_tk=D_FF // 2,
        down_tn=min(1024, D),
        attn_t=min(512, S // 4),
        attn_stage_blocks=3,
        swa_tq=min(1024, S),
    )


def _lane_tiles(a):
    return [a[:, u * LANES:(u + 1) * LANES] for u in range(a.shape[1] // LANES)]


def _rep(a, width):
    return a if width == LANES else jnp.concatenate([a] * (width // LANES), axis=1)


def _lane_rep(v):
    return jnp.broadcast_to(v.astype(_F32)[:, None], (v.shape[0], LANES))


def _cast_kernel(w_ref, *rest):
    o_ref = rest[-1]
    w = w_ref[...]
    if len(rest) == 2:
        w = w * _rep(rest[0][...], w.shape[1])
    o_ref[...] = w.astype(o_ref.dtype)


def _cast_bf16(w, layer, row_gain=None):
    _, R, C = w.shape
    tr = min(R, max(8, CAST_BLOCK_BYTES // (4 * C) // 8 * 8))
    while R % tr:
        tr -= 8
    args, specs = [w], [pl.BlockSpec((None, tr, C), lambda i: (layer, i, 0))]
    if row_gain is not None:
        args.append(_lane_rep(row_gain))
        specs.append(pl.BlockSpec((tr, LANES), lambda i: (i, 0)))
    return pl.pallas_call(
        _cast_kernel,
        grid=(R // tr,),
        in_specs=specs,
        out_specs=pl.BlockSpec((tr, C), lambda i: (i, 0)),
        out_shape=jax.ShapeDtypeStruct((R, C), _BF16),
        compiler_params=_params("arbitrary"),
        name="cast_bf16",
    )(*args)


class _SideCast:
    def __init__(self, w, layer, row_gain, steps, step_index):
        _, R, C = w.shape
        stride = 1
        while (R * stride) % steps or (R * stride // steps) % BF16_SUBLANES:
            stride *= 2
            assert stride <= steps, (R, steps)
        rows = R * stride // steps
        index = lambda *ids: step_index(*ids) // stride
        self.has_gain = row_gain is not None
        self.args = [w]
        self.in_specs = [pl.BlockSpec((None, rows, C), lambda *ids: (layer, index(*ids), 0))]
        if self.has_gain:
            self.args.append(_lane_rep(row_gain))
            self.in_specs.append(pl.BlockSpec((rows, LANES), lambda *ids: (index(*ids), 0)))
        self.out_spec = pl.BlockSpec((rows, C), lambda *ids: (index(*ids), 0))
        self.out_shape = jax.ShapeDtypeStruct((R, C), _BF16)


def _run_side_casts(jobs_have_gain, in_refs, out_refs):
    pos = 0
    for has_gain, o_ref in zip(jobs_have_gain, out_refs):
        n = 2 if has_gain else 1
        _cast_kernel(*in_refs[pos:pos + n], o_ref)
        pos += n


def _prep_kernel(x_ref, xb_ref, r_ref):
    x = x_ref[...]
    xb_ref[...] = x.astype(xb_ref.dtype)
    r_ref[...] = jnp.broadcast_to(lax.rsqrt(jnp.mean(x * x, axis=-1, keepdims=True) + EPS), r_ref.shape)


def _prep(x, tm):
    M, D = x.shape
    return pl.pallas_call(
        _prep_kernel,
        grid=(M // tm,),
        in_specs=[pl.BlockSpec((tm, D), lambda i: (i, 0))],
        out_specs=[pl.BlockSpec((tm, D), lambda i: (i, 0)), pl.BlockSpec((tm, LANES), lambda i: (i, 0))],
        out_shape=[jax.ShapeDtypeStruct((M, D), _BF16), jax.ShapeDtypeStruct((M, LANES), _F32)],
        compiler_params=_params("arbitrary"),
        name="rms_stats",
    )(x)


def _accumulate_inv_rms(r_ref, ssq, first, last, width):
    @pl.when(first)
    def _():
        r_ref[...] = ssq

    @pl.when(jnp.logical_not(first))
    def _():
        r_ref[...] += ssq

    @pl.when(last)
    def _():
        total = jnp.sum(r_ref[...], axis=-1, keepdims=True)
        r_ref[...] = jnp.broadcast_to(lax.rsqrt(total / width + EPS), r_ref.shape)


def _row_tile_spec(tm, K):
    return pl.BlockSpec((tm, K), lambda i, j: (i, 0))


def _col_groups(width):
    sub = min(MXU_COLS, width)
    return [slice(u * sub, (u + 1) * sub) for u in range(width // sub)]


def _proj_headnorm_kernel(x_ref, r_ref, w_ref, g_ref, o_ref, *, n_norm_tiles, n_tiles):
    def sub_dots():
        for cols in _col_groups(o_ref.shape[1]):
            y = jnp.dot(x_ref[...], w_ref[:, cols], preferred_element_type=_F32)
            yield cols, _rep(r_ref[...], y.shape[1]) * y

    def normed():
        for cols, y in sub_dots():
            for c in range(y.shape[1] // HEAD_DIM):
                yc = y[:, c * HEAD_DIM:(c + 1) * HEAD_DIM]
                sl = slice(cols.start + c * HEAD_DIM, cols.start + (c + 1) * HEAD_DIM)
                r = lax.rsqrt(jnp.mean(yc * yc, axis=-1, keepdims=True) + EPS)
                o_ref[:, sl] = (yc * r * g_ref[:, sl]).astype(o_ref.dtype)

    def plain():
        for cols, y in sub_dots():
            o_ref[:, cols] = y.astype(o_ref.dtype)

    if n_norm_tiles == n_tiles:
        normed()
    else:
        j = pl.program_id(1)
        pl.when(j < n_norm_tiles)(normed)
        pl.when(j >= n_norm_tiles)(plain)


def _proj_headnorm(xb, inv_rms, w, g_cols, n_norm_cols, tm, tn):
    M, K = xb.shape
    N = w.shape[1]
    kern = functools.partial(_proj_headnorm_kernel, n_norm_tiles=n_norm_cols // tn, n_tiles=N // tn)
    return pl.pallas_call(
        kern,
        grid=(M // tm, N // tn),
        in_specs=[_row_tile_spec(tm, K),
                  pl.BlockSpec((tm, LANES), lambda i, j: (i, 0)),
                  pl.BlockSpec((K, tn), lambda i, j: (0, j)),
                  pl.BlockSpec((1, tn), lambda i, j: (0, j))],
        out_specs=pl.BlockSpec((tm, tn), lambda i, j: (i, j)),
        out_shape=jax.ShapeDtypeStruct((M, N), _BF16),
        compiler_params=_params("arbitrary", "arbitrary"),
        name="proj_headnorm",
    )(xb, inv_rms, w, g_cols)


def _store_residual_tile(cols, y, o_ref, stat_refs, ssq):
    o_ref[:, cols] = y
    if not stat_refs:
        return None
    stat_refs[0][:, cols] = y.astype(_BF16)
    part = functools.reduce(jnp.add, _lane_tiles(y * y))
    return part if ssq is None else ssq + part


def _proj_residual_kernel(x_ref, w_ref, res_ref, o_ref, ob_ref, r_ref, *, width):
    j = pl.program_id(1)
    ssq = None
    for cols in _col_groups(o_ref.shape[1]):
        y = res_ref[:, cols] + jnp.dot(x_ref[...], w_ref[:, cols], preferred_element_type=_F32)
        ssq = _store_residual_tile(cols, y, o_ref, (ob_ref,), ssq)
    _accumulate_inv_rms(r_ref, ssq, j == 0, j == pl.num_programs(1) - 1, width)


def _proj_residual(x, w, res, tm, tn):
    M, K = x.shape
    N = w.shape[1]
    tile = pl.BlockSpec((tm, tn), lambda i, j: (i, j))
    return pl.pallas_call(
        functools.partial(_proj_residual_kernel, width=N),
        grid=(M // tm, N // tn),
        in_specs=[_row_tile_spec(tm, K),
                  pl.BlockSpec((K, tn), lambda i, j: (0, j)),
                  tile],
        out_specs=[tile, tile, pl.BlockSpec((tm, LANES), lambda i, j: (i, 0))],
        out_shape=[jax.ShapeDtypeStruct((M, N), _F32), jax.ShapeDtypeStruct((M, N), _BF16),
                   jax.ShapeDtypeStruct((M, LANES), _F32)],
        compiler_params=_params("arbitrary", "arbitrary"),
        name="proj_residual",
    )(x, w, res)


def _gate_up_kernel(x_ref, r_ref, wg_ref, wu_ref, gain_ref, *rest):
    o_ref = rest[-2] if len(rest) == 3 else rest[0]
    if len(rest) == 3:
        rest[2][...] = rest[0][...].astype(_BF16)
    x = x_ref[...]
    tn = o_ref.shape[1]
    gain = _rep(gain_ref[...], tn)
    r = _rep(r_ref[...], tn)
    g = r * jnp.dot(x, (wg_ref[...] * gain).astype(_BF16), preferred_element_type=_F32)
    u = r * jnp.dot(x, (wu_ref[...] * gain).astype(_BF16), preferred_element_type=_F32)
    o_ref[...] = (g * jax.nn.sigmoid(g) * u).astype(o_ref.dtype)


def _gate_up(xb, inv_rms, wg, wu, layer, gain, w_side, tm, tn):
    M, K = xb.shape
    N = wg.shape[2]
    nj = N // tn
    steps = (M // tm) * nj
    _, R, C = w_side.shape
    if R % steps or (R // steps) % BF16_SUBLANES:
        side_out = _cast_bf16(w_side, layer)
        side_args, side_in, side_spec, side_shape = [], [], [], []
    else:
        rows = R // steps
        side_args = [w_side]
        side_in = [pl.BlockSpec((None, rows, C), lambda i, j: (layer, i * nj + j, 0))]
        side_spec = [pl.BlockSpec((rows, C), lambda i, j: (i * nj + j, 0))]
        side_shape = [jax.ShapeDtypeStruct((R, C), _BF16)]
    w_spec = pl.BlockSpec((None, K, tn), lambda i, j: (layer, 0, j))
    outs = pl.pallas_call(
        _gate_up_kernel,
        grid=(M // tm, nj),
        in_specs=[_row_tile_spec(tm, K),
                  pl.BlockSpec((tm, LANES), lambda i, j: (i, 0)),
                  w_spec, w_spec,
                  pl.BlockSpec((K, LANES), lambda i, j: (0, 0))] + side_in,
        out_specs=[pl.BlockSpec((tm, tn), lambda i, j: (i, j))] + side_spec,
        out_shape=[jax.ShapeDtypeStruct((M, N), _BF16)] + side_shape,
        compiler_params=_params("arbitrary", "arbitrary"),
        name="ffn_gate_up",
    )(xb, inv_rms, wg, wu, _lane_rep(gain), *side_args)
    return (outs[0], outs[1]) if side_args else (outs[0], side_out)


def _down_kernel(x_ref, w_ref, res_ref, o_ref, *rest, width):
    *stat_refs, acc_ref = rest
    k = pl.program_id(1)
    j = pl.program_id(2)

    def parts():
        for cols in _col_groups(o_ref.shape[1]):
            yield cols, jnp.dot(x_ref[...], w_ref[:, cols], preferred_element_type=_F32)

    @pl.when(k == 0)
    def _():
        for cols, part in parts():
            acc_ref[j, :, cols] = part

    @pl.when(k == pl.num_programs(1) - 1)
    def _():
        ssq = None
        for cols, part in parts():
            y = res_ref[:, cols] + acc_ref[j, :, cols] + part
            ssq = _store_residual_tile(cols, y, o_ref, stat_refs, ssq)
        if stat_refs:
            _accumulate_inv_rms(stat_refs[1], ssq, j == 0, j == pl.num_programs(2) - 1, width)


def _down_residual(x, w, res, with_stats, tm, tk, tn):
    M, K = x.shape
    N = w.shape[1]
    nk = K // tk
    assert nk == 2
    tile = pl.BlockSpec((tm, tn), lambda i, k, j: (i, j * k))
    out_specs, out_shape = [tile], [jax.ShapeDtypeStruct((M, N), _F32)]
    if with_stats:
        out_specs += [tile, pl.BlockSpec((tm, LANES), lambda i, k, j: (i, 0))]
        out_shape += [jax.ShapeDtypeStruct((M, N), _BF16), jax.ShapeDtypeStruct((M, LANES), _F32)]
    return pl.pallas_call(
        functools.partial(_down_kernel, width=N),
        grid=(M // tm, nk, N // tn),
        in_specs=[pl.BlockSpec((tm, tk), lambda i, k, j: (i, k)),
                  pl.BlockSpec((tk, tn), lambda i, k, j: (k, j)),
                  tile],
        out_specs=out_specs,
        out_shape=out_shape,
        scratch_shapes=[pltpu.VMEM((N // tn, tm, tn), _F32)],
        compiler_params=_params("arbitrary", "arbitrary", "arbitrary"),
        name="ffn_down_residual",
    )(x, w, res)


def _diff_attn_kernel(q_ref, k_ref, v_ref, td_ref, ts_ref, lam_ref, gsub_ref, *refs,
                      t, stage_blocks, lambda_init, side_gains):
    n_side_in = sum(2 if has_gain else 1 for has_gain in side_gains)
    side_in, o_ref = refs[:n_side_in], refs[n_side_in]
    side_out = refs[n_side_in + 1:n_side_in + 1 + len(side_gains)]
    bias_ref, sx_ref, sy_ref, sz_ref, mx_ref, my_ref, mz_ref, m_ref, l_ref, acc_ref = (
        refs[n_side_in + 1 + len(side_gains):])
    _attention_step(q_ref, k_ref, v_ref, td_ref, ts_ref, lam_ref, gsub_ref, o_ref,
                    bias_ref, sx_ref, sy_ref, sz_ref, mx_ref, my_ref, mz_ref, m_ref, l_ref, acc_ref,
                    t=t, stage_blocks=stage_blocks, lambda_init=lambda_init)
    _run_side_casts(side_gains, side_in, side_out)


def _attention_step(q_ref, k_ref, v_ref, td_ref, ts_ref, lam_ref, gsub_ref, o_ref,
                    bias_ref, sx_ref, sy_ref, sz_ref, mx_ref, my_ref, mz_ref, m_ref, l_ref, acc_ref,
                    *, t, stage_blocks, lambda_init):
    i = pl.program_id(2)
    nt = t // LANES

    @pl.when(i == 0)
    def _build_bias():
        zeros = jnp.zeros((LANES, LANES), _F32)
        neg = jnp.full((LANES, LANES), NEG, _F32)
        for c in range(2):
            for a in range(nt):
                for b in range(nt):
                    rows = slice(a * LANES, (a + 1) * LANES)
                    cols = slice(b * LANES, (b + 1) * LANES)
                    if a == b:
                        tile = td_ref[c]
                    elif a == b + 1:
                        tile = ts_ref[c]
                    else:
                        tile = zeros if a > b else neg
                    bias_ref[c, 0, rows, cols] = tile
                    bias_ref[c, 1, rows, cols] = ts_ref[c] if (a == 0 and b == nt - 1) else zeros
                    bias_ref[c, 2, rows, cols] = neg

    def stage_a(blocks, s_out, m_out):
        for c in range(2):
            cols = slice(c * HEAD_DIM, (c + 1) * HEAD_DIM)
            part = None
            for n, (j, kind) in enumerate(blocks):
                start = pl.multiple_of(j * t, t)
                s = lax.dot_general(q_ref[:, cols], k_ref[pl.ds(start, t), cols], _NT,
                                    preferred_element_type=_F32)
                if kind is not None:
                    s = s + bias_ref[c, kind]
                s_out[c, :, n * t:(n + 1) * t] = s
                block_max = functools.reduce(jnp.maximum, _lane_tiles(s))
                part = block_max if part is None else jnp.maximum(part, block_max)
            m_out[c] = jnp.broadcast_to(jnp.max(part, axis=-1, keepdims=True), (t, LANES))

    def stage_b(blocks, s_in, m_in, first=False):
        for c in range(2):
            if first:
                m_new = m_in[c]
                psum = jnp.zeros((t, LANES), _F32)
                pv = jnp.zeros(acc_ref.shape[1:], _F32)
            else:
                m_old = m_ref[c]
                m_new = jnp.maximum(m_old, m_in[c])
                alpha = jnp.exp2(m_old - m_new)
                psum = alpha * l_ref[c]
                pv = _rep(alpha, acc_ref.shape[2]) * acc_ref[c]
            m_rep = _rep(m_new, MXU_COLS)
            for n, (j, _) in enumerate(blocks):
                for u in range(t // MXU_COLS):
                    off = u * MXU_COLS
                    p = jnp.exp2(s_in[c, :, n * t + off:n * t + off + MXU_COLS] - m_rep)
                    psum = psum + functools.reduce(jnp.add, _lane_tiles(p))
                    v_rows = v_ref[pl.ds(pl.multiple_of(j * t + off, MXU_COLS), MXU_COLS), :]
                    pv = pv + jnp.dot(p.astype(_BF16), v_rows, preferred_element_type=_F32)
            l_ref[c] = psum
            acc_ref[c] = pv
            m_ref[c] = m_new

    g = stage_blocks
    j_sub = jnp.maximum(i - 1, 0)
    n_far = jnp.maximum(i - 1, 0) // g
    left_over = jnp.maximum(i - 1, 0) % g
    near = [(i, 0), (j_sub, jnp.where(i == 0, 2, 1))] + [(jnp.maximum(i - 2 - n, 0), None) for n in range(g - 1)]

    def far(f):
        return [(g * f + n, None) for n in range(g)]

    for width in range(2, g + 2):
        is_width = left_over == width - 2

        @pl.when(is_width)
        def _(width=width):
            stage_a(near[:width], sz_ref, mz_ref)

        @pl.when(jnp.logical_and(is_width, n_far == 0))
        def _(width=width):
            stage_b(near[:width], sz_ref, mz_ref, first=True)

        @pl.when(jnp.logical_and(is_width, n_far > 0))
        def _(width=width):
            stage_a(far(0), sy_ref, my_ref)
            stage_b(near[:width], sz_ref, mz_ref, first=True)

    n_rest = jnp.maximum(n_far - 1, 0)
    n_pairs = n_rest // 2

    def pair_body(u, carry):
        stage_a(far(2 * u + 1), sx_ref, mx_ref)
        stage_b(far(2 * u), sy_ref, my_ref)
        stage_a(far(2 * u + 2), sy_ref, my_ref)
        stage_b(far(2 * u + 1), sx_ref, mx_ref)
        return carry

    lax.fori_loop(0, n_pairs, pair_body, 0)

    @pl.when(jnp.logical_and(n_far > 0, n_rest % 2 == 1))
    def _():
        stage_a(far(n_far - 1), sx_ref, mx_ref)
        stage_b(far(2 * n_pairs), sy_ref, my_ref)
        stage_b(far(n_far - 1), sx_ref, mx_ref)

    @pl.when(jnp.logical_and(n_far > 0, n_rest % 2 == 0))
    def _():
        stage_b(far(2 * n_pairs), sy_ref, my_ref)

    lf = lam_ref[...]
    lam = (jnp.exp(jnp.sum(lf[0:1] * lf[1:2], axis=-1, keepdims=True))
           - jnp.exp(jnp.sum(lf[2:3] * lf[3:4], axis=-1, keepdims=True)) + lambda_init)
    l0 = jnp.sum(l_ref[0], axis=-1, keepdims=True)
    l1 = jnp.sum(l_ref[1], axis=-1, keepdims=True)
    o = acc_ref[0] / l0 - lam * (acc_ref[1] / l1)
    o = o * lax.rsqrt(jnp.mean(o * o, axis=-1, keepdims=True) + EPS)
    o_ref[...] = (o * gsub_ref[...] * (1.0 - lambda_init)).astype(o_ref.dtype)


def _diff_attention(qkv, td, ts, lam_qk, g_sub, side_weights, B, S, D, t, stage_blocks, lambda_init):
    n_heads = D // (2 * HEAD_DIM)
    nq = S // t
    w = 2 * HEAD_DIM
    jobs = [_SideCast(sw, layer, gain, B * n_heads * nq, lambda b, h, i: (b * n_heads + h) * nq + i)
            for sw, layer, gain in side_weights]
    kern = functools.partial(_diff_attn_kernel, t=t, stage_blocks=stage_blocks, lambda_init=lambda_init,
                             side_gains=tuple(job.has_gain for job in jobs))
    outs = pl.pallas_call(
        kern,
        grid=(B, n_heads, nq),
        in_specs=[pl.BlockSpec((t, w), lambda b, h, i: (b * nq + i, h)),
                  pl.BlockSpec((S, w), lambda b, h, i: (b, n_heads + h)),
                  pl.BlockSpec((S, w), lambda b, h, i: (b, 2 * n_heads + h)),
                  pl.BlockSpec((2, LANES, LANES), lambda b, h, i: (h, 0, 0)),
                  pl.BlockSpec((2, LANES, LANES), lambda b, h, i: (h, 0, 0)),
                  pl.BlockSpec((4, HEAD_DIM), lambda b, h, i: (0, 0)),
                  pl.BlockSpec((1, w), lambda b, h, i: (0, 0))] + [s for job in jobs for s in job.in_specs],
        out_specs=[pl.BlockSpec((t, w), lambda b, h, i: (b * nq + i, h))] + [job.out_spec for job in jobs],
        out_shape=[jax.ShapeDtypeStruct((B * S, D), _BF16)] + [job.out_shape for job in jobs],
        scratch_shapes=[pltpu.VMEM((2, 3, t, t), _F32),
                        pltpu.VMEM((2, t, stage_blocks * t), _F32),
                        pltpu.VMEM((2, t, stage_blocks * t), _F32),
                        pltpu.VMEM((2, t, (stage_blocks + 1) * t), _F32),
                        pltpu.VMEM((2, t, LANES), _F32),
                        pltpu.VMEM((2, t, LANES), _F32),
                        pltpu.VMEM((2, t, LANES), _F32),
                        pltpu.VMEM((2, t, LANES), _F32),
                        pltpu.VMEM((2, t, LANES), _F32),
                        pltpu.VMEM((2, t, w), _F32)],
        compiler_params=_params("arbitrary", "arbitrary", "arbitrary"),
        name="diff_attention",
    )(qkv, qkv, qkv, td, ts, lam_qk, g_sub, *[a for job in jobs for a in job.args])
    return outs[0], list(outs[1:])


def _swa_kernel(q_ref, kc_ref, kp_ref, vc_ref, vp_ref, bias_ref, sink_ref, o_ref, *, tq):
    first = pl.program_id(2) == 0
    rows = SWA_GROUP * WINDOW
    bias = bias_ref[...].reshape(rows, 2 * WINDOW)
    sink = jnp.concatenate(
        [jnp.broadcast_to(sink_ref[r:r + 1, :], (WINDOW, LANES)) for r in range(SWA_GROUP)], axis=0)
    col = lax.broadcasted_iota(jnp.int32, (rows, 2 * WINDOW), 1)
    ones = jnp.ones((2 * WINDOW, LANES), _BF16)
    for c in range(tq // WINDOW):
        cur = slice(c * WINDOW, (c + 1) * WINDOW)
        if c == 0:
            k_prev, v_prev = kp_ref[...], vp_ref[...]
        else:
            prev = slice((c - 1) * WINDOW, c * WINDOW)
            k_prev, v_prev = kc_ref[prev, :], vc_ref[prev, :]
        kk = jnp.concatenate([k_prev, kc_ref[cur, :]], axis=0)
        vv = jnp.concatenate([jnp.concatenate([v_prev, vc_ref[cur, :]], axis=0), ones], axis=1)
        q4 = jnp.concatenate(
            [q_ref[cur, r * HEAD_DIM:(r + 1) * HEAD_DIM] for r in range(SWA_GROUP)], axis=0)
        s = lax.dot_general(q4, kk, _NT, preferred_element_type=_F32) + bias
        if c == 0:
            s = jnp.where(first & (col < WINDOW), NEG, s)
        part = functools.reduce(jnp.maximum, _lane_tiles(s))
        m = jnp.maximum(jnp.broadcast_to(jnp.max(part, axis=-1, keepdims=True), (rows, LANES)), sink)
        p = jnp.exp2(s - jnp.concatenate([m, m], axis=1))
        pv = jnp.dot(p.astype(_BF16), vv, preferred_element_type=_F32)
        o = pv[:, :HEAD_DIM] / (pv[:, HEAD_DIM:] + jnp.exp2(sink - m))
        for r in range(SWA_GROUP):
            o_ref[cur, r * HEAD_DIM:(r + 1) * HEAD_DIM] = (
                o[r * WINDOW:(r + 1) * WINDOW, :].astype(o_ref.dtype))


def _swa_attention(q, kv, bias, sinks, B, S, D, tq):
    n_kv = D // (SWA_GROUP * HEAD_DIM)
    nt = S // tq
    bpt = tq // WINDOW
    gw = SWA_GROUP * HEAD_DIM

    def prev_map(off):
        return lambda b, g, t: (jnp.maximum((b * nt + t) * bpt - 1, 0), off + g)

    kern = functools.partial(_swa_kernel, tq=tq)
    return pl.pallas_call(
        kern,
        grid=(B, n_kv, nt),
        in_specs=[pl.BlockSpec((tq, gw), lambda b, g, t: (b * nt + t, g)),
                  pl.BlockSpec((tq, HEAD_DIM), lambda b, g, t: (b * nt + t, g)),
                  pl.BlockSpec((WINDOW, HEAD_DIM), prev_map(0)),
                  pl.BlockSpec((tq, HEAD_DIM), lambda b, g, t: (b * nt + t, n_kv + g)),
                  pl.BlockSpec((WINDOW, HEAD_DIM), prev_map(n_kv)),
                  pl.BlockSpec((SWA_GROUP, WINDOW, 2 * WINDOW), lambda b, g, t: (g, 0, 0)),
                  pl.BlockSpec((None, SWA_GROUP, LANES), lambda b, g, t: (g, 0, 0))],
        out_specs=pl.BlockSpec((tq, gw), lambda b, g, t: (b * nt + t, g)),
        out_shape=jax.ShapeDtypeStruct((B * S, D), _BF16),
        compiler_params=_params("arbitrary", "arbitrary", "arbitrary"),
        name="swa_attention",
    )(q, kv, kv, kv, kv, bias, sinks)


def _t5_bucket(n):
    max_exact = NUM_BUCKETS // 2
    nf = jnp.maximum(n, 1).astype(_F32)
    large = max_exact + (jnp.log(nf / max_exact) / math.log(MAX_DISTANCE / max_exact)
                         * (NUM_BUCKETS - max_exact)).astype(jnp.int32)
    large = jnp.minimum(large, NUM_BUCKETS - 1)
    return jnp.where(n < max_exact, n, large)


def _bias_tiles(rel_bias):
    def lookup(table, idx):
        onehot = (idx[..., None] == jnp.arange(table.shape[0])).astype(_F32)
        return jnp.einsum("...d,dh->h...", onehot, table, precision=lax.Precision.HIGHEST)

    lut = lookup(rel_bias.astype(_F32), _t5_bucket(jnp.arange(MAX_DISTANCE))).T * LOG2E
    shifted = lut - lut[MAX_DISTANCE - 1]
    qi = jnp.arange(LANES)[:, None]
    kj = jnp.arange(LANES)[None, :]
    clip = lambda dist: jnp.clip(dist, 0, MAX_DISTANCE - 1)
    d0 = qi - kj
    td = jnp.where((d0 >= 0)[None], lookup(shifted, clip(d0)), NEG)
    ts = lookup(shifted, clip(LANES + qi - kj))
    dw = WINDOW + qi - jnp.arange(2 * WINDOW)[None, :]
    swa = jnp.where(((dw >= 0) & (dw < WINDOW))[None], lookup(lut, clip(dw)), NEG)
    return td, ts, swa


def _forward(x, rel_bias, g_attn_norm, g_ffn_norm, w_qkv_a, w_o_a, g_q_a, g_k_a, lam_qk_a, g_sub_a,
             g_kv_norm, w_kv, g_k_shared, w_q_b, w_o_b, g_q_b, sinks_b, w_gate, w_up, w_down, tiles):
    B, S, D = x.shape
    M = B * S
    T = tiles
    q_scale = HEAD_DIM ** -0.5 * LOG2E
    n_maps = D // HEAD_DIM
    kvw = w_kv.shape[1] // 2
    td, ts, swa_bias = _bias_tiles(rel_bias)
    bf = _cast_bf16
    x = x.reshape(M, D)

    def ffn(x, xb, inv_rms, layer, with_stats):
        act, w_down_bf16 = _gate_up(xb, inv_rms, w_gate, w_up, layer, g_ffn_norm[layer], w_down,
                                    T["mm_tm"], T["gu_tn"])
        return _down_residual(act, w_down_bf16, x, with_stats, T["down_tm"], T["down_tk"], T["down_tn"])

    lambda_init = 0.8 - 0.6 * math.exp(-0.3 * 0)
    xb, inv_rms = _prep(x, T["rms_tm"])
    g_cols = jnp.concatenate([jnp.tile(g_q_a[0] * q_scale, n_maps), jnp.tile(g_k_a[0], n_maps),
                              jnp.ones((D,), _F32)])[None]
    qkv = _proj_headnorm(xb, inv_rms, bf(w_qkv_a, 0, g_attn_norm[0]), g_cols, 2 * D, T["mm_tm"], T["qkv_tn"])
    o, (w_o_a_bf, w_kv_bf, w_q_b_bf, w_o_b_bf) = _diff_attention(
        qkv, td, ts, lam_qk_a[0], g_sub_a[0][None],
        [(w_o_a, 0, None), (w_kv[None], 0, g_kv_norm), (w_q_b, 0, g_attn_norm[1]), (w_o_b, 0, None)],
        B, S, D, T["attn_t"], T["attn_stage_blocks"], lambda_init)
    x, xb, inv_rms = _proj_residual(o, w_o_a_bf, x, T["mm_tm"], T["res_tn"])
    x, xb, inv_rms = ffn(x, xb, inv_rms, 0, True)

    g_kv_cols = jnp.concatenate([jnp.tile(g_k_shared, kvw // HEAD_DIM), jnp.ones((kvw,), _F32)])[None]
    kv = _proj_headnorm(xb, inv_rms, w_kv_bf, g_kv_cols, kvw, T["mm_tm"], min(T["qkv_tn"], kvw))
    q = _proj_headnorm(xb, inv_rms, w_q_b_bf, jnp.tile(g_q_b[0] * q_scale, n_maps)[None],
                       D, T["mm_tm"], T["qkv_tn"])
    sinks = jnp.broadcast_to((sinks_b[0].astype(_F32) * LOG2E).reshape(-1, SWA_GROUP, 1),
                             (n_maps // SWA_GROUP, SWA_GROUP, LANES))
    o = _swa_attention(q, kv, swa_bias, sinks, B, S, D, T["swa_tq"])
    x, xb, inv_rms = _proj_residual(o, w_o_b_bf, x, T["mm_tm"], T["res_tn"])
    (x,) = ffn(x, xb, inv_rms, 1, False)
    return x.reshape(B, S, D)


def kernel(x, rel_bias, g_attn_norm, g_ffn_norm, w_qkv_a, w_o_a, g_q_a, g_k_a, lam_qk_a, g_sub_a,
           g_kv_norm, w_kv, g_k_shared, w_q_b, w_o_b, g_q_b, sinks_b, w_gate, w_up, w_down):
    B, S, D = x.shape
    tiles = _tiles(B * S, S, D, w_gate.shape[-1])
    return _forward(x, rel_bias, g_attn_norm, g_ffn_norm, w_qkv_a, w_o_a, g_q_a, g_k_a, lam_qk_a,
                    g_sub_a, g_kv_norm, w_kv, g_k_shared, w_q_b, w_o_b, g_q_b, sinks_b,
                    w_gate, w_up, w_down, tiles)
```

```python
import functools
import math

import jax
import jax.numpy as jnp
from jax import lax
from jax.experimental import pallas as pl
from jax.experimental.pallas import tpu as pltpu

HEAD_DIM = 128
WINDOW = 128
SWA_GROUP = 4
NUM_BUCKETS = 32
MAX_DISTANCE = 128
EPS = 1e-6
NEG = -1e30
LOG2E = math.log2(math.e)
LANES = 128
VMEM_LIMIT_BYTES = 56 * 1024 * 1024
CAST_BLOCK_BYTES = 8 * 1024 * 1024
MXU_COLS = 256
BF16_SUBLANES = 16

_F32 = jnp.float32
_BF16 = jnp.bfloat16
_NT = (((1,), (1,)), ((), ()))


def _params(*sem):
    return pltpu.CompilerParams(dimension_semantics=sem, vmem_limit_bytes=VMEM_LIMIT_BYTES)


def _tiles(M, S, D, D_FF):
    return dict(
        rms_tm=min(256, M),
        mm_tm=min(1024, M),
        qkv_tn=min(1024, D),
        res_tn=min(512, D),
        gu_tn=min(256, D_FF),
        down_tm=min(512, M),
        down_tk=D_FF // 2,
        down_tn=min(1024, D),
        attn_t=min(512, S // 4),
        attn_stage_blocks=3,
        swa_tq=min(1024, S),
    )


def _lane_tiles(a):
    return [a[:, u * LANES:(u + 1) * LANES] for u in range(a.shape[1] // LANES)]


def _rep(a, width):
    return a if width == LANES else jnp.concatenate([a] * (width // LANES), axis=1)


def _lane_rep(v):
    return jnp.broadcast_to(v.astype(_F32)[:, None], (v.shape[0], LANES))


def _cast_kernel(w_ref, *rest):
    o_ref = rest[-1]
    w = w_ref[...]
    if len(rest) == 2:
        w = w * _rep(rest[0][...], w.shape[1])
    o_ref[...] = w.astype(o_ref.dtype)


def _cast_bf16(w, layer, row_gain=None):
    _, R, C = w.shape
    tr = min(R, max(8, CAST_BLOCK_BYTES // (4 * C) // 8 * 8))
    while R % tr:
        tr -= 8
    args, specs = [w], [pl.BlockSpec((None, tr, C), lambda i: (layer, i, 0))]
    if row_gain is not None:
        args.append(_lane_rep(row_gain))
        specs.append(pl.BlockSpec((tr, LANES), lambda i: (i, 0)))
    return pl.pallas_call(
        _cast_kernel,
        grid=(R // tr,),
        in_specs=specs,
        out_specs=pl.BlockSpec((tr, C), lambda i: (i, 0)),
        out_shape=jax.ShapeDtypeStruct((R, C), _BF16),
        compiler_params=_params("arbitrary"),
        name="cast_bf16",
    )(*args)


class _SideCast:
    def __init__(self, w, layer, row_gain, steps, step_index):
        _, R, C = w.shape
        stride = 1
        while (R * stride) % steps or (R * stride // steps) % BF16_SUBLANES:
            stride += 1
            assert stride <= steps, (R, steps)
        rows = R * stride // steps
        index = lambda *ids: step_index(*ids) // stride
        self.has_gain = row_gain is not None
        self.args = [w]
        self.in_specs = [pl.BlockSpec((None, rows, C), lambda *ids: (layer, index(*ids), 0))]
        if self.has_gain:
            self.args.append(_lane_rep(row_gain))
            self.in_specs.append(pl.BlockSpec((rows, LANES), lambda *ids: (index(*ids), 0)))
        self.out_spec = pl.BlockSpec((rows, C), lambda *ids: (index(*ids), 0))
        self.out_shape = jax.ShapeDtypeStruct((R, C), _BF16)


def _run_side_casts(jobs_have_gain, in_refs, out_refs):
    pos = 0
    for has_gain, o_ref in zip(jobs_have_gain, out_refs):
        n = 2 if has_gain else 1
        _cast_kernel(*in_refs[pos:pos + n], o_ref)
        pos += n


def _prep_kernel(x_ref, xb_ref, r_ref):
    x = x_ref[...]
    xb_ref[...] = x.astype(xb_ref.dtype)
    r_ref[...] = jnp.broadcast_to(lax.rsqrt(jnp.mean(x * x, axis=-1, keepdims=True) + EPS), r_ref.shape)


def _prep(x, tm):
    M, D = x.shape
    return pl.pallas_call(
        _prep_kernel,
        grid=(M // tm,),
        in_specs=[pl.BlockSpec((tm, D), lambda i: (i, 0))],
        out_specs=[pl.BlockSpec((tm, D), lambda i: (i, 0)), pl.BlockSpec((tm, LANES), lambda i: (i, 0))],
        out_shape=[jax.ShapeDtypeStruct((M, D), _BF16), jax.ShapeDtypeStruct((M, LANES), _F32)],
        compiler_params=_params("arbitrary"),
        name="rms_stats",
    )(x)


def _accumulate_inv_rms(r_ref, ssq, first, last, width):
    @pl.when(first)
    def _():
        r_ref[...] = ssq

    @pl.when(jnp.logical_not(first))
    def _():
        r_ref[...] += ssq

    @pl.when(last)
    def _():
        total = jnp.sum(r_ref[...], axis=-1, keepdims=True)
        r_ref[...] = jnp.broadcast_to(lax.rsqrt(total / width + EPS), r_ref.shape)


def _row_tile_spec(tm, K):
    return pl.BlockSpec((tm, K), lambda i, j: (i, 0))


def _col_groups(width):
    sub = min(MXU_COLS, width)
    return [slice(u * sub, (u + 1) * sub) for u in range(width // sub)]


def _proj_headnorm_kernel(x_ref, r_ref, w_ref, g_ref, *refs, n_norm_tiles, n_tiles, side_gains):
    n_side_in = sum(2 if has_gain else 1 for has_gain in side_gains)
    o_ref = refs[n_side_in]
    _run_side_casts(side_gains, refs[:n_side_in], refs[n_side_in + 1:])

    def sub_dots():
        for cols in _col_groups(o_ref.shape[1]):
            y = jnp.dot(x_ref[...], w_ref[:, cols], preferred_element_type=_F32)
            yield cols, _rep(r_ref[...], y.shape[1]) * y

    def normed():
        for cols, y in sub_dots():
            for c in range(y.shape[1] // HEAD_DIM):
                yc = y[:, c * HEAD_DIM:(c + 1) * HEAD_DIM]
                sl = slice(cols.start + c * HEAD_DIM, cols.start + (c + 1) * HEAD_DIM)
                r = lax.rsqrt(jnp.mean(yc * yc, axis=-1, keepdims=True) + EPS)
                o_ref[:, sl] = (yc * r * g_ref[:, sl]).astype(o_ref.dtype)

    def plain():
        for cols, y in sub_dots():
            o_ref[:, cols] = y.astype(o_ref.dtype)

    if n_norm_tiles == n_tiles:
        normed()
    else:
        j = pl.program_id(1)
        pl.when(j < n_norm_tiles)(normed)
        pl.when(j >= n_norm_tiles)(plain)


def _proj_headnorm(xb, inv_rms, w, g_cols, n_norm_cols, tm, tn, side_weights=()):
    M, K = xb.shape
    N = w.shape[1]
    nj = N // tn
    jobs = [_SideCast(sw, layer, gain, (M // tm) * nj, lambda i, j: i * nj + j)
            for sw, layer, gain in side_weights]
    kern = functools.partial(_proj_headnorm_kernel, n_norm_tiles=n_norm_cols // tn, n_tiles=nj,
                             side_gains=tuple(job.has_gain for job in jobs))
    outs = pl.pallas_call(
        kern,
        grid=(M // tm, nj),
        in_specs=[_row_tile_spec(tm, K),
                  pl.BlockSpec((tm, LANES), lambda i, j: (i, 0)),
                  pl.BlockSpec((K, tn), lambda i, j: (0, j)),
                  pl.BlockSpec((1, tn), lambda i, j: (0, j))] + [s for job in jobs for s in job.in_specs],
        out_specs=[pl.BlockSpec((tm, tn), lambda i, j: (i, j))] + [job.out_spec for job in jobs],
        out_shape=[jax.ShapeDtypeStruct((M, N), _BF16)] + [job.out_shape for job in jobs],
        compiler_params=_params("arbitrary", "arbitrary"),
        name="proj_headnorm",
    )(xb, inv_rms, w, g_cols, *[a for job in jobs for a in job.args])
    return outs[0], list(outs[1:])


def _store_residual_tile(cols, y, o_ref, stat_refs, ssq):
    o_ref[:, cols] = y
    if not stat_refs:
        return None
    stat_refs[0][:, cols] = y.astype(_BF16)
    part = functools.reduce(jnp.add, _lane_tiles(y * y))
    return part if ssq is None else ssq + part


def _proj_residual_kernel(x_ref, w_ref, res_ref, o_ref, ob_ref, r_ref, *, width):
    j = pl.program_id(1)
    ssq = None
    for cols in _col_groups(o_ref.shape[1]):
        y = res_ref[:, cols] + jnp.dot(x_ref[...], w_ref[:, cols], preferred_element_type=_F32)
        ssq = _store_residual_tile(cols, y, o_ref, (ob_ref,), ssq)
    _accumulate_inv_rms(r_ref, ssq, j == 0, j == pl.num_programs(1) - 1, width)


def _proj_residual(x, w, res, tm, tn):
    M, K = x.shape
    N = w.shape[1]
    tile = pl.BlockSpec((tm, tn), lambda i, j: (i, j))
    return pl.pallas_call(
        functools.partial(_proj_residual_kernel, width=N),
        grid=(M // tm, N // tn),
        in_specs=[_row_tile_spec(tm, K),
                  pl.BlockSpec((K, tn), lambda i, j: (0, j)),
                  tile],
        out_specs=[tile, tile, pl.BlockSpec((tm, LANES), lambda i, j: (i, 0))],
        out_shape=[jax.ShapeDtypeStruct((M, N), _F32), jax.ShapeDtypeStruct((M, N), _BF16),
                   jax.ShapeDtypeStruct((M, LANES), _F32)],
        compiler_params=_params("arbitrary", "arbitrary"),
        name="proj_residual",
    )(x, w, res)


def _gate_up_kernel(x_ref, r_ref, wg_ref, wu_ref, gain_ref, *rest):
    o_ref = rest[-2] if len(rest) == 3 else rest[0]
    if len(rest) == 3:
        rest[2][...] = rest[0][...].astype(_BF16)
    x = x_ref[...]
    tn = o_ref.shape[1]
    gain = _rep(gain_ref[...], tn)
    r = _rep(r_ref[...], tn)
    g = r * jnp.dot(x, (wg_ref[...] * gain).astype(_BF16), preferred_element_type=_F32)
    u = r * jnp.dot(x, (wu_ref[...] * gain).astype(_BF16), preferred_element_type=_F32)
    o_ref[...] = (g * jax.nn.sigmoid(g) * u).astype(o_ref.dtype)


def _gate_up(xb, inv_rms, wg, wu, layer, gain, w_side, tm, tn):
    M, K = xb.shape
    N = wg.shape[2]
    nj = N // tn
    steps = (M // tm) * nj
    _, R, C = w_side.shape
    if R % steps or (R // steps) % BF16_SUBLANES:
        side_out = _cast_bf16(w_side, layer)
        side_args, side_in, side_spec, side_shape = [], [], [], []
    else:
        rows = R // steps
        side_args = [w_side]
        side_in = [pl.BlockSpec((None, rows, C), lambda i, j: (layer, i * nj + j, 0))]
        side_spec = [pl.BlockSpec((rows, C), lambda i, j: (i * nj + j, 0))]
        side_shape = [jax.ShapeDtypeStruct((R, C), _BF16)]
    w_spec = pl.BlockSpec((None, K, tn), lambda i, j: (layer, 0, j))
    outs = pl.pallas_call(
        _gate_up_kernel,
        grid=(M // tm, nj),
        in_specs=[_row_tile_spec(tm, K),
                  pl.BlockSpec((tm, LANES), lambda i, j: (i, 0)),
                  w_spec, w_spec,
                  pl.BlockSpec((K, LANES), lambda i, j: (0, 0))] + side_in,
        out_specs=[pl.BlockSpec((tm, tn), lambda i, j: (i, j))] + side_spec,
        out_shape=[jax.ShapeDtypeStruct((M, N), _BF16)] + side_shape,
        compiler_params=_params("arbitrary", "arbitrary"),
        name="ffn_gate_up",
    )(xb, inv_rms, wg, wu, _lane_rep(gain), *side_args)
    return (outs[0], outs[1]) if side_args else (outs[0], side_out)


def _down_kernel(x_ref, w_ref, res_ref, o_ref, *rest, width):
    *stat_refs, acc_ref = rest
    k = pl.program_id(1)
    j = pl.program_id(2)

    def parts():
        for cols in _col_groups(o_ref.shape[1]):
            yield cols, jnp.dot(x_ref[...], w_ref[:, cols], preferred_element_type=_F32)

    @pl.when(k == 0)
    def _():
        for cols, part in parts():
            acc_ref[j, :, cols] = part

    @pl.when(k == pl.num_programs(1) - 1)
    def _():
        ssq = None
        for cols, part in parts():
            y = res_ref[:, cols] + acc_ref[j, :, cols] + part
            ssq = _store_residual_tile(cols, y, o_ref, stat_refs, ssq)
        if stat_refs:
            _accumulate_inv_rms(stat_refs[1], ssq, j == 0, j == pl.num_programs(2) - 1, width)


def _down_residual(x, w, res, with_stats, tm, tk, tn):
    M, K = x.shape
    N = w.shape[1]
    nk = K // tk
    assert nk == 2
    tile = pl.BlockSpec((tm, tn), lambda i, k, j: (i, j * k))
    out_specs, out_shape = [tile], [jax.ShapeDtypeStruct((M, N), _F32)]
    if with_stats:
        out_specs += [tile, pl.BlockSpec((tm, LANES), lambda i, k, j: (i, 0))]
        out_shape += [jax.ShapeDtypeStruct((M, N), _BF16), jax.ShapeDtypeStruct((M, LANES), _F32)]
    return pl.pallas_call(
        functools.partial(_down_kernel, width=N),
        grid=(M // tm, nk, N // tn),
        in_specs=[pl.BlockSpec((tm, tk), lambda i, k, j: (i, k)),
                  pl.BlockSpec((tk, tn), lambda i, k, j: (k, j)),
                  tile],
        out_specs=out_specs,
        out_shape=out_shape,
        scratch_shapes=[pltpu.VMEM((N // tn, tm, tn), _F32)],
        compiler_params=_params("arbitrary", "arbitrary", "arbitrary"),
        name="ffn_down_residual",
    )(x, w, res)


def _diff_attn_kernel(q_ref, k_ref, v_ref, td_ref, ts_ref, lam_ref, gsub_ref, o_ref,
                      bias_ref, sx_ref, sy_ref, sz_ref, mx_ref, my_ref, mz_ref, m_ref, l_ref, acc_ref,
                      *, t, stage_blocks, lambda_init):
    i = pl.program_id(2)
    nt = t // LANES

    @pl.when(i == 0)
    def _build_bias():
        zeros = jnp.zeros((LANES, LANES), _F32)
        neg = jnp.full((LANES, LANES), NEG, _F32)
        for c in range(2):
            for a in range(nt):
                for b in range(nt):
                    rows = slice(a * LANES, (a + 1) * LANES)
                    cols = slice(b * LANES, (b + 1) * LANES)
                    if a == b:
                        tile = td_ref[c]
                    elif a == b + 1:
                        tile = ts_ref[c]
                    else:
                        tile = zeros if a > b else neg
                    bias_ref[c, 0, rows, cols] = tile
                    bias_ref[c, 1, rows, cols] = ts_ref[c] if (a == 0 and b == nt - 1) else zeros
                    bias_ref[c, 2, rows, cols] = neg

    def stage_a(blocks, s_out, m_out):
        for c in range(2):
            cols = slice(c * HEAD_DIM, (c + 1) * HEAD_DIM)
            part = None
            for n, (j, kind) in enumerate(blocks):
                start = pl.multiple_of(j * t, t)
                s = lax.dot_general(q_ref[:, cols], k_ref[pl.ds(start, t), cols], _NT,
                                    preferred_element_type=_F32)
                if kind is not None:
                    s = s + bias_ref[c, kind]
                s_out[c, :, n * t:(n + 1) * t] = s
                block_max = functools.reduce(jnp.maximum, _lane_tiles(s))
                part = block_max if part is None else jnp.maximum(part, block_max)
            m_out[c] = jnp.broadcast_to(jnp.max(part, axis=-1, keepdims=True), (t, LANES))

    def stage_b(blocks, s_in, m_in, first=False):
        for c in range(2):
            if first:
                m_new = m_in[c]
                psum = jnp.zeros((t, LANES), _F32)
                pv = jnp.zeros(acc_ref.shape[1:], _F32)
            else:
                m_old = m_ref[c]
                m_new = jnp.maximum(m_old, m_in[c])
                alpha = jnp.exp2(m_old - m_new)
                psum = alpha * l_ref[c]
                pv = _rep(alpha, acc_ref.shape[2]) * acc_ref[c]
            m_rep = _rep(m_new, MXU_COLS)
            for n, (j, _) in enumerate(blocks):
                for u in range(t // MXU_COLS):
                    off = u * MXU_COLS
                    p = jnp.exp2(s_in[c, :, n * t + off:n * t + off + MXU_COLS] - m_rep)
                    psum = psum + functools.reduce(jnp.add, _lane_tiles(p))
                    v_rows = v_ref[pl.ds(pl.multiple_of(j * t + off, MXU_COLS), MXU_COLS), :]
                    pv = pv + jnp.dot(p.astype(_BF16), v_rows, preferred_element_type=_F32)
            l_ref[c] = psum
            acc_ref[c] = pv
            m_ref[c] = m_new

    g = stage_blocks
    j_sub = jnp.maximum(i - 1, 0)
    n_far = jnp.maximum(i - 1, 0) // g
    left_over = jnp.maximum(i - 1, 0) % g
    near = [(i, 0), (j_sub, jnp.where(i == 0, 2, 1))] + [(jnp.maximum(i - 2 - n, 0), None) for n in range(g - 1)]

    def far(f):
        return [(g * f + n, None) for n in range(g)]

    for width in range(2, g + 2):
        is_width = left_over == width - 2

        @pl.when(is_width)
        def _(width=width):
            stage_a(near[:width], sz_ref, mz_ref)

        @pl.when(jnp.logical_and(is_width, n_far == 0))
        def _(width=width):
            stage_b(near[:width], sz_ref, mz_ref, first=True)

        @pl.when(jnp.logical_and(is_width, n_far > 0))
        def _(width=width):
            stage_a(far(0), sy_ref, my_ref)
            stage_b(near[:width], sz_ref, mz_ref, first=True)

    n_rest = jnp.maximum(n_far - 1, 0)
    n_pairs = n_rest // 2

    def pair_body(u, carry):
        stage_a(far(2 * u + 1), sx_ref, mx_ref)
        stage_b(far(2 * u), sy_ref, my_ref)
        stage_a(far(2 * u + 2), sy_ref, my_ref)
        stage_b(far(2 * u + 1), sx_ref, mx_ref)
        return carry

    lax.fori_loop(0, n_pairs, pair_body, 0)

    @pl.when(jnp.logical_and(n_far > 0, n_rest % 2 == 1))
    def _():
        stage_a(far(n_far - 1), sx_ref, mx_ref)
        stage_b(far(2 * n_pairs), sy_ref, my_ref)
        stage_b(far(n_far - 1), sx_ref, mx_ref)

    @pl.when(jnp.logical_and(n_far > 0, n_rest % 2 == 0))
    def _():
        stage_b(far(2 * n_pairs), sy_ref, my_ref)

    lf = lam_ref[...]
    lam = (jnp.exp(jnp.sum(lf[0:1] * lf[1:2], axis=-1, keepdims=True))
           - jnp.exp(jnp.sum(lf[2:3] * lf[3:4], axis=-1, keepdims=True)) + lambda_init)
    l0 = jnp.sum(l_ref[0], axis=-1, keepdims=True)
    l1 = jnp.sum(l_ref[1], axis=-1, keepdims=True)
    o = acc_ref[0] / l0 - lam * (acc_ref[1] / l1)
    o = o * lax.rsqrt(jnp.mean(o * o, axis=-1, keepdims=True) + EPS)
    o_ref[...] = (o * gsub_ref[...] * (1.0 - lambda_init)).astype(o_ref.dtype)


def _diff_attention(qkv, td, ts, lam_qk, g_sub, B, S, D, t, stage_blocks, lambda_init):
    n_heads = D // (2 * HEAD_DIM)
    nq = S // t
    w = 2 * HEAD_DIM
    kern = functools.partial(_diff_attn_kernel, t=t, stage_blocks=stage_blocks, lambda_init=lambda_init)
    return pl.pallas_call(
        kern,
        grid=(B, n_heads, nq),
        in_specs=[pl.BlockSpec((t, w), lambda b, h, i: (b * nq + i, h)),
                  pl.BlockSpec((S, w), lambda b, h, i: (b, n_heads + h)),
                  pl.BlockSpec((S, w), lambda b, h, i: (b, 2 * n_heads + h)),
                  pl.BlockSpec((2, LANES, LANES), lambda b, h, i: (h, 0, 0)),
                  pl.BlockSpec((2, LANES, LANES), lambda b, h, i: (h, 0, 0)),
                  pl.BlockSpec((4, HEAD_DIM), lambda b, h, i: (0, 0)),
                  pl.BlockSpec((1, w), lambda b, h, i: (0, 0))],
        out_specs=pl.BlockSpec((t, w), lambda b, h, i: (b * nq + i, h)),
        out_shape=jax.ShapeDtypeStruct((B * S, D), _BF16),
        scratch_shapes=[pltpu.VMEM((2, 3, t, t), _F32),
                        pltpu.VMEM((2, t, stage_blocks * t), _F32),
                        pltpu.VMEM((2, t, stage_blocks * t), _F32),
                        pltpu.VMEM((2, t, (stage_blocks + 1) * t), _F32),
                        pltpu.VMEM((2, t, LANES), _F32),
                        pltpu.VMEM((2, t, LANES), _F32),
                        pltpu.VMEM((2, t, LANES), _F32),
                        pltpu.VMEM((2, t, LANES), _F32),
                        pltpu.VMEM((2, t, LANES), _F32),
                        pltpu.VMEM((2, t, w), _F32)],
        compiler_params=_params("arbitrary", "arbitrary", "arbitrary"),
        name="diff_attention",
    )(qkv, qkv, qkv, td, ts, lam_qk, g_sub)


def _swa_kernel(q_ref, kc_ref, kp_ref, vc_ref, vp_ref, bias_ref, sink_ref, o_ref, *, tq):
    first = pl.program_id(2) == 0
    rows = SWA_GROUP * WINDOW
    bias = bias_ref[...].reshape(rows, 2 * WINDOW)
    sink = jnp.concatenate(
        [jnp.broadcast_to(sink_ref[r:r + 1, :], (WINDOW, LANES)) for r in range(SWA_GROUP)], axis=0)
    col = lax.broadcasted_iota(jnp.int32, (rows, 2 * WINDOW), 1)
    ones = jnp.ones((2 * WINDOW, LANES), _BF16)
    for c in range(tq // WINDOW):
        cur = slice(c * WINDOW, (c + 1) * WINDOW)
        if c == 0:
            k_prev, v_prev = kp_ref[...], vp_ref[...]
        else:
            prev = slice((c - 1) * WINDOW, c * WINDOW)
            k_prev, v_prev = kc_ref[prev, :], vc_ref[prev, :]
        kk = jnp.concatenate([k_prev, kc_ref[cur, :]], axis=0)
        vv = jnp.concatenate([jnp.concatenate([v_prev, vc_ref[cur, :]], axis=0), ones], axis=1)
        q4 = jnp.concatenate(
            [q_ref[cur, r * HEAD_DIM:(r + 1) * HEAD_DIM] for r in range(SWA_GROUP)], axis=0)
        s = lax.dot_general(q4, kk, _NT, preferred_element_type=_F32) + bias
        if c == 0:
            s = jnp.where(first & (col < WINDOW), NEG, s)
        part = functools.reduce(jnp.maximum, _lane_tiles(s))
        m = jnp.maximum(jnp.broadcast_to(jnp.max(part, axis=-1, keepdims=True), (rows, LANES)), sink)
        p = jnp.exp2(s - jnp.concatenate([m, m], axis=1))
        pv = jnp.dot(p.astype(_BF16), vv, preferred_element_type=_F32)
        o = pv[:, :HEAD_DIM] / (pv[:, HEAD_DIM:] + jnp.exp2(sink - m))
        for r in range(SWA_GROUP):
            o_ref[cur, r * HEAD_DIM:(r + 1) * HEAD_DIM] = (
                o[r * WINDOW:(r + 1) * WINDOW, :].astype(o_ref.dtype))


def _swa_attention(q, kv, bias, sinks, B, S, D, tq):
    n_kv = D // (SWA_GROUP * HEAD_DIM)
    nt = S // tq
    bpt = tq // WINDOW
    gw = SWA_GROUP * HEAD_DIM

    def prev_map(off):
        return lambda b, g, t: (jnp.maximum((b * nt + t) * bpt - 1, 0), off + g)

    kern = functools.partial(_swa_kernel, tq=tq)
    return pl.pallas_call(
        kern,
        grid=(B, n_kv, nt),
        in_specs=[pl.BlockSpec((tq, gw), lambda b, g, t: (b * nt + t, g)),
                  pl.BlockSpec((tq, HEAD_DIM), lambda b, g, t: (b * nt + t, g)),
                  pl.BlockSpec((WINDOW, HEAD_DIM), prev_map(0)),
                  pl.BlockSpec((tq, HEAD_DIM), lambda b, g, t: (b * nt + t, n_kv + g)),
                  pl.BlockSpec((WINDOW, HEAD_DIM), prev_map(n_kv)),
                  pl.BlockSpec((SWA_GROUP, WINDOW, 2 * WINDOW), lambda b, g, t: (g, 0, 0)),
                  pl.BlockSpec((None, SWA_GROUP, LANES), lambda b, g, t: (g, 0, 0))],
        out_specs=pl.BlockSpec((tq, gw), lambda b, g, t: (b * nt + t, g)),
        out_shape=jax.ShapeDtypeStruct((B * S, D), _BF16),
        compiler_params=_params("arbitrary", "arbitrary", "arbitrary"),
        name="swa_attention",
    )(q, kv, kv, kv, kv, bias, sinks)


def _t5_bucket(n):
    max_exact = NUM_BUCKETS // 2
    nf = jnp.maximum(n, 1).astype(_F32)
    large = max_exact + (jnp.log(nf / max_exact) / math.log(MAX_DISTANCE / max_exact)
                         * (NUM_BUCKETS - max_exact)).astype(jnp.int32)
    large = jnp.minimum(large, NUM_BUCKETS - 1)
    return jnp.where(n < max_exact, n, large)


def _bias_tiles(rel_bias):
    def lookup(table, idx):
        onehot = (idx[..., None] == jnp.arange(table.shape[0])).astype(_F32)
        return jnp.einsum("...d,dh->h...", onehot, table, precision=lax.Precision.HIGHEST)

    lut = lookup(rel_bias.astype(_F32), _t5_bucket(jnp.arange(MAX_DISTANCE))).T * LOG2E
    shifted = lut - lut[MAX_DISTANCE - 1]
    qi = jnp.arange(LANES)[:, None]
    kj = jnp.arange(LANES)[None, :]
    clip = lambda dist: jnp.clip(dist, 0, MAX_DISTANCE - 1)
    d0 = qi - kj
    td = jnp.where((d0 >= 0)[None], lookup(shifted, clip(d0)), NEG)
    ts = lookup(shifted, clip(LANES + qi - kj))
    dw = WINDOW + qi - jnp.arange(2 * WINDOW)[None, :]
    swa = jnp.where(((dw >= 0) & (dw < WINDOW))[None], lookup(lut, clip(dw)), NEG)
    return td, ts, swa


def _forward(x, rel_bias, g_attn_norm, g_ffn_norm, w_qkv_a, w_o_a, g_q_a, g_k_a, lam_qk_a, g_sub_a,
             g_kv_norm, w_kv, g_k_shared, w_q_b, w_o_b, g_q_b, sinks_b, w_gate, w_up, w_down, tiles):
    B, S, D = x.shape
    M = B * S
    T = tiles
    q_scale = HEAD_DIM ** -0.5 * LOG2E
    n_maps = D // HEAD_DIM
    kvw = w_kv.shape[1] // 2
    td, ts, swa_bias = _bias_tiles(rel_bias)
    bf = _cast_bf16
    x = x.reshape(M, D)

    def ffn(x, xb, inv_rms, layer, with_stats):
        act, w_down_bf16 = _gate_up(xb, inv_rms, w_gate, w_up, layer, g_ffn_norm[layer], w_down,
                                    T["mm_tm"], T["gu_tn"])
        return _down_residual(act, w_down_bf16, x, with_stats, T["down_tm"], T["down_tk"], T["down_tn"])

    lambda_init = 0.8 - 0.6 * math.exp(-0.3 * 0)
    xb, inv_rms = _prep(x, T["rms_tm"])
    g_cols = jnp.concatenate([jnp.tile(g_q_a[0] * q_scale, n_maps), jnp.tile(g_k_a[0], n_maps),
                              jnp.ones((D,), _F32)])[None]
    qkv, (w_o_a_bf, w_kv_bf, w_q_b_bf, w_o_b_bf) = _proj_headnorm(
        xb, inv_rms, bf(w_qkv_a, 0, g_attn_norm[0]), g_cols, 2 * D, T["mm_tm"], T["qkv_tn"],
        [(w_o_a, 0, None), (w_kv[None], 0, g_kv_norm), (w_q_b, 0, g_attn_norm[1]), (w_o_b, 0, None)])
    o = _diff_attention(qkv, td, ts, lam_qk_a[0], g_sub_a[0][None], B, S, D, T["attn_t"], T["attn_stage_blocks"],
                        lambda_init)
    x, xb, inv_rms = _proj_residual(o, w_o_a_bf, x, T["mm_tm"], T["res_tn"])
    x, xb, inv_rms = ffn(x, xb, inv_rms, 0, True)

    g_kv_cols = jnp.concatenate([jnp.tile(g_k_shared, kvw // HEAD_DIM), jnp.ones((kvw,), _F32)])[None]
    kv, _ = _proj_headnorm(xb, inv_rms, w_kv_bf, g_kv_cols, kvw, T["mm_tm"], min(T["qkv_tn"], kvw))
    q, _ = _proj_headnorm(xb, inv_rms, w_q_b_bf, jnp.tile(g_q_b[0] * q_scale, n_maps)[None],
                          D, T["mm_tm"], T["qkv_tn"])
    sinks = jnp.broadcast_to((sinks_b[0].astype(_F32) * LOG2E).reshape(-1, SWA_GROUP, 1),
                             (n_maps // SWA_GROUP, SWA_GROUP, LANES))
    o = _swa_attention(q, kv, swa_bias, sinks, B, S, D, T["swa_tq"])
    x, xb, inv_rms = _proj_residual(o, w_o_b_bf, x, T["mm_tm"], T["res_tn"])
    (x,) = ffn(x, xb, inv_rms, 1, False)
    return x.reshape(B, S, D)


def kernel(x, rel_bias, g_attn_norm, g_ffn_norm, w_qkv_a, w_o_a, g_q_a, g_k_a, lam_qk_a, g_sub_a,
           g_kv_norm, w_kv, g_k_shared, w_q_b, w_o_b, g_q_b, sinks_b, w_gate, w_up, w_down):
    B, S, D = x.shape
    tiles = _tiles(B * S, S, D, w_gate.shape[-1])
    return _forward(x, rel_bias, g_attn_norm, g_ffn_norm, w_qkv_a, w_o_a, g_q_a, g_k_a, lam_qk_a,
                    g_sub_a, g_kv_norm, w_kv, g_k_shared, w_q_b, w_o_b, g_q_b, sinks_b,
                    w_gate, w_up, w_down, tiles)
```

```python
import functools
import math

import jax
import jax.numpy as jnp
from jax import lax
from jax.experimental import pallas as pl
from jax.experimental.pallas import tpu as pltpu

HEAD_DIM = 128
WINDOW = 128
SWA_GROUP = 4
NUM_BUCKETS = 32
MAX_DISTANCE = 128
EPS = 1e-6
NEG = -1e30
LOG2E = math.log2(math.e)
LANES = 128
VMEM_LIMIT_BYTES = 56 * 1024 * 1024
CAST_BLOCK_BYTES = 8 * 1024 * 1024
MXU_COLS = 256
BF16_SUBLANES = 16
WEIGHT_BUFFERS = 3

_F32 = jnp.float32
_BF16 = jnp.bfloat16
_NT = (((1,), (1,)), ((), ()))


def _params(*sem):
    return pltpu.CompilerParams(dimension_semantics=sem, vmem_limit_bytes=VMEM_LIMIT_BYTES)


def _tiles(M, S, D, D_FF):
    return dict(
        rms_tm=min(256, M),
        mm_tm=min(1024, M),
        qkv_tn=min(1024, D),
        res_tn=min(512, D),
        gu_tn=min(256, D_FF),
        down_tm=min(512, M),
        down_tk=D_FF // 2,
        down_tn=min(1024, D),
        attn_t=min(512, S // 4),
        attn_stage_blocks=3,
        swa_tq=min(1024, S),
    )


def _lane_tiles(a):
    return [a[:, u * LANES:(u + 1) * LANES] for u in range(a.shape[1] // LANES)]


def _rep(a, width):
    return a if width == LANES else jnp.concatenate([a] * (width // LANES), axis=1)


def _lane_rep(v):
    return jnp.broadcast_to(v.astype(_F32)[:, None], (v.shape[0], LANES))


def _cast_kernel(w_ref, *rest):
    o_ref = rest[-1]
    w = w_ref[...]
    if len(rest) == 2:
        w = w * _rep(rest[0][...], w.shape[1])
    o_ref[...] = w.astype(o_ref.dtype)


def _cast_bf16(w, layer, row_gain=None):
    _, R, C = w.shape
    tr = min(R, max(8, CAST_BLOCK_BYTES // (4 * C) // 8 * 8))
    while R % tr:
        tr -= 8
    args, specs = [w], [pl.BlockSpec((None, tr, C), lambda i: (layer, i, 0))]
    if row_gain is not None:
        args.append(_lane_rep(row_gain))
        specs.append(pl.BlockSpec((tr, LANES), lambda i: (i, 0)))
    return pl.pallas_call(
        _cast_kernel,
        grid=(R // tr,),
        in_specs=specs,
        out_specs=pl.BlockSpec((tr, C), lambda i: (i, 0)),
        out_shape=jax.ShapeDtypeStruct((R, C), _BF16),
        compiler_params=_params("arbitrary"),
        name="cast_bf16",
    )(*args)


def _prep_kernel(x_ref, xb_ref, r_ref):
    x = x_ref[...]
    xb_ref[...] = x.astype(xb_ref.dtype)
    r_ref[...] = jnp.broadcast_to(lax.rsqrt(jnp.mean(x * x, axis=-1, keepdims=True) + EPS), r_ref.shape)


def _prep(x, tm):
    M, D = x.shape
    return pl.pallas_call(
        _prep_kernel,
        grid=(M // tm,),
        in_specs=[pl.BlockSpec((tm, D), lambda i: (i, 0))],
        out_specs=[pl.BlockSpec((tm, D), lambda i: (i, 0)), pl.BlockSpec((tm, LANES), lambda i: (i, 0))],
        out_shape=[jax.ShapeDtypeStruct((M, D), _BF16), jax.ShapeDtypeStruct((M, LANES), _F32)],
        compiler_params=_params("arbitrary"),
        name="rms_stats",
    )(x)


def _accumulate_inv_rms(r_ref, ssq, first, last, width):
    @pl.when(first)
    def _():
        r_ref[...] = ssq

    @pl.when(jnp.logical_not(first))
    def _():
        r_ref[...] += ssq

    @pl.when(last)
    def _():
        total = jnp.sum(r_ref[...], axis=-1, keepdims=True)
        r_ref[...] = jnp.broadcast_to(lax.rsqrt(total / width + EPS), r_ref.shape)


def _row_tile_spec(tm, K):
    return pl.BlockSpec((tm, K), lambda i, j: (i, 0))


def _col_groups(width):
    sub = min(MXU_COLS, width)
    return [slice(u * sub, (u + 1) * sub) for u in range(width // sub)]


def _proj_headnorm_kernel(x_ref, r_ref, w_ref, g_ref, o_ref, *, n_norm_tiles, n_tiles):
    def sub_dots():
        for cols in _col_groups(o_ref.shape[1]):
            y = jnp.dot(x_ref[...], w_ref[:, cols], preferred_element_type=_F32)
            yield cols, _rep(r_ref[...], y.shape[1]) * y

    def normed():
        for cols, y in sub_dots():
            for c in range(y.shape[1] // HEAD_DIM):
                yc = y[:, c * HEAD_DIM:(c + 1) * HEAD_DIM]
                sl = slice(cols.start + c * HEAD_DIM, cols.start + (c + 1) * HEAD_DIM)
                r = lax.rsqrt(jnp.mean(yc * yc, axis=-1, keepdims=True) + EPS)
                o_ref[:, sl] = (yc * r * g_ref[:, sl]).astype(o_ref.dtype)

    def plain():
        for cols, y in sub_dots():
            o_ref[:, cols] = y.astype(o_ref.dtype)

    if n_norm_tiles == n_tiles:
        normed()
    else:
        j = pl.program_id(1)
        pl.when(j < n_norm_tiles)(normed)
        pl.when(j >= n_norm_tiles)(plain)


def _proj_headnorm(xb, inv_rms, w, g_cols, n_norm_cols, tm, tn):
    M, K = xb.shape
    N = w.shape[1]
    kern = functools.partial(_proj_headnorm_kernel, n_norm_tiles=n_norm_cols // tn, n_tiles=N // tn)
    return pl.pallas_call(
        kern,
        grid=(M // tm, N // tn),
        in_specs=[_row_tile_spec(tm, K),
                  pl.BlockSpec((tm, LANES), lambda i, j: (i, 0)),
                  pl.BlockSpec((K, tn), lambda i, j: (0, j)),
                  pl.BlockSpec((1, tn), lambda i, j: (0, j))],
        out_specs=pl.BlockSpec((tm, tn), lambda i, j: (i, j)),
        out_shape=jax.ShapeDtypeStruct((M, N), _BF16),
        compiler_params=_params("arbitrary", "arbitrary"),
        name="proj_headnorm",
    )(xb, inv_rms, w, g_cols)


def _store_residual_tile(cols, y, o_ref, stat_refs, ssq):
    o_ref[:, cols] = y
    if not stat_refs:
        return None
    stat_refs[0][:, cols] = y.astype(_BF16)
    part = functools.reduce(jnp.add, _lane_tiles(y * y))
    return part if ssq is None else ssq + part


def _proj_residual_kernel(x_ref, w_ref, res_ref, o_ref, ob_ref, r_ref, *, width):
    j = pl.program_id(1)
    ssq = None
    for cols in _col_groups(o_ref.shape[1]):
        y = res_ref[:, cols] + jnp.dot(x_ref[...], w_ref[:, cols], preferred_element_type=_F32)
        ssq = _store_residual_tile(cols, y, o_ref, (ob_ref,), ssq)
    _accumulate_inv_rms(r_ref, ssq, j == 0, j == pl.num_programs(1) - 1, width)


def _proj_residual(x, w, res, tm, tn):
    M, K = x.shape
    N = w.shape[1]
    tile = pl.BlockSpec((tm, tn), lambda i, j: (i, j))
    return pl.pallas_call(
        functools.partial(_proj_residual_kernel, width=N),
        grid=(M // tm, N // tn),
        in_specs=[_row_tile_spec(tm, K),
                  pl.BlockSpec((K, tn), lambda i, j: (0, j)),
                  tile],
        out_specs=[tile, tile, pl.BlockSpec((tm, LANES), lambda i, j: (i, 0))],
        out_shape=[jax.ShapeDtypeStruct((M, N), _F32), jax.ShapeDtypeStruct((M, N), _BF16),
                   jax.ShapeDtypeStruct((M, LANES), _F32)],
        compiler_params=_params("arbitrary", "arbitrary"),
        name="proj_residual",
    )(x, w, res)


def _gate_up_kernel(x_ref, r_ref, wg_ref, wu_ref, gain_ref, *rest):
    o_ref = rest[-2] if len(rest) == 3 else rest[0]
    if len(rest) == 3:
        rest[2][...] = rest[0][...].astype(_BF16)
    x = x_ref[...]
    tn = o_ref.shape[1]
    gain = _rep(gain_ref[...], tn)
    r = _rep(r_ref[...], tn)
    g = r * jnp.dot(x, (wg_ref[...] * gain).astype(_BF16), preferred_element_type=_F32)
    u = r * jnp.dot(x, (wu_ref[...] * gain).astype(_BF16), preferred_element_type=_F32)
    o_ref[...] = (g * jax.nn.sigmoid(g) * u).astype(o_ref.dtype)


def _gate_up(xb, inv_rms, wg, wu, layer, gain, w_side, tm, tn):
    M, K = xb.shape
    N = wg.shape[2]
    nj = N // tn
    steps = (M // tm) * nj
    _, R, C = w_side.shape
    if R % steps or (R // steps) % BF16_SUBLANES:
        side_out = _cast_bf16(w_side, layer)
        side_args, side_in, side_spec, side_shape = [], [], [], []
    else:
        rows = R // steps
        side_args = [w_side]
        side_in = [pl.BlockSpec((None, rows, C), lambda i, j: (layer, i * nj + j, 0))]
        side_spec = [pl.BlockSpec((rows, C), lambda i, j: (i * nj + j, 0))]
        side_shape = [jax.ShapeDtypeStruct((R, C), _BF16)]
    w_spec = pl.BlockSpec((None, K, tn), lambda i, j: (layer, 0, j), pipeline_mode=pl.Buffered(WEIGHT_BUFFERS))
    in_specs = [_row_tile_spec(tm, K),
                pl.BlockSpec((tm, LANES), lambda i, j: (i, 0)),
                w_spec, w_spec,
                pl.BlockSpec((K, LANES), lambda i, j: (0, 0))] + side_in
    out_specs = [pl.BlockSpec((tm, tn), lambda i, j: (i, j))] + side_spec

    def pipelined(*refs):
        pltpu.emit_pipeline(_gate_up_kernel, grid=(M // tm, nj), in_specs=in_specs, out_specs=out_specs)(*refs)

    any_spec = pl.BlockSpec(memory_space=pl.ANY)
    outs = pl.pallas_call(
        pipelined,
        in_specs=[any_spec] * len(in_specs),
        out_specs=[any_spec] * len(out_specs),
        out_shape=[jax.ShapeDtypeStruct((M, N), _BF16)] + side_shape,
        compiler_params=pltpu.CompilerParams(vmem_limit_bytes=VMEM_LIMIT_BYTES),
        name="ffn_gate_up",
    )(xb, inv_rms, wg, wu, _lane_rep(gain), *side_args)
    return (outs[0], outs[1]) if side_args else (outs[0], side_out)


def _down_kernel(x_ref, w_ref, res_ref, o_ref, *rest, width):
    *stat_refs, acc_ref = rest
    k = pl.program_id(1)
    j = pl.program_id(2)

    def parts():
        for cols in _col_groups(o_ref.shape[1]):
            yield cols, jnp.dot(x_ref[...], w_ref[:, cols], preferred_element_type=_F32)

    @pl.when(k == 0)
    def _():
        for cols, part in parts():
            acc_ref[j, :, cols] = part

    @pl.when(k == pl.num_programs(1) - 1)
    def _():
        ssq = None
        for cols, part in parts():
            y = res_ref[:, cols] + acc_ref[j, :, cols] + part
            ssq = _store_residual_tile(cols, y, o_ref, stat_refs, ssq)
        if stat_refs:
            _accumulate_inv_rms(stat_refs[1], ssq, j == 0, j == pl.num_programs(2) - 1, width)


def _down_residual(x, w, res, with_stats, tm, tk, tn):
    M, K = x.shape
    N = w.shape[1]
    nk = K // tk
    assert nk == 2
    tile = pl.BlockSpec((tm, tn), lambda i, k, j: (i, j * k))
    out_specs, out_shape = [tile], [jax.ShapeDtypeStruct((M, N), _F32)]
    if with_stats:
        out_specs += [tile, pl.BlockSpec((tm, LANES), lambda i, k, j: (i, 0))]
        out_shape += [jax.ShapeDtypeStruct((M, N), _BF16), jax.ShapeDtypeStruct((M, LANES), _F32)]
    return pl.pallas_call(
        functools.partial(_down_kernel, width=N),
        grid=(M // tm, nk, N // tn),
        in_specs=[pl.BlockSpec((tm, tk), lambda i, k, j: (i, k)),
                  pl.BlockSpec((tk, tn), lambda i, k, j: (k, j)),
                  tile],
        out_specs=out_specs,
        out_shape=out_shape,
        scratch_shapes=[pltpu.VMEM((N // tn, tm, tn), _F32)],
        compiler_params=_params("arbitrary", "arbitrary", "arbitrary"),
        name="ffn_down_residual",
    )(x, w, res)


def _diff_attn_kernel(q_ref, k_ref, v_ref, td_ref, ts_ref, lam_ref, gsub_ref, o_ref,
                      bias_ref, sx_ref, sy_ref, sz_ref, mx_ref, my_ref, mz_ref, m_ref, l_ref, acc_ref,
                      *, t, stage_blocks, lambda_init):
    i = pl.program_id(2)
    nt = t // LANES

    @pl.when(i == 0)
    def _build_bias():
        zeros = jnp.zeros((LANES, LANES), _F32)
        neg = jnp.full((LANES, LANES), NEG, _F32)
        for c in range(2):
            for a in range(nt):
                for b in range(nt):
                    rows = slice(a * LANES, (a + 1) * LANES)
                    cols = slice(b * LANES, (b + 1) * LANES)
                    if a == b:
                        tile = td_ref[c]
                    elif a == b + 1:
                        tile = ts_ref[c]
                    else:
                        tile = zeros if a > b else neg
                    bias_ref[c, 0, rows, cols] = tile
                    bias_ref[c, 1, rows, cols] = ts_ref[c] if (a == 0 and b == nt - 1) else zeros
                    bias_ref[c, 2, rows, cols] = neg

    def stage_a(blocks, s_out, m_out):
        for c in range(2):
            cols = slice(c * HEAD_DIM, (c + 1) * HEAD_DIM)
            part = None
            for n, (j, kind) in enumerate(blocks):
                start = pl.multiple_of(j * t, t)
                s = lax.dot_general(q_ref[:, cols], k_ref[pl.ds(start, t), cols], _NT,
                                    preferred_element_type=_F32)
                if kind is not None:
                    s = s + bias_ref[c, kind]
                s_out[c, :, n * t:(n + 1) * t] = s
                block_max = functools.reduce(jnp.maximum, _lane_tiles(s))
                part = block_max if part is None else jnp.maximum(part, block_max)
            m_out[c] = jnp.broadcast_to(jnp.max(part, axis=-1, keepdims=True), (t, LANES))

    def stage_b(blocks, s_in, m_in, first=False):
        for c in range(2):
            if first:
                m_new = m_in[c]
                psum = jnp.zeros((t, LANES), _F32)
                pv = jnp.zeros(acc_ref.shape[1:], _F32)
            else:
                m_old = m_ref[c]
                m_new = jnp.maximum(m_old, m_in[c])
                alpha = jnp.exp2(m_old - m_new)
                psum = alpha * l_ref[c]
                pv = _rep(alpha, acc_ref.shape[2]) * acc_ref[c]
            m_rep = _rep(m_new, MXU_COLS)
            for n, (j, _) in enumerate(blocks):
                for u in range(t // MXU_COLS):
                    off = u * MXU_COLS
                    p = jnp.exp2(s_in[c, :, n * t + off:n * t + off + MXU_COLS] - m_rep)
                    psum = psum + functools.reduce(jnp.add, _lane_tiles(p))
                    v_rows = v_ref[pl.ds(pl.multiple_of(j * t + off, MXU_COLS), MXU_COLS), :]
                    pv = pv + jnp.dot(p.astype(_BF16), v_rows, preferred_element_type=_F32)
            l_ref[c] = psum
            acc_ref[c] = pv
            m_ref[c] = m_new

    g = stage_blocks
    j_sub = jnp.maximum(i - 1, 0)
    n_far = jnp.maximum(i - 1, 0) // g
    left_over = jnp.maximum(i - 1, 0) % g
    near = [(i, 0), (j_sub, jnp.where(i == 0, 2, 1))] + [(jnp.maximum(i - 2 - n, 0), None) for n in range(g - 1)]

    def far(f):
        return [(g * f + n, None) for n in range(g)]

    for width in range(2, g + 2):
        is_width = left_over == width - 2

        @pl.when(is_width)
        def _(width=width):
            stage_a(near[:width], sz_ref, mz_ref)

        @pl.when(jnp.logical_and(is_width, n_far == 0))
        def _(width=width):
            stage_b(near[:width], sz_ref, mz_ref, first=True)

        @pl.when(jnp.logical_and(is_width, n_far > 0))
        def _(width=width):
            stage_a(far(0), sy_ref, my_ref)
            stage_b(near[:width], sz_ref, mz_ref, first=True)

    n_rest = jnp.maximum(n_far - 1, 0)
    n_pairs = n_rest // 2

    def pair_body(u, carry):
        stage_a(far(2 * u + 1), sx_ref, mx_ref)
        stage_b(far(2 * u), sy_ref, my_ref)
        stage_a(far(2 * u + 2), sy_ref, my_ref)
        stage_b(far(2 * u + 1), sx_ref, mx_ref)
        return carry

    lax.fori_loop(0, n_pairs, pair_body, 0)

    @pl.when(jnp.logical_and(n_far > 0, n_rest % 2 == 1))
    def _():
        stage_a(far(n_far - 1), sx_ref, mx_ref)
        stage_b(far(2 * n_pairs), sy_ref, my_ref)
        stage_b(far(n_far - 1), sx_ref, mx_ref)

    @pl.when(jnp.logical_and(n_far > 0, n_rest % 2 == 0))
    def _():
        stage_b(far(2 * n_pairs), sy_ref, my_ref)

    lf = lam_ref[...]
    lam = (jnp.exp(jnp.sum(lf[0:1] * lf[1:2], axis=-1, keepdims=True))
           - jnp.exp(jnp.sum(lf[2:3] * lf[3:4], axis=-1, keepdims=True)) + lambda_init)
    l0 = jnp.sum(l_ref[0], axis=-1, keepdims=True)
    l1 = jnp.sum(l_ref[1], axis=-1, keepdims=True)
    o = acc_ref[0] / l0 - lam * (acc_ref[1] / l1)
    o = o * lax.rsqrt(jnp.mean(o * o, axis=-1, keepdims=True) + EPS)
    o_ref[...] = (o * gsub_ref[...] * (1.0 - lambda_init)).astype(o_ref.dtype)


def _diff_attention(qkv, td, ts, lam_qk, g_sub, B, S, D, t, stage_blocks, lambda_init):
    n_heads = D // (2 * HEAD_DIM)
    nq = S // t
    w = 2 * HEAD_DIM
    kern = functools.partial(_diff_attn_kernel, t=t, stage_blocks=stage_blocks, lambda_init=lambda_init)
    return pl.pallas_call(
        kern,
        grid=(B, n_heads, nq),
        in_specs=[pl.BlockSpec((t, w), lambda b, h, i: (b * nq + i, h)),
                  pl.BlockSpec((S, w), lambda b, h, i: (b, n_heads + h)),
                  pl.BlockSpec((S, w), lambda b, h, i: (b, 2 * n_heads + h)),
                  pl.BlockSpec((2, LANES, LANES), lambda b, h, i: (h, 0, 0)),
                  pl.BlockSpec((2, LANES, LANES), lambda b, h, i: (h, 0, 0)),
                  pl.BlockSpec((4, HEAD_DIM), lambda b, h, i: (0, 0)),
                  pl.BlockSpec((1, w), lambda b, h, i: (0, 0))],
        out_specs=pl.BlockSpec((t, w), lambda b, h, i: (b * nq + i, h)),
        out_shape=jax.ShapeDtypeStruct((B * S, D), _BF16),
        scratch_shapes=[pltpu.VMEM((2, 3, t, t), _F32),
                        pltpu.VMEM((2, t, stage_blocks * t), _F32),
                        pltpu.VMEM((2, t, stage_blocks * t), _F32),
                        pltpu.VMEM((2, t, (stage_blocks + 1) * t), _F32),
                        pltpu.VMEM((2, t, LANES), _F32),
                        pltpu.VMEM((2, t, LANES), _F32),
                        pltpu.VMEM((2, t, LANES), _F32),
                        pltpu.VMEM((2, t, LANES), _F32),
                        pltpu.VMEM((2, t, LANES), _F32),
                        pltpu.VMEM((2, t, w), _F32)],
        compiler_params=_params("arbitrary", "arbitrary", "arbitrary"),
        name="diff_attention",
    )(qkv, qkv, qkv, td, ts, lam_qk, g_sub)


def _swa_kernel(q_ref, kc_ref, kp_ref, vc_ref, vp_ref, bias_ref, sink_ref, o_ref, *, tq):
    first = pl.program_id(2) == 0
    rows = SWA_GROUP * WINDOW
    bias = bias_ref[...].reshape(rows, 2 * WINDOW)
    sink = jnp.concatenate(
        [jnp.broadcast_to(sink_ref[r:r + 1, :], (WINDOW, LANES)) for r in range(SWA_GROUP)], axis=0)
    col = lax.broadcasted_iota(jnp.int32, (rows, 2 * WINDOW), 1)
    ones = jnp.ones((2 * WINDOW, LANES), _BF16)
    for c in range(tq // WINDOW):
        cur = slice(c * WINDOW, (c + 1) * WINDOW)
        if c == 0:
            k_prev, v_prev = kp_ref[...], vp_ref[...]
        else:
            prev = slice((c - 1) * WINDOW, c * WINDOW)
            k_prev, v_prev = kc_ref[prev, :], vc_ref[prev, :]
        kk = jnp.concatenate([k_prev, kc_ref[cur, :]], axis=0)
        vv = jnp.concatenate([jnp.concatenate([v_prev, vc_ref[cur, :]], axis=0), ones], axis=1)
        q4 = jnp.concatenate(
            [q_ref[cur, r * HEAD_DIM:(r + 1) * HEAD_DIM] for r in range(SWA_GROUP)], axis=0)
        s = lax.dot_general(q4, kk, _NT, preferred_element_type=_F32) + bias
        if c == 0:
            s = jnp.where(first & (col < WINDOW), NEG, s)
        part = functools.reduce(jnp.maximum, _lane_tiles(s))
        m = jnp.maximum(jnp.broadcast_to(jnp.max(part, axis=-1, keepdims=True), (rows, LANES)), sink)
        p = jnp.exp2(s - jnp.concatenate([m, m], axis=1))
        pv = jnp.dot(p.astype(_BF16), vv, preferred_element_type=_F32)
        o = pv[:, :HEAD_DIM] / (pv[:, HEAD_DIM:] + jnp.exp2(sink - m))
        for r in range(SWA_GROUP):
            o_ref[cur, r * HEAD_DIM:(r + 1) * HEAD_DIM] = (
                o[r * WINDOW:(r + 1) * WINDOW, :].astype(o_ref.dtype))


def _swa_attention(q, kv, bias, sinks, B, S, D, tq):
    n_kv = D // (SWA_GROUP * HEAD_DIM)
    nt = S // tq
    bpt = tq // WINDOW
    gw = SWA_GROUP * HEAD_DIM

    def prev_map(off):
        return lambda b, g, t: (jnp.maximum((b * nt + t) * bpt - 1, 0), off + g)

    kern = functools.partial(_swa_kernel, tq=tq)
    return pl.pallas_call(
        kern,
        grid=(B, n_kv, nt),
        in_specs=[pl.BlockSpec((tq, gw), lambda b, g, t: (b * nt + t, g)),
                  pl.BlockSpec((tq, HEAD_DIM), lambda b, g, t: (b * nt + t, g)),
                  pl.BlockSpec((WINDOW, HEAD_DIM), prev_map(0)),
                  pl.BlockSpec((tq, HEAD_DIM), lambda b, g, t: (b * nt + t, n_kv + g)),
                  pl.BlockSpec((WINDOW, HEAD_DIM), prev_map(n_kv)),
                  pl.BlockSpec((SWA_GROUP, WINDOW, 2 * WINDOW), lambda b, g, t: (g, 0, 0)),
                  pl.BlockSpec((None, SWA_GROUP, LANES), lambda b, g, t: (g, 0, 0))],
        out_specs=pl.BlockSpec((tq, gw), lambda b, g, t: (b * nt + t, g)),
        out_shape=jax.ShapeDtypeStruct((B * S, D), _BF16),
        compiler_params=_params("arbitrary", "arbitrary", "arbitrary"),
        name="swa_attention",
    )(q, kv, kv, kv, kv, bias, sinks)


def _t5_bucket(n):
    max_exact = NUM_BUCKETS // 2
    nf = jnp.maximum(n, 1).astype(_F32)
    large = max_exact + (jnp.log(nf / max_exact) / math.log(MAX_DISTANCE / max_exact)
                         * (NUM_BUCKETS - max_exact)).astype(jnp.int32)
    large = jnp.minimum(large, NUM_BUCKETS - 1)
    return jnp.where(n < max_exact, n, large)


def _bias_tiles(rel_bias):
    def lookup(table, idx):
        onehot = (idx[..., None] == jnp.arange(table.shape[0])).astype(_F32)
        return jnp.einsum("...d,dh->h...", onehot, table, precision=lax.Precision.HIGHEST)

    lut = lookup(rel_bias.astype(_F32), _t5_bucket(jnp.arange(MAX_DISTANCE))).T * LOG2E
    shifted = lut - lut[MAX_DISTANCE - 1]
    qi = jnp.arange(LANES)[:, None]
    kj = jnp.arange(LANES)[None, :]
    clip = lambda dist: jnp.clip(dist, 0, MAX_DISTANCE - 1)
    d0 = qi - kj
    td = jnp.where((d0 >= 0)[None], lookup(shifted, clip(d0)), NEG)
    ts = lookup(shifted, clip(LANES + qi - kj))
    dw = WINDOW + qi - jnp.arange(2 * WINDOW)[None, :]
    swa = jnp.where(((dw >= 0) & (dw < WINDOW))[None], lookup(lut, clip(dw)), NEG)
    return td, ts, swa


def _forward(x, rel_bias, g_attn_norm, g_ffn_norm, w_qkv_a, w_o_a, g_q_a, g_k_a, lam_qk_a, g_sub_a,
             g_kv_norm, w_kv, g_k_shared, w_q_b, w_o_b, g_q_b, sinks_b, w_gate, w_up, w_down, tiles):
    B, S, D = x.shape
    M = B * S
    T = tiles
    q_scale = HEAD_DIM ** -0.5 * LOG2E
    n_maps = D // HEAD_DIM
    kvw = w_kv.shape[1] // 2
    td, ts, swa_bias = _bias_tiles(rel_bias)
    bf = _cast_bf16
    x = x.reshape(M, D)

    def ffn(x, xb, inv_rms, layer, with_stats):
        act, w_down_bf16 = _gate_up(xb, inv_rms, w_gate, w_up, layer, g_ffn_norm[layer], w_down,
                                    T["mm_tm"], T["gu_tn"])
        return _down_residual(act, w_down_bf16, x, with_stats, T["down_tm"], T["down_tk"], T["down_tn"])

    lambda_init = 0.8 - 0.6 * math.exp(-0.3 * 0)
    xb, inv_rms = _prep(x, T["rms_tm"])
    g_cols = jnp.concatenate([jnp.tile(g_q_a[0] * q_scale, n_maps), jnp.tile(g_k_a[0], n_maps),
                              jnp.ones((D,), _F32)])[None]
    qkv = _proj_headnorm(xb, inv_rms, bf(w_qkv_a, 0, g_attn_norm[0]), g_cols, 2 * D, T["mm_tm"], T["qkv_tn"])
    o = _diff_attention(qkv, td, ts, lam_qk_a[0], g_sub_a[0][None], B, S, D, T["attn_t"], T["attn_stage_blocks"],
                        lambda_init)
    x, xb, inv_rms = _proj_residual(o, bf(w_o_a, 0), x, T["mm_tm"], T["res_tn"])
    x, xb, inv_rms = ffn(x, xb, inv_rms, 0, True)

    g_kv_cols = jnp.concatenate([jnp.tile(g_k_shared, kvw // HEAD_DIM), jnp.ones((kvw,), _F32)])[None]
    kv = _proj_headnorm(xb, inv_rms, bf(w_kv[None], 0, g_kv_norm), g_kv_cols, kvw,
                        T["mm_tm"], min(T["qkv_tn"], kvw))
    q = _proj_headnorm(xb, inv_rms, bf(w_q_b, 0, g_attn_norm[1]), jnp.tile(g_q_b[0] * q_scale, n_maps)[None],
                       D, T["mm_tm"], T["qkv_tn"])
    sinks = jnp.broadcast_to((sinks_b[0].astype(_F32) * LOG2E).reshape(-1, SWA_GROUP, 1),
                             (n_maps // SWA_GROUP, SWA_GROUP, LANES))
    o = _swa_attention(q, kv, swa_bias, sinks, B, S, D, T["swa_tq"])
    x, xb, inv_rms = _proj_residual(o, bf(w_o_b, 0), x, T["mm_tm"], T["res_tn"])
    (x,) = ffn(x, xb, inv_rms, 1, False)
    return x.reshape(B, S, D)


def kernel(x, rel_bias, g_attn_norm, g_ffn_norm, w_qkv_a, w_o_a, g_q_a, g_k_a, lam_qk_a, g_sub_a,
           g_kv_norm, w_kv, g_k_shared, w_q_b, w_o_b, g_q_b, sinks_b, w_gate, w_up, w_down):
    B, S, D = x.shape
    tiles = _tiles(B * S, S, D, w_gate.shape[-1])
    return _forward(x, rel_bias, g_attn_norm, g_ffn_norm, w_qkv_a, w_o_a, g_q_a, g_k_a, lam_qk_a,
                    g_sub_a, g_kv_norm, w_kv, g_k_shared, w_q_b, w_o_b, g_q_b, sinks_b,
                    w_gate, w_up, w_down, tiles)
```

```python
import functools
import math

import jax
import jax.numpy as jnp
from jax import lax
from jax.experimental import pallas as pl
from jax.experimental.pallas import tpu as pltpu

HEAD_DIM = 128
WINDOW = 128
SWA_GROUP = 4
NUM_BUCKETS = 32
MAX_DISTANCE = 128
EPS = 1e-6
NEG = -1e30
LOG2E = math.log2(math.e)
LANES = 128
VMEM_LIMIT_BYTES = 56 * 1024 * 1024
CAST_BLOCK_BYTES = 8 * 1024 * 1024
MXU_COLS = 256
BF16_SUBLANES = 16
ATTN_ROW_GROUP = 256

_F32 = jnp.float32
_BF16 = jnp.bfloat16
_NT = (((1,), (1,)), ((), ()))


def _params(*sem):
    return pltpu.CompilerParams(dimension_semantics=sem, vmem_limit_bytes=VMEM_LIMIT_BYTES)


def _tiles(M, S, D, D_FF):
    return dict(
        rms_tm=min(256, M),
        mm_tm=min(1024, M),
        qkv_tn=min(1024, D),
        res_tn=min(512, D),
        gu_tn=min(256, D_FF),
        down_tm=min(512, M),
        down_tk=D_FF // 2,
        down_tn=min(1024, D),
        attn_t=min(512, S // 4),
        attn_stage_blocks=3,
        swa_tq=min(1024, S),
    )


def _lane_tiles(a):
    return [a[:, u * LANES:(u + 1) * LANES] for u in range(a.shape[1] // LANES)]


def _rep(a, width):
    return a if width == LANES else jnp.concatenate([a] * (width // LANES), axis=1)


def _lane_rep(v):
    return jnp.broadcast_to(v.astype(_F32)[:, None], (v.shape[0], LANES))


def _cast_kernel(w_ref, *rest):
    o_ref = rest[-1]
    w = w_ref[...]
    if len(rest) == 2:
        w = w * _rep(rest[0][...], w.shape[1])
    o_ref[...] = w.astype(o_ref.dtype)


def _cast_bf16(w, layer, row_gain=None):
    _, R, C = w.shape
    tr = min(R, max(8, CAST_BLOCK_BYTES // (4 * C) // 8 * 8))
    while R % tr:
        tr -= 8
    args, specs = [w], [pl.BlockSpec((None, tr, C), lambda i: (layer, i, 0))]
    if row_gain is not None:
        args.append(_lane_rep(row_gain))
        specs.append(pl.BlockSpec((tr, LANES), lambda i: (i, 0)))
    return pl.pallas_call(
        _cast_kernel,
        grid=(R // tr,),
        in_specs=specs,
        out_specs=pl.BlockSpec((tr, C), lambda i: (i, 0)),
        out_shape=jax.ShapeDtypeStruct((R, C), _BF16),
        compiler_params=_params("arbitrary"),
        name="cast_bf16",
    )(*args)


def _prep_kernel(x_ref, xb_ref, r_ref):
    x = x_ref[...]
    xb_ref[...] = x.astype(xb_ref.dtype)
    r_ref[...] = jnp.broadcast_to(lax.rsqrt(jnp.mean(x * x, axis=-1, keepdims=True) + EPS), r_ref.shape)


def _prep(x, tm):
    M, D = x.shape
    return pl.pallas_call(
        _prep_kernel,
        grid=(M // tm,),
        in_specs=[pl.BlockSpec((tm, D), lambda i: (i, 0))],
        out_specs=[pl.BlockSpec((tm, D), lambda i: (i, 0)), pl.BlockSpec((tm, LANES), lambda i: (i, 0))],
        out_shape=[jax.ShapeDtypeStruct((M, D), _BF16), jax.ShapeDtypeStruct((M, LANES), _F32)],
        compiler_params=_params("arbitrary"),
        name="rms_stats",
    )(x)


def _accumulate_inv_rms(r_ref, ssq, first, last, width):
    @pl.when(first)
    def _():
        r_ref[...] = ssq

    @pl.when(jnp.logical_not(first))
    def _():
        r_ref[...] += ssq

    @pl.when(last)
    def _():
        total = jnp.sum(r_ref[...], axis=-1, keepdims=True)
        r_ref[...] = jnp.broadcast_to(lax.rsqrt(total / width + EPS), r_ref.shape)


def _row_tile_spec(tm, K):
    return pl.BlockSpec((tm, K), lambda i, j: (i, 0))


def _col_groups(width):
    sub = min(MXU_COLS, width)
    return [slice(u * sub, (u + 1) * sub) for u in range(width // sub)]


def _proj_headnorm_kernel(x_ref, r_ref, w_ref, g_ref, o_ref, *, n_norm_tiles, n_tiles):
    def sub_dots():
        for cols in _col_groups(o_ref.shape[1]):
            y = jnp.dot(x_ref[...], w_ref[:, cols], preferred_element_type=_F32)
            yield cols, _rep(r_ref[...], y.shape[1]) * y

    def normed():
        for cols, y in sub_dots():
            for c in range(y.shape[1] // HEAD_DIM):
                yc = y[:, c * HEAD_DIM:(c + 1) * HEAD_DIM]
                sl = slice(cols.start + c * HEAD_DIM, cols.start + (c + 1) * HEAD_DIM)
                r = lax.rsqrt(jnp.mean(yc * yc, axis=-1, keepdims=True) + EPS)
                o_ref[:, sl] = (yc * r * g_ref[:, sl]).astype(o_ref.dtype)

    def plain():
        for cols, y in sub_dots():
            o_ref[:, cols] = y.astype(o_ref.dtype)

    if n_norm_tiles == n_tiles:
        normed()
    else:
        j = pl.program_id(1)
        pl.when(j < n_norm_tiles)(normed)
        pl.when(j >= n_norm_tiles)(plain)


def _proj_headnorm(xb, inv_rms, w, g_cols, n_norm_cols, tm, tn):
    M, K = xb.shape
    N = w.shape[1]
    kern = functools.partial(_proj_headnorm_kernel, n_norm_tiles=n_norm_cols // tn, n_tiles=N // tn)
    return pl.pallas_call(
        kern,
        grid=(M // tm, N // tn),
        in_specs=[_row_tile_spec(tm, K),
                  pl.BlockSpec((tm, LANES), lambda i, j: (i, 0)),
                  pl.BlockSpec((K, tn), lambda i, j: (0, j)),
                  pl.BlockSpec((1, tn), lambda i, j: (0, j))],
        out_specs=pl.BlockSpec((tm, tn), lambda i, j: (i, j)),
        out_shape=jax.ShapeDtypeStruct((M, N), _BF16),
        compiler_params=_params("arbitrary", "arbitrary"),
        name="proj_headnorm",
    )(xb, inv_rms, w, g_cols)


def _store_residual_tile(cols, y, o_ref, stat_refs, ssq):
    o_ref[:, cols] = y
    if not stat_refs:
        return None
    stat_refs[0][:, cols] = y.astype(_BF16)
    part = functools.reduce(jnp.add, _lane_tiles(y * y))
    return part if ssq is None else ssq + part


def _proj_residual_kernel(x_ref, w_ref, res_ref, o_ref, ob_ref, r_ref, *, width):
    j = pl.program_id(1)
    ssq = None
    for cols in _col_groups(o_ref.shape[1]):
        y = res_ref[:, cols] + jnp.dot(x_ref[...], w_ref[:, cols], preferred_element_type=_F32)
        ssq = _store_residual_tile(cols, y, o_ref, (ob_ref,), ssq)
    _accumulate_inv_rms(r_ref, ssq, j == 0, j == pl.num_programs(1) - 1, width)


def _proj_residual(x, w, res, tm, tn):
    M, K = x.shape
    N = w.shape[1]
    tile = pl.BlockSpec((tm, tn), lambda i, j: (i, j))
    return pl.pallas_call(
        functools.partial(_proj_residual_kernel, width=N),
        grid=(M // tm, N // tn),
        in_specs=[_row_tile_spec(tm, K),
                  pl.BlockSpec((K, tn), lambda i, j: (0, j)),
                  tile],
        out_specs=[tile, tile, pl.BlockSpec((tm, LANES), lambda i, j: (i, 0))],
        out_shape=[jax.ShapeDtypeStruct((M, N), _F32), jax.ShapeDtypeStruct((M, N), _BF16),
                   jax.ShapeDtypeStruct((M, LANES), _F32)],
        compiler_params=_params("arbitrary", "arbitrary"),
        name="proj_residual",
    )(x, w, res)


def _gate_up_kernel(x_ref, r_ref, wg_ref, wu_ref, gain_ref, *rest):
    o_ref = rest[-2] if len(rest) == 3 else rest[0]
    if len(rest) == 3:
        rest[2][...] = rest[0][...].astype(_BF16)
    x = x_ref[...]
    tn = o_ref.shape[1]
    gain = _rep(gain_ref[...], tn)
    r = _rep(r_ref[...], tn)
    g = r * jnp.dot(x, (wg_ref[...] * gain).astype(_BF16), preferred_element_type=_F32)
    u = r * jnp.dot(x, (wu_ref[...] * gain).astype(_BF16), preferred_element_type=_F32)
    o_ref[...] = (g * jax.nn.sigmoid(g) * u).astype(o_ref.dtype)


def _gate_up(xb, inv_rms, wg, wu, layer, gain, w_side, tm, tn):
    M, K = xb.shape
    N = wg.shape[2]
    nj = N // tn
    steps = (M // tm) * nj
    _, R, C = w_side.shape
    if R % steps or (R // steps) % BF16_SUBLANES:
        side_out = _cast_bf16(w_side, layer)
        side_args, side_in, side_spec, side_shape = [], [], [], []
    else:
        rows = R // steps
        side_args = [w_side]
        side_in = [pl.BlockSpec((None, rows, C), lambda i, j: (layer, i * nj + j, 0))]
        side_spec = [pl.BlockSpec((rows, C), lambda i, j: (i * nj + j, 0))]
        side_shape = [jax.ShapeDtypeStruct((R, C), _BF16)]
    w_spec = pl.BlockSpec((None, K, tn), lambda i, j: (layer, 0, j))
    outs = pl.pallas_call(
        _gate_up_kernel,
        grid=(M // tm, nj),
        in_specs=[_row_tile_spec(tm, K),
                  pl.BlockSpec((tm, LANES), lambda i, j: (i, 0)),
                  w_spec, w_spec,
                  pl.BlockSpec((K, LANES), lambda i, j: (0, 0))] + side_in,
        out_specs=[pl.BlockSpec((tm, tn), lambda i, j: (i, j))] + side_spec,
        out_shape=[jax.ShapeDtypeStruct((M, N), _BF16)] + side_shape,
        compiler_params=_params("arbitrary", "arbitrary"),
        name="ffn_gate_up",
    )(xb, inv_rms, wg, wu, _lane_rep(gain), *side_args)
    return (outs[0], outs[1]) if side_args else (outs[0], side_out)


def _down_kernel(x_ref, w_ref, res_ref, o_ref, *rest, width):
    *stat_refs, acc_ref = rest
    k = pl.program_id(1)
    j = pl.program_id(2)

    def parts():
        for cols in _col_groups(o_ref.shape[1]):
            yield cols, jnp.dot(x_ref[...], w_ref[:, cols], preferred_element_type=_F32)

    @pl.when(k == 0)
    def _():
        for cols, part in parts():
            acc_ref[j, :, cols] = part

    @pl.when(k == pl.num_programs(1) - 1)
    def _():
        ssq = None
        for cols, part in parts():
            y = res_ref[:, cols] + acc_ref[j, :, cols] + part
            ssq = _store_residual_tile(cols, y, o_ref, stat_refs, ssq)
        if stat_refs:
            _accumulate_inv_rms(stat_refs[1], ssq, j == 0, j == pl.num_programs(2) - 1, width)


def _down_residual(x, w, res, with_stats, tm, tk, tn):
    M, K = x.shape
    N = w.shape[1]
    nk = K // tk
    assert nk == 2
    tile = pl.BlockSpec((tm, tn), lambda i, k, j: (i, j * k))
    out_specs, out_shape = [tile], [jax.ShapeDtypeStruct((M, N), _F32)]
    if with_stats:
        out_specs += [tile, pl.BlockSpec((tm, LANES), lambda i, k, j: (i, 0))]
        out_shape += [jax.ShapeDtypeStruct((M, N), _BF16), jax.ShapeDtypeStruct((M, LANES), _F32)]
    return pl.pallas_call(
        functools.partial(_down_kernel, width=N),
        grid=(M // tm, nk, N // tn),
        in_specs=[pl.BlockSpec((tm, tk), lambda i, k, j: (i, k)),
                  pl.BlockSpec((tk, tn), lambda i, k, j: (k, j)),
                  tile],
        out_specs=out_specs,
        out_shape=out_shape,
        scratch_shapes=[pltpu.VMEM((N // tn, tm, tn), _F32)],
        compiler_params=_params("arbitrary", "arbitrary", "arbitrary"),
        name="ffn_down_residual",
    )(x, w, res)


def _diff_attn_kernel(q_ref, k_ref, v_ref, td_ref, ts_ref, lam_ref, gsub_ref, o_ref,
                      bias_ref, sx_ref, sy_ref, sz_ref, mx_ref, my_ref, mz_ref, m_ref, l_ref, acc_ref,
                      *, t, stage_blocks, lambda_init):
    i = pl.program_id(2)
    nt = t // LANES

    @pl.when(i == 0)
    def _build_bias():
        zeros = jnp.zeros((LANES, LANES), _F32)
        neg = jnp.full((LANES, LANES), NEG, _F32)
        for c in range(2):
            for a in range(nt):
                for b in range(nt):
                    rows = slice(a * LANES, (a + 1) * LANES)
                    cols = slice(b * LANES, (b + 1) * LANES)
                    if a == b:
                        tile = td_ref[c]
                    elif a == b + 1:
                        tile = ts_ref[c]
                    else:
                        tile = zeros if a > b else neg
                    bias_ref[c, 0, rows, cols] = tile
                    bias_ref[c, 1, rows, cols] = ts_ref[c] if (a == 0 and b == nt - 1) else zeros
                    bias_ref[c, 2, rows, cols] = neg

    def stage_a(blocks, s_out, m_out):
        for c in range(2):
            cols = slice(c * HEAD_DIM, (c + 1) * HEAD_DIM)
            part = None
            for n, (j, kind) in enumerate(blocks):
                start = pl.multiple_of(j * t, t)
                s = lax.dot_general(q_ref[:, cols], k_ref[pl.ds(start, t), cols], _NT,
                                    preferred_element_type=_F32)
                if kind is not None:
                    s = s + bias_ref[c, kind]
                s_out[c, :, n * t:(n + 1) * t] = s
                block_max = functools.reduce(jnp.maximum, _lane_tiles(s))
                part = block_max if part is None else jnp.maximum(part, block_max)
            m_out[c] = jnp.broadcast_to(jnp.max(part, axis=-1, keepdims=True), (t, LANES))

    def stage_b(blocks, s_in, m_in, first=False):
        rg = min(ATTN_ROW_GROUP, t)
        for c in range(2):
            for rows in [slice(a * rg, (a + 1) * rg) for a in range(t // rg)]:
                if first:
                    m_new = m_in[c, rows]
                    psum = jnp.zeros((rg, LANES), _F32)
                    pv = jnp.zeros((rg, acc_ref.shape[2]), _F32)
                else:
                    m_old = m_ref[c, rows]
                    m_new = jnp.maximum(m_old, m_in[c, rows])
                    alpha = jnp.exp2(m_old - m_new)
                    psum = alpha * l_ref[c, rows]
                    pv = _rep(alpha, acc_ref.shape[2]) * acc_ref[c, rows]
                m_rep = _rep(m_new, MXU_COLS)
                for n, (j, _) in enumerate(blocks):
                    for u in range(t // MXU_COLS):
                        off = u * MXU_COLS
                        p = jnp.exp2(s_in[c, rows, n * t + off:n * t + off + MXU_COLS] - m_rep)
                        psum = psum + functools.reduce(jnp.add, _lane_tiles(p))
                        v_rows = v_ref[pl.ds(pl.multiple_of(j * t + off, MXU_COLS), MXU_COLS), :]
                        pv = pv + jnp.dot(p.astype(_BF16), v_rows, preferred_element_type=_F32)
                l_ref[c, rows] = psum
                acc_ref[c, rows] = pv
                m_ref[c, rows] = m_new

    g = stage_blocks
    j_sub = jnp.maximum(i - 1, 0)
    n_far = jnp.maximum(i - 1, 0) // g
    left_over = jnp.maximum(i - 1, 0) % g
    near = [(i, 0), (j_sub, jnp.where(i == 0, 2, 1))] + [(jnp.maximum(i - 2 - n, 0), None) for n in range(g - 1)]

    def far(f):
        return [(g * f + n, None) for n in range(g)]

    for width in range(2, g + 2):
        is_width = left_over == width - 2

        @pl.when(is_width)
        def _(width=width):
            stage_a(near[:width], sz_ref, mz_ref)

        @pl.when(jnp.logical_and(is_width, n_far == 0))
        def _(width=width):
            stage_b(near[:width], sz_ref, mz_ref, first=True)

        @pl.when(jnp.logical_and(is_width, n_far > 0))
        def _(width=width):
            stage_a(far(0), sy_ref, my_ref)
            stage_b(near[:width], sz_ref, mz_ref, first=True)

    n_rest = jnp.maximum(n_far - 1, 0)
    n_pairs = n_rest // 2

    def pair_body(u, carry):
        stage_a(far(2 * u + 1), sx_ref, mx_ref)
        stage_b(far(2 * u), sy_ref, my_ref)
        stage_a(far(2 * u + 2), sy_ref, my_ref)
        stage_b(far(2 * u + 1), sx_ref, mx_ref)
        return carry

    lax.fori_loop(0, n_pairs, pair_body, 0)

    @pl.when(jnp.logical_and(n_far > 0, n_rest % 2 == 1))
    def _():
        stage_a(far(n_far - 1), sx_ref, mx_ref)
        stage_b(far(2 * n_pairs), sy_ref, my_ref)
        stage_b(far(n_far - 1), sx_ref, mx_ref)

    @pl.when(jnp.logical_and(n_far > 0, n_rest % 2 == 0))
    def _():
        stage_b(far(2 * n_pairs), sy_ref, my_ref)

    lf = lam_ref[...]
    lam = (jnp.exp(jnp.sum(lf[0:1] * lf[1:2], axis=-1, keepdims=True))
           - jnp.exp(jnp.sum(lf[2:3] * lf[3:4], axis=-1, keepdims=True)) + lambda_init)
    l0 = jnp.sum(l_ref[0], axis=-1, keepdims=True)
    l1 = jnp.sum(l_ref[1], axis=-1, keepdims=True)
    o = acc_ref[0] / l0 - lam * (acc_ref[1] / l1)
    o = o * lax.rsqrt(jnp.mean(o * o, axis=-1, keepdims=True) + EPS)
    o_ref[...] = (o * gsub_ref[...] * (1.0 - lambda_init)).astype(o_ref.dtype)


def _diff_attention(qkv, td, ts, lam_qk, g_sub, B, S, D, t, stage_blocks, lambda_init):
    n_heads = D // (2 * HEAD_DIM)
    nq = S // t
    w = 2 * HEAD_DIM
    kern = functools.partial(_diff_attn_kernel, t=t, stage_blocks=stage_blocks, lambda_init=lambda_init)
    return pl.pallas_call(
        kern,
        grid=(B, n_heads, nq),
        in_specs=[pl.BlockSpec((t, w), lambda b, h, i: (b * nq + i, h)),
                  pl.BlockSpec((S, w), lambda b, h, i: (b, n_heads + h)),
                  pl.BlockSpec((S, w), lambda b, h, i: (b, 2 * n_heads + h)),
                  pl.BlockSpec((2, LANES, LANES), lambda b, h, i: (h, 0, 0)),
                  pl.BlockSpec((2, LANES, LANES), lambda b, h, i: (h, 0, 0)),
                  pl.BlockSpec((4, HEAD_DIM), lambda b, h, i: (0, 0)),
                  pl.BlockSpec((1, w), lambda b, h, i: (0, 0))],
        out_specs=pl.BlockSpec((t, w), lambda b, h, i: (b * nq + i, h)),
        out_shape=jax.ShapeDtypeStruct((B * S, D), _BF16),
        scratch_shapes=[pltpu.VMEM((2, 3, t, t), _F32),
                        pltpu.VMEM((2, t, stage_blocks * t), _F32),
                        pltpu.VMEM((2, t, stage_blocks * t), _F32),
                        pltpu.VMEM((2, t, (stage_blocks + 1) * t), _F32),
                        pltpu.VMEM((2, t, LANES), _F32),
                        pltpu.VMEM((2, t, LANES), _F32),
                        pltpu.VMEM((2, t, LANES), _F32),
                        pltpu.VMEM((2, t, LANES), _F32),
                        pltpu.VMEM((2, t, LANES), _F32),
                        pltpu.VMEM((2, t, w), _F32)],
        compiler_params=_params("arbitrary", "arbitrary", "arbitrary"),
        name="diff_attention",
    )(qkv, qkv, qkv, td, ts, lam_qk, g_sub)


def _swa_kernel(q_ref, kc_ref, kp_ref, vc_ref, vp_ref, bias_ref, sink_ref, o_ref, *, tq):
    first = pl.program_id(2) == 0
    rows = SWA_GROUP * WINDOW
    bias = bias_ref[...].reshape(rows, 2 * WINDOW)
    sink = jnp.concatenate(
        [jnp.broadcast_to(sink_ref[r:r + 1, :], (WINDOW, LANES)) for r in range(SWA_GROUP)], axis=0)
    col = lax.broadcasted_iota(jnp.int32, (rows, 2 * WINDOW), 1)
    ones = jnp.ones((2 * WINDOW, LANES), _BF16)
    for c in range(tq // WINDOW):
        cur = slice(c * WINDOW, (c + 1) * WINDOW)
        if c == 0:
            k_prev, v_prev = kp_ref[...], vp_ref[...]
        else:
            prev = slice((c - 1) * WINDOW, c * WINDOW)
            k_prev, v_prev = kc_ref[prev, :], vc_ref[prev, :]
        kk = jnp.concatenate([k_prev, kc_ref[cur, :]], axis=0)
        vv = jnp.concatenate([jnp.concatenate([v_prev, vc_ref[cur, :]], axis=0), ones], axis=1)
        q4 = jnp.concatenate(
            [q_ref[cur, r * HEAD_DIM:(r + 1) * HEAD_DIM] for r in range(SWA_GROUP)], axis=0)
        s = lax.dot_general(q4, kk, _NT, preferred_element_type=_F32) + bias
        if c == 0:
            s = jnp.where(first & (col < WINDOW), NEG, s)
        part = functools.reduce(jnp.maximum, _lane_tiles(s))
        m = jnp.maximum(jnp.broadcast_to(jnp.max(part, axis=-1, keepdims=True), (rows, LANES)), sink)
        p = jnp.exp2(s - jnp.concatenate([m, m], axis=1))
        pv = jnp.dot(p.astype(_BF16), vv, preferred_element_type=_F32)
        o = pv[:, :HEAD_DIM] / (pv[:, HEAD_DIM:] + jnp.exp2(sink - m))
        for r in range(SWA_GROUP):
            o_ref[cur, r * HEAD_DIM:(r + 1) * HEAD_DIM] = (
                o[r * WINDOW:(r + 1) * WINDOW, :].astype(o_ref.dtype))


def _swa_attention(q, kv, bias, sinks, B, S, D, tq):
    n_kv = D // (SWA_GROUP * HEAD_DIM)
    nt = S // tq
    bpt = tq // WINDOW
    gw = SWA_GROUP * HEAD_DIM

    def prev_map(off):
        return lambda b, g, t: (jnp.maximum((b * nt + t) * bpt - 1, 0), off + g)

    kern = functools.partial(_swa_kernel, tq=tq)
    return pl.pallas_call(
        kern,
        grid=(B, n_kv, nt),
        in_specs=[pl.BlockSpec((tq, gw), lambda b, g, t: (b * nt + t, g)),
                  pl.BlockSpec((tq, HEAD_DIM), lambda b, g, t: (b * nt + t, g)),
                  pl.BlockSpec((WINDOW, HEAD_DIM), prev_map(0)),
                  pl.BlockSpec((tq, HEAD_DIM), lambda b, g, t: (b * nt + t, n_kv + g)),
                  pl.BlockSpec((WINDOW, HEAD_DIM), prev_map(n_kv)),
                  pl.BlockSpec((SWA_GROUP, WINDOW, 2 * WINDOW), lambda b, g, t: (g, 0, 0)),
                  pl.BlockSpec((None, SWA_GROUP, LANES), lambda b, g, t: (g, 0, 0))],
        out_specs=pl.BlockSpec((tq, gw), lambda b, g, t: (b * nt + t, g)),
        out_shape=jax.ShapeDtypeStruct((B * S, D), _BF16),
        compiler_params=_params("arbitrary", "arbitrary", "arbitrary"),
        name="swa_attention",
    )(q, kv, kv, kv, kv, bias, sinks)


def _t5_bucket(n):
    max_exact = NUM_BUCKETS // 2
    nf = jnp.maximum(n, 1).astype(_F32)
    large = max_exact + (jnp.log(nf / max_exact) / math.log(MAX_DISTANCE / max_exact)
                         * (NUM_BUCKETS - max_exact)).astype(jnp.int32)
    large = jnp.minimum(large, NUM_BUCKETS - 1)
    return jnp.where(n < max_exact, n, large)


def _bias_tiles(rel_bias):
    def lookup(table, idx):
        onehot = (idx[..., None] == jnp.arange(table.shape[0])).astype(_F32)
        return jnp.einsum("...d,dh->h...", onehot, table, precision=lax.Precision.HIGHEST)

    lut = lookup(rel_bias.astype(_F32), _t5_bucket(jnp.arange(MAX_DISTANCE))).T * LOG2E
    shifted = lut - lut[MAX_DISTANCE - 1]
    qi = jnp.arange(LANES)[:, None]
    kj = jnp.arange(LANES)[None, :]
    clip = lambda dist: jnp.clip(dist, 0, MAX_DISTANCE - 1)
    d0 = qi - kj
    td = jnp.where((d0 >= 0)[None], lookup(shifted, clip(d0)), NEG)
    ts = lookup(shifted, clip(LANES + qi - kj))
    dw = WINDOW + qi - jnp.arange(2 * WINDOW)[None, :]
    swa = jnp.where(((dw >= 0) & (dw < WINDOW))[None], lookup(lut, clip(dw)), NEG)
    return td, ts, swa


def _forward(x, rel_bias, g_attn_norm, g_ffn_norm, w_qkv_a, w_o_a, g_q_a, g_k_a, lam_qk_a, g_sub_a,
             g_kv_norm, w_kv, g_k_shared, w_q_b, w_o_b, g_q_b, sinks_b, w_gate, w_up, w_down, tiles):
    B, S, D = x.shape
    M = B * S
    T = tiles
    q_scale = HEAD_DIM ** -0.5 * LOG2E
    n_maps = D // HEAD_DIM
    kvw = w_kv.shape[1] // 2
    td, ts, swa_bias = _bias_tiles(rel_bias)
    bf = _cast_bf16
    x = x.reshape(M, D)

    def ffn(x, xb, inv_rms, layer, with_stats):
        act, w_down_bf16 = _gate_up(xb, inv_rms, w_gate, w_up, layer, g_ffn_norm[layer], w_down,
                                    T["mm_tm"], T["gu_tn"])
        return _down_residual(act, w_down_bf16, x, with_stats, T["down_tm"], T["down_tk"], T["down_tn"])

    lambda_init = 0.8 - 0.6 * math.exp(-0.3 * 0)
    xb, inv_rms = _prep(x, T["rms_tm"])
    g_cols = jnp.concatenate([jnp.tile(g_q_a[0] * q_scale, n_maps), jnp.tile(g_k_a[0], n_maps),
                              jnp.ones((D,), _F32)])[None]
    qkv = _proj_headnorm(xb, inv_rms, bf(w_qkv_a, 0, g_attn_norm[0]), g_cols, 2 * D, T["mm_tm"], T["qkv_tn"])
    o = _diff_attention(qkv, td, ts, lam_qk_a[0], g_sub_a[0][None], B, S, D, T["attn_t"], T["attn_stage_blocks"],
                        lambda_init)
    x, xb, inv_rms = _proj_residual(o, bf(w_o_a, 0), x, T["mm_tm"], T["res_tn"])
    x, xb, inv_rms = ffn(x, xb, inv_rms, 0, True)

    g_kv_cols = jnp.concatenate([jnp.tile(g_k_shared, kvw // HEAD_DIM), jnp.ones((kvw,), _F32)])[None]
    kv = _proj_headnorm(xb, inv_rms, bf(w_kv[None], 0, g_kv_norm), g_kv_cols, kvw,
                        T["mm_tm"], min(T["qkv_tn"], kvw))
    q = _proj_headnorm(xb, inv_rms, bf(w_q_b, 0, g_attn_norm[1]), jnp.tile(g_q_b[0] * q_scale, n_maps)[None],
                       D, T["mm_tm"], T["qkv_tn"])
    sinks = jnp.broadcast_to((sinks_b[0].astype(_F32) * LOG2E).reshape(-1, SWA_GROUP, 1),
                             (n_maps // SWA_GROUP, SWA_GROUP, LANES))
    o = _swa_attention(q, kv, swa_bias, sinks, B, S, D, T["swa_tq"])
    x, xb, inv_rms = _proj_residual(o, bf(w_o_b, 0), x, T["mm_tm"], T["res_tn"])
    (x,) = ffn(x, xb, inv_rms, 1, False)
    return x.reshape(B, S, D)


def kernel(x, rel_bias, g_attn_norm, g_ffn_norm, w_qkv_a, w_o_a, g_q_a, g_k_a, lam_qk_a, g_sub_a,
           g_kv_norm, w_kv, g_k_shared, w_q_b, w_o_b, g_q_b, sinks_b, w_gate, w_up, w_down):
    B, S, D = x.shape
    tiles = _tiles(B * S, S, D, w_gate.shape[-1])
    return _forward(x, rel_bias, g_attn_norm, g_ffn_norm, w_qkv_a, w_o_a, g_q_a, g_k_a, lam_qk_a,
                    g_sub_a, g_kv_norm, w_kv, g_k_shared, w_q_b, w_o_b, g_q_b, sinks_b,
                    w_gate, w_up, w_down, tiles)
```
